```python
import jax, jax.numpy as jnp
from jax import lax
import numpy as np

D_MODEL = 2048
BATCH = 2
SEQ = 4096
DEPTH = 1
DEC_BATCH = 32
DEC_SEQ = 4
PAST_LEN = 16384
PAGE_SIZE = 128

NSA_HEADS = 16
NSA_GROUPS = 4
NSA_HPG = NSA_HEADS // NSA_GROUPS
NSA_DK = 128
NSA_SCALE = NSA_DK ** -0.5
CMP_LEN = 32
CMP_STRIDE = 16
CMP_HIDDEN = 256
SEL_BLOCK = 64
N_SEL = 16
WINDOW = 512
QUERY_BLOCK = 128
GLA_HEADS = 4
GLA_DK = 256
GLA_DV = 512
GLA_RANK = 16
GLA_TAU = 16.0
GLA_CHUNK = 64
PEER_HEADS = 8
PEER_NKEYS = 128
PEER_EXPERTS = PEER_NKEYS * PEER_NKEYS
PEER_QDIM = 256
PEER_HALF = PEER_QDIM // 2
PEER_TOPK = 16
PEER_BLOCK = 128

NORM_EPS = 1e-6
NEG = -1e30
BIG = 1e30

IN_SPLITS = (NSA_HEADS * NSA_DK,) + (NSA_GROUPS * NSA_DK,) * 6 + (
    3 * NSA_HEADS, GLA_HEADS * GLA_DK, GLA_HEADS * GLA_DK, GLA_HEADS * GLA_DV,
    GLA_HEADS * GLA_DV, GLA_RANK, D_MODEL, D_MODEL)
IN_COLS = sum(IN_SPLITS)

kernel_name = 'nsa_gla_peer_hybrid_step'


def rmsnorm(x, g):
    xf = x.astype(jnp.float32)
    y = xf * lax.rsqrt(jnp.mean(xf * xf, axis=-1, keepdims=True) + NORM_EPS)
    return (y * g.astype(jnp.float32)).astype(x.dtype)


def project_inputs(h, w_in, gla_w_lr2, gla_b_lr):
    lead = h.shape[:-1]
    z = h @ w_in
    offs = np.cumsum(np.array(IN_SPLITS))[:-1].tolist()
    (q, kc, vc, ks, vs, kw, vw, g_nsa, gq, gk, gv, gr, glr, m_a, m_b) = jnp.split(z, offs, axis=-1)

    def heads(t, n, d):
        return t.reshape(lead + (n, d))

    log_a = jax.nn.log_sigmoid((glr @ gla_w_lr2 + gla_b_lr).astype(jnp.float32)) / GLA_TAU
    return (heads(q, NSA_HEADS, NSA_DK), heads(kc, NSA_GROUPS, NSA_DK), heads(vc, NSA_GROUPS, NSA_DK),
            heads(ks, NSA_GROUPS, NSA_DK), heads(vs, NSA_GROUPS, NSA_DK),
            heads(kw, NSA_GROUPS, NSA_DK), heads(vw, NSA_GROUPS, NSA_DK),
            jax.nn.sigmoid(g_nsa).reshape(lead + (NSA_HEADS, 3)),
            heads(gq, GLA_HEADS, GLA_DK), heads(gk, GLA_HEADS, GLA_DK), heads(gv, GLA_HEADS, GLA_DV),
            gr, heads(log_a, GLA_HEADS, GLA_DK), m_a, m_b)


def half_proj(rows, w1):
    t = rows.shape[-3]
    halves = rows.reshape(rows.shape[:-3] + (t // CMP_STRIDE, CMP_STRIDE) + rows.shape[-2:])
    w = w1.reshape((CMP_LEN // CMP_STRIDE, CMP_STRIDE) + w1.shape[1:])
    first = jnp.einsum('...nrgd,rdh->...ngh', halves, w[0])
    second = jnp.einsum('...nrgd,rdh->...ngh', halves, w[1])
    return first, second


def compress(first, second, pe, w1, w2):
    pre = first[..., :-1, :, :] + second[..., 1:, :, :] + jnp.einsum('rd,rdh->h', pe, w1)
    return jax.nn.gelu(pre) @ w2


def cmp_attend_select(qg, pos, kc, vc, seq_len):
    nc = kc.shape[0]
    s = jnp.einsum('tghd,cgd->tghc', qg, kc).astype(jnp.float32) * NSA_SCALE
    end = jnp.arange(nc) * CMP_STRIDE + CMP_LEN - 1
    mask = (end[None, :] <= pos[:, None])[:, None, None, :]
    p = jax.nn.softmax(jnp.where(mask, s, NEG), axis=-1) * mask
    o = jnp.einsum('tghc,cgd->tghd', p.astype(vc.dtype), vc)
    pg = p.sum(axis=2)
    z = jnp.zeros_like(pg[..., :1])
    ph = jnp.concatenate([pg, z], -1) + jnp.concatenate([z, pg], -1)
    n_blk = max(-(-seq_len // SEL_BLOCK), N_SEL)
    hpb = SEL_BLOCK // CMP_STRIDE
    ph = jnp.pad(ph, ((0, 0), (0, 0), (0, n_blk * hpb - ph.shape[-1])))
    ps = ph.reshape(ph.shape[:-1] + (n_blk, hpb)).sum(-1)
    blk = jnp.arange(n_blk)[None, :]
    cur = (pos // SEL_BLOCK)[:, None]
    valid = blk <= cur
    forced = (blk == 0) | (blk == cur) | (blk == cur - 1)
    score = jnp.where(forced[:, None, :], BIG, jnp.where(valid[:, None, :], ps, NEG))
    _, idx = lax.top_k(score, N_SEL)
    tok = (idx[..., None] * SEL_BLOCK + jnp.arange(SEL_BLOCK)).reshape(idx.shape[:-1] + (N_SEL * SEL_BLOCK,))
    return o, tok


def sel_attend(qg, pos, k_sel, v_sel, tok):
    s = jnp.einsum('tghd,tgsd->tghs', qg, k_sel).astype(jnp.float32) * NSA_SCALE
    mask = (tok <= pos[:, None, None])[:, :, None, :]
    p = jax.nn.softmax(jnp.where(mask, s, NEG), axis=-1)
    return jnp.einsum('tghs,tgsd->tghd', p.astype(v_sel.dtype), v_sel)


def win_attend(qg, pos, kw, vw, kpos):
    s = jnp.einsum('tghd,wgd->tghw', qg, kw).astype(jnp.float32) * NSA_SCALE
    d = pos[:, None] - kpos[None, :]
    mask = ((d >= 0) & (d <= WINDOW) & (kpos[None, :] >= 0))[:, None, None, :]
    p = jax.nn.softmax(jnp.where(mask, s, NEG), axis=-1)
    return jnp.einsum('tghw,wgd->tghd', p.astype(vw.dtype), vw)


def nsa_combine(o_c, o_s, o_w, gates):
    tq = o_c.shape[0]
    o = jnp.stack([o_c, o_s, o_w], axis=-1).reshape(tq, NSA_HEADS, NSA_DK, 3)
    return jnp.einsum('thdc,thc->thd', o, gates.astype(o.dtype)).reshape(tq, NSA_HEADS * NSA_DK)


def gather_rows(rows, tok):
    tok = jnp.clip(tok, 0, rows.shape[0] - 1)
    return rows[tok, jnp.arange(NSA_GROUPS)[None, :, None]]


def gather_paged(pool, page_row, new_rows, tok):
    past = page_row.shape[0] * PAGE_SIZE
    tok = jnp.clip(tok, 0, past + new_rows.shape[0] - 1)
    tp = jnp.minimum(tok, past - 1)
    g = jnp.arange(NSA_GROUPS)[None, :, None]
    from_past = pool[page_row[tp // PAGE_SIZE], tp % PAGE_SIZE, g]
    from_new = new_rows[jnp.clip(tok - past, 0, new_rows.shape[0] - 1), g]
    return jnp.where((tok < past)[..., None], from_past, from_new)


def nsa_prompt(q, kc, vc, ks, vs, kw, vw, gates):
    b_sz, t_len = q.shape[:2]
    nqb = t_len // QUERY_BLOCK
    kw_pad = jnp.pad(kw, ((0, 0), (WINDOW, 0), (0, 0), (0, 0)))
    vw_pad = jnp.pad(vw, ((0, 0), (WINDOW, 0), (0, 0), (0, 0)))
    q_blocks = q.reshape(b_sz * nqb, QUERY_BLOCK, NSA_GROUPS, NSA_HPG, NSA_DK)
    g_blocks = gates.reshape(b_sz * nqb, QUERY_BLOCK, NSA_HEADS, 3)
    b_idx = jnp.repeat(jnp.arange(b_sz), nqb)
    starts = jnp.tile(jnp.arange(nqb) * QUERY_BLOCK, b_sz)

    def block(args):
        qb, gb, b, start = args
        pos = start + jnp.arange(QUERY_BLOCK)
        o_c, tok = cmp_attend_select(qb, pos, kc[b], vc[b], t_len)
        o_s = sel_attend(qb, pos, gather_rows(ks[b], tok), gather_rows(vs[b], tok), tok)
        kwb = lax.dynamic_slice_in_dim(kw_pad[b], start, QUERY_BLOCK + WINDOW, axis=0)
        vwb = lax.dynamic_slice_in_dim(vw_pad[b], start, QUERY_BLOCK + WINDOW, axis=0)
        kpos = start - WINDOW + jnp.arange(QUERY_BLOCK + WINDOW)
        o_w = win_attend(qb, pos, kwb, vwb, kpos)
        return nsa_combine(o_c, o_s, o_w, gb)

    o = lax.map(block, (q_blocks, g_blocks, b_idx, starts))
    return o.reshape(b_sz, t_len, NSA_HEADS * NSA_DK)


def compress_paged(pool, page_table, new_rows, pe, w1, w2):
    bd, n_pages = page_table.shape
    past_rows = pool[page_table].reshape(bd, n_pages * PAGE_SIZE, NSA_GROUPS, NSA_DK)
    fp, sp = half_proj(past_rows, w1)
    n_new = (new_rows.shape[1] // CMP_STRIDE) * CMP_STRIDE
    fn, sn = half_proj(new_rows[:, :n_new], w1)
    return compress(jnp.concatenate([fp, fn], 1), jnp.concatenate([sp, sn], 1), pe, w1, w2)


def nsa_sample(q, kc, vc, ks_new, vs_new, kw_all, vw_all, gates, pool_k, pool_v, page_table):
    bd, s_len = q.shape[:2]
    past = page_table.shape[1] * PAGE_SIZE
    total = past + s_len
    wb = kw_all.shape[1] - s_len
    pos = past + jnp.arange(s_len)
    kpos = past - wb + jnp.arange(wb + s_len)
    qg = q.reshape(bd, s_len, NSA_GROUPS, NSA_HPG, NSA_DK)

    def one(qb, gb, kcb, vcb, prow, ksb, vsb, kwb, vwb, pk, pv):
        o_c, tok = cmp_attend_select(qb, pos, kcb, vcb, total)
        o_s = sel_attend(qb, pos, gather_paged(pk, prow, ksb, tok), gather_paged(pv, prow, vsb, tok), tok)
        o_w = win_attend(qb, pos, kwb, vwb, kpos)
        return nsa_combine(o_c, o_s, o_w, gb)

    return jax.vmap(one, in_axes=(0,) * 9 + (None, None))(
        qg, gates, kc, vc, page_table, ks_new, vs_new, kw_all, vw_all, pool_k, pool_v)


def gla_chunked(q, k, v, log_a, s0, chunk):
    b_sz, t_len, nh, dk = q.shape
    dv = v.shape[-1]
    n = t_len // chunk

    def to_chunks(t):
        return jnp.moveaxis(t.astype(jnp.float32).reshape((b_sz, n, chunk) + t.shape[2:]), 1, 0)

    causal = jnp.tril(jnp.ones((chunk, chunk), dtype=bool))[None, :, :, None, None]

    def step(S, inp):
        qc, kc, vc, ac = inp
        b = jnp.cumsum(ac, axis=1)
        diff = b[:, :, None] - b[:, None, :]
        dec = jnp.where(causal, jnp.exp(jnp.where(causal, diff, 0.0)), 0.0)
        A = jnp.einsum('bthk,bshk,btshk->bhts', qc, kc, dec)
        o = jnp.einsum('bhts,bshv->bthv', A, vc) + jnp.einsum('bthk,bhkv->bthv', qc * jnp.exp(b), S)
        b_last = b[:, -1]
        S = jnp.exp(b_last)[..., None] * S + jnp.einsum('bshk,bshv->bhkv', kc * jnp.exp(b_last[:, None] - b), vc)
        return S, o

    S, o = lax.scan(step, s0.astype(jnp.float32),
                    (to_chunks(q * (dk ** -0.5)), to_chunks(k), to_chunks(v), to_chunks(log_a)))
    o = jnp.moveaxis(o, 0, 1).reshape(b_sz, t_len, nh, dv)
    return o.astype(v.dtype), S.astype(s0.dtype)


def gla_branch(gq, gk, gv, gr, log_a, s0, chunk, gla_norm_g):
    o, S = gla_chunked(gq, gk, gv, log_a, s0, chunk)
    o = rmsnorm(o, gla_norm_g)
    o = o.reshape(o.shape[:-2] + (GLA_HEADS * GLA_DV,)) * jax.nn.silu(gr)
    return o, S


def merge_branches(o_nsa, o_gla, m_a, m_b, w_nsa_proj, w_gla_proj, w_out):
    y = jax.nn.sigmoid(m_a) * (o_nsa @ w_nsa_proj) + jax.nn.sigmoid(m_b) * (o_gla @ w_gla_proj)
    return y @ w_out


def peer(h, w_q, keys1, keys2, u, v):
    lead = h.shape[:-1]
    xf = h.reshape(-1, D_MODEL)
    n = xf.shape[0]
    nb = -(-n // PEER_BLOCK)
    xpad = jnp.pad(xf, ((0, nb * PEER_BLOCK - n), (0, 0)))

    def block(xb):
        q = (xb @ w_q).reshape(PEER_BLOCK, PEER_HEADS, 2, PEER_HALF)
        s1 = jnp.einsum('nhd,hkd->nhk', q[:, :, 0], keys1).astype(jnp.float32)
        s2 = jnp.einsum('nhd,hkd->nhk', q[:, :, 1], keys2).astype(jnp.float32)
        v1, i1 = lax.top_k(s1, PEER_TOPK)
        v2, i2 = lax.top_k(s2, PEER_TOPK)
        cand = (v1[..., :, None] + v2[..., None, :]).reshape(PEER_BLOCK, PEER_HEADS, PEER_TOPK * PEER_TOPK)
        cid = (i1[..., :, None] * PEER_NKEYS + i2[..., None, :]).reshape(PEER_BLOCK, PEER_HEADS, PEER_TOPK * PEER_TOPK)
        sc, j = lax.top_k(cand, PEER_TOPK)
        eid = jnp.take_along_axis(cid, j, axis=-1)
        g = jax.nn.softmax(sc, axis=-1)
        ue = u[eid]
        ve = v[eid]
        act = jax.nn.gelu(jnp.einsum('nd,nhkd->nhk', xb, ue)).astype(jnp.float32)
        return jnp.einsum('nhk,nhkd->nd', (g * act).astype(ve.dtype), ve)

    y = lax.map(block, xpad.reshape(nb, PEER_BLOCK, D_MODEL)).reshape(nb * PEER_BLOCK, D_MODEL)[:n]
    return y.reshape(lead + (D_MODEL,))


def setup_inputs(seed: int = 0) -> dict:
    key = jax.random.key(seed)
    ks = jax.random.split(key, 40)
    f32 = jnp.float32
    n_pages = PAST_LEN // PAGE_SIZE
    n_used = DEC_BATCH * n_pages
    n_pool = n_used + max(1, n_used // 4)
    page_table = jax.random.permutation(ks[0], n_pool)[:n_used].reshape(DEC_BATCH, n_pages).astype(jnp.int32)
    wb = min(WINDOW, PAST_LEN)

    def nrm(i, shape, scale=1.0):
        return jax.random.normal(ks[i], shape, f32) * scale

    pool_shape = (n_pool, PAGE_SIZE, NSA_GROUPS, NSA_DK)
    return {
        'x_prompt': nrm(1, (BATCH, SEQ, D_MODEL)),
        'x_sample': nrm(2, (DEC_BATCH, DEC_SEQ, D_MODEL)),
        'cache_k_cmp': nrm(3, pool_shape),
        'cache_v_cmp': nrm(4, pool_shape),
        'cache_k_slc': nrm(5, pool_shape),
        'cache_v_slc': nrm(6, pool_shape),
        'cache_k_win': nrm(7, (DEC_BATCH, wb, NSA_GROUPS, NSA_DK)),
        'cache_v_win': nrm(8, (DEC_BATCH, wb, NSA_GROUPS, NSA_DK)),
        'state_gla': nrm(9, (DEC_BATCH, GLA_HEADS, GLA_DK, GLA_DV)),
        'page_table': page_table,
        'norm1_g': 1.0 + nrm(10, (D_MODEL,), 0.01),
        'w_in': nrm(11, (D_MODEL, IN_COLS), D_MODEL ** -0.5),
        'cmp_pe_k': nrm(12, (CMP_LEN, NSA_DK), 0.1),
        'cmp_w1_k': nrm(13, (CMP_LEN, NSA_DK, CMP_HIDDEN), (CMP_LEN * NSA_DK) ** -0.5),
        'cmp_w2_k': nrm(14, (CMP_HIDDEN, NSA_DK), CMP_HIDDEN ** -0.5),
        'cmp_pe_v': nrm(15, (CMP_LEN, NSA_DK), 0.1),
        'cmp_w1_v': nrm(16, (CMP_LEN, NSA_DK, CMP_HIDDEN), (CMP_LEN * NSA_DK) ** -0.5),
        'cmp_w2_v': nrm(17, (CMP_HIDDEN, NSA_DK), CMP_HIDDEN ** -0.5),
        'gla_w_lr2': nrm(18, (GLA_RANK, GLA_HEADS * GLA_DK), GLA_RANK ** -0.5),
        'gla_b_lr': nrm(19, (GLA_HEADS * GLA_DK,), 0.1),
        'gla_norm_g': 1.0 + nrm(20, (GLA_DV,), 0.01),
        'w_nsa_proj': nrm(21, (NSA_HEADS * NSA_DK, D_MODEL), (NSA_HEADS * NSA_DK) ** -0.5),
        'w_gla_proj': nrm(22, (GLA_HEADS * GLA_DV, D_MODEL), (GLA_HEADS * GLA_DV) ** -0.5),
        'w_out': nrm(23, (D_MODEL, D_MODEL), D_MODEL ** -0.5),
        'norm2_g': 1.0 + nrm(24, (D_MODEL,), 0.01),
        'peer_w_q': nrm(25, (D_MODEL, PEER_HEADS * PEER_QDIM), D_MODEL ** -0.5),
        'peer_keys1': nrm(26, (PEER_HEADS, PEER_NKEYS, PEER_HALF), PEER_HALF ** -0.5),
        'peer_keys2': nrm(27, (PEER_HEADS, PEER_NKEYS, PEER_HALF), PEER_HALF ** -0.5),
        'peer_u': nrm(28, (PEER_EXPERTS, D_MODEL), D_MODEL ** -0.5),
        'peer_v': nrm(29, (PEER_EXPERTS, D_MODEL), PEER_HEADS ** -0.5),
        'norm_f_g': 1.0 + nrm(30, (D_MODEL,), 0.01),
    }


def reference(x_prompt, x_sample, cache_k_cmp, cache_v_cmp, cache_k_slc, cache_v_slc, cache_k_win,
              cache_v_win, state_gla, page_table, norm1_g, w_in, cmp_pe_k, cmp_w1_k, cmp_w2_k, cmp_pe_v,
              cmp_w1_v, cmp_w2_v, gla_w_lr2, gla_b_lr, gla_norm_g, w_nsa_proj, w_gla_proj, w_out, norm2_g,
              peer_w_q, peer_keys1, peer_keys2, peer_u, peer_v, norm_f_g):
    b_sz, t_len, _ = x_prompt.shape
    h = rmsnorm(x_prompt, norm1_g)
    (q, kcr_p, vcr_p, ksr_p, vsr_p, kwr_p, vwr_p, g_nsa, gq, gk, gv, gr, log_a, m_a, m_b) = project_inputs(
        h, w_in, gla_w_lr2, gla_b_lr)
    f, s = half_proj(kcr_p, cmp_w1_k)
    kc = compress(f, s, cmp_pe_k, cmp_w1_k, cmp_w2_k)
    f, s = half_proj(vcr_p, cmp_w1_v)
    vc = compress(f, s, cmp_pe_v, cmp_w1_v, cmp_w2_v)
    o_nsa = nsa_prompt(q, kc, vc, ksr_p, vsr_p, kwr_p, vwr_p, g_nsa)
    s0 = jnp.zeros((b_sz, GLA_HEADS, GLA_DK, GLA_DV), jnp.float32)
    o_gla, gla_state_p = gla_branch(gq, gk, gv, gr, log_a, s0, GLA_CHUNK, gla_norm_g)
    x = x_prompt + merge_branches(o_nsa, o_gla, m_a, m_b, w_nsa_proj, w_gla_proj, w_out)
    x = x + peer(rmsnorm(x, norm2_g), peer_w_q, peer_keys1, peer_keys2, peer_u, peer_v)
    y_prompt = rmsnorm(x, norm_f_g)
    wl = min(WINDOW, t_len)
    k_win_p = kwr_p[:, t_len - wl:]
    v_win_p = vwr_p[:, t_len - wl:]

    s_len = x_sample.shape[1]
    h = rmsnorm(x_sample, norm1_g)
    (q, kcr_s, vcr_s, ksr_s, vsr_s, kwr_s, vwr_s, g_nsa, gq, gk, gv, gr, log_a, m_a, m_b) = project_inputs(
        h, w_in, gla_w_lr2, gla_b_lr)
    kc = compress_paged(cache_k_cmp, page_table, kcr_s, cmp_pe_k, cmp_w1_k, cmp_w2_k)
    vc = compress_paged(cache_v_cmp, page_table, vcr_s, cmp_pe_v, cmp_w1_v, cmp_w2_v)
    kw_all = jnp.concatenate([cache_k_win, kwr_s], axis=1)
    vw_all = jnp.concatenate([cache_v_win, vwr_s], axis=1)
    o_nsa = nsa_sample(q, kc, vc, ksr_s, vsr_s, kw_all, vw_all, g_nsa, cache_k_slc, cache_v_slc, page_table)
    o_gla, gla_state_s = gla_branch(gq, gk, gv, gr, log_a, state_gla, s_len, gla_norm_g)
    x = x_sample + merge_branches(o_nsa, o_gla, m_a, m_b, w_nsa_proj, w_gla_proj, w_out)
    x = x + peer(rmsnorm(x, norm2_g), peer_w_q, peer_keys1, peer_keys2, peer_u, peer_v)
    y_sample = rmsnorm(x, norm_f_g)
    wb = cache_k_win.shape[1]
    k_win_s = kw_all[:, s_len:s_len + wb]
    v_win_s = vw_all[:, s_len:s_len + wb]

    return (y_prompt, y_sample, kcr_p, vcr_p, ksr_p, vsr_p, k_win_p, v_win_p, gla_state_p,
            kcr_s, vcr_s, ksr_s, vsr_s, k_win_s, v_win_s, gla_state_s)
```

```python
import functools

import numpy as np
import jax
import jax.numpy as jnp
from jax import lax
from jax.experimental import pallas as pl
from jax.experimental.pallas import tpu as pltpu

F32 = jnp.float32
BF16 = jnp.bfloat16

D_MODEL = 2048
PAGE_SIZE = 128
NSA_HEADS = 16
NSA_GROUPS = 4
NSA_HPG = NSA_HEADS // NSA_GROUPS
NSA_DK = 128
NSA_SCALE = NSA_DK ** -0.5
CMP_LEN = 32
CMP_STRIDE = 16
CMP_HIDDEN = 256
SEL_BLOCK = 64
N_SEL = 16
WINDOW = 512
QUERY_BLOCK = 128
GLA_HEADS = 4
GLA_DK = 256
GLA_DV = 512
GLA_RANK = 16
GLA_TAU = 16.0
GLA_CHUNK = 64
PEER_HEADS = 8
PEER_NKEYS = 128
PEER_HALF = 128
PEER_TOPK = 16
NORM_EPS = 1e-6
NEG = -1e30
BIG = 1e30
PAD_SCORE = -3e38
REMOVED = -float("inf")

IN_SPLITS = (NSA_HEADS * NSA_DK,) + (NSA_GROUPS * NSA_DK,) * 6 + (
    3 * NSA_HEADS, GLA_HEADS * GLA_DK, GLA_HEADS * GLA_DK, GLA_HEADS * GLA_DV,
    GLA_HEADS * GLA_DV, GLA_RANK, D_MODEL, D_MODEL)

GD = NSA_GROUPS * NSA_DK
C_Q = 0
C_KC = C_Q + NSA_HEADS * NSA_DK
C_VC = C_KC + GD
C_KS = C_VC + GD
C_VS = C_KS + GD
C_KW = C_VS + GD
C_VW = C_KW + GD
C_GQ = C_VW + GD
C_GK = C_GQ + GLA_HEADS * GLA_DK
C_GV = C_GK + GLA_HEADS * GLA_DK
C_GR = C_GV + GLA_HEADS * GLA_DV
C_MA = C_GR + GLA_HEADS * GLA_DV
C_MB = C_MA + D_MODEL
C_TAIL = C_MB + D_MODEL
Z_COLS = C_TAIL + 512
GLR_LANE = 16
LANES = 128

VMEM_LIMIT = 56 * 1024 * 1024


def _params(sem):
    return pltpu.CompilerParams(dimension_semantics=sem, vmem_limit_bytes=VMEM_LIMIT)


def _dot(a, b):
    return jnp.dot(a, b, preferred_element_type=F32)


def _dot_nt(a, b):
    return lax.dot_general(a, b, (((1,), (1,)), ((), ())), preferred_element_type=F32)


def _dot_tn(a, b):
    return lax.dot_general(a, b, (((0,), (0,)), ((), ())), preferred_element_type=F32)


def _dot_exact(a, b):
    return jnp.dot(a, b, preferred_element_type=F32, precision=lax.Precision.HIGHEST)


def _rms(x, g):
    return x * lax.rsqrt(jnp.mean(x * x, axis=-1, keepdims=True) + NORM_EPS) * g


def _norm_matmul_kernel(x_ref, g_ref, w_ref, o_ref, h_ref):
    @pl.when(pl.program_id(1) == 0)
    def _():
        h_ref[...] = _rms(x_ref[...], g_ref[...]).astype(BF16)

    o_ref[...] = _dot(h_ref[...], w_ref[...])


def norm_matmul(x, g, w, tm, tn):
    n, d = x.shape
    cols = w.shape[1]
    return pl.pallas_call(
        _norm_matmul_kernel,
        out_shape=jax.ShapeDtypeStruct((n, cols), F32),
        grid=(n // tm, cols // tn),
        in_specs=[pl.BlockSpec((tm, d), lambda i, j: (i, 0)),
                  pl.BlockSpec((1, d), lambda i, j: (0, 0)),
                  pl.BlockSpec((d, tn), lambda i, j: (0, j))],
        out_specs=pl.BlockSpec((tm, tn), lambda i, j: (i, j)),
        scratch_shapes=[pltpu.VMEM((tm, d), BF16)],
        compiler_params=_params(("parallel", "arbitrary")),
    )(x, g.reshape(1, d), w)


def _half_rows(ref, first_row, n_half, row_stride, lane0):
    parts = [ref[pl.ds(first_row + r * row_stride, n_half, stride=CMP_STRIDE * row_stride), pl.ds(lane0, NSA_DK)]
             for r in range(CMP_STRIDE)]
    return jnp.concatenate(parts, axis=1)


def _half_proj_dense_kernel(x_ref, w_ref, o_ref, *, n_half):
    o_ref[...] = _dot(_half_rows(x_ref, 0, n_half, 1, 0).astype(BF16), w_ref[...])


def half_proj_dense(z, w1cat, b_sz, t_len):
    n_half = t_len // CMP_STRIDE
    return pl.pallas_call(
        functools.partial(_half_proj_dense_kernel, n_half=n_half),
        out_shape=jax.ShapeDtypeStruct((2, b_sz, NSA_GROUPS, n_half, 2 * CMP_HIDDEN), F32),
        grid=(2, b_sz, NSA_GROUPS),
        in_specs=[pl.BlockSpec((t_len, NSA_DK), lambda kv, b, g: (b, C_KC // NSA_DK + kv * NSA_GROUPS + g)),
                  pl.BlockSpec((None, CMP_STRIDE * NSA_DK, 2 * CMP_HIDDEN), lambda kv, b, g: (kv, 0, 0))],
        out_specs=pl.BlockSpec((None, None, None, n_half, 2 * CMP_HIDDEN), lambda kv, b, g: (kv, b, g, 0, 0)),
        compiler_params=_params(("parallel", "parallel", "parallel")),
    )(z, w1cat)


HALVES_PER_PAGE = PAGE_SIZE // CMP_STRIDE


def _half_proj_paged_kernel(pt_ref, *refs, pages):
    page_refs, w_ref, o_ref = refs[:pages], refs[pages], refs[pages + 1]
    for g in range(NSA_GROUPS):
        rows = [_half_rows(page_refs[p], g, HALVES_PER_PAGE, NSA_GROUPS, 0) for p in range(pages)]
        xg = jnp.concatenate(rows, axis=0).astype(BF16)
        o_ref[g] = _dot(xg, w_ref[...])


def half_proj_paged(pool2d, page_table, w1cat_one, pages):
    bd, n_pages = page_table.shape
    n_half = n_pages * HALVES_PER_PAGE

    def page_spec(i):
        return pl.BlockSpec((None, PAGE_SIZE * NSA_GROUPS, NSA_DK),
                            lambda b, j, pt: (pt[b, j * pages + i], 0, 0))

    grid_spec = pltpu.PrefetchScalarGridSpec(
        num_scalar_prefetch=1,
        grid=(bd, n_pages // pages),
        in_specs=[page_spec(i) for i in range(pages)]
        + [pl.BlockSpec((CMP_STRIDE * NSA_DK, 2 * CMP_HIDDEN), lambda b, j, pt: (0, 0))],
        out_specs=pl.BlockSpec((None, NSA_GROUPS, pages * HALVES_PER_PAGE, 2 * CMP_HIDDEN),
                               lambda b, j, pt: (b, 0, j, 0)),
    )
    return pl.pallas_call(
        functools.partial(_half_proj_paged_kernel, pages=pages),
        out_shape=jax.ShapeDtypeStruct((bd, NSA_GROUPS, n_half, 2 * CMP_HIDDEN), F32),
        grid_spec=grid_spec,
        compiler_params=_params(("parallel", "arbitrary")),
    )(page_table, *([pool2d] * pages), w1cat_one)


def _compress_finish_kernel(fs_ref, pe_ref, w1_ref, w2_ref, o_ref, *, n_half):
    pe = jnp.broadcast_to(pe_ref[...], (8, CMP_LEN * NSA_DK))
    c = _dot(pe, w1_ref[...])[0:1]
    first = fs_ref[:, 0:CMP_HIDDEN]
    second_next = pltpu.roll(fs_ref[:, CMP_HIDDEN:2 * CMP_HIDDEN], n_half - 1, 0)
    hid = jax.nn.gelu(first + second_next + c)
    out = _dot(hid.astype(BF16), w2_ref[...])
    row = lax.broadcasted_iota(jnp.int32, (n_half, 1), 0)
    o_ref[...] = jnp.where(row < n_half - 1, out, 0.0)


def compress_finish(fs, pe, w1, w2):
    kv, b_sz, _, n_half, _ = fs.shape
    return pl.pallas_call(
        functools.partial(_compress_finish_kernel, n_half=n_half),
        out_shape=jax.ShapeDtypeStruct((kv, b_sz, NSA_GROUPS, n_half, NSA_DK), F32),
        grid=(kv, b_sz, NSA_GROUPS),
        in_specs=[pl.BlockSpec((None, None, None, n_half, 2 * CMP_HIDDEN), lambda k, b, g: (k, b, g, 0, 0)),
                  pl.BlockSpec((None, 1, CMP_LEN * NSA_DK), lambda k, b, g: (k, 0, 0)),
                  pl.BlockSpec((None, CMP_LEN * NSA_DK, CMP_HIDDEN), lambda k, b, g: (k, 0, 0)),
                  pl.BlockSpec((None, CMP_HIDDEN, NSA_DK), lambda k, b, g: (k, 0, 0))],
        out_specs=pl.BlockSpec((None, None, None, n_half, NSA_DK), lambda k, b, g: (k, b, g, 0, 0)),
        compiler_params=_params(("parallel", "parallel", "parallel")),
    )(fs, pe, w1, w2)


def _cmp_probs(s, pos, n_cmp):
    c = lax.broadcasted_iota(jnp.int32, (1, s.shape[1]), 1)
    mask = (c * CMP_STRIDE + (CMP_LEN - 1) <= pos) & (c < n_cmp)
    s = jnp.where(mask, s, NEG)
    m = jnp.max(s, axis=-1, keepdims=True)
    e = jnp.where(mask, jnp.exp(s - m), 0.0)
    l = jnp.sum(e, axis=-1, keepdims=True)
    return e / jnp.where(l > 0.0, l, 1.0)


def _block_scores(pg, n_lanes):
    ncp = pg.shape[1]
    c = lax.broadcasted_iota(jnp.int32, (ncp, n_lanes), 0)
    blk = lax.broadcasted_iota(jnp.int32, (ncp, n_lanes), 1)
    hpb = SEL_BLOCK // CMP_STRIDE
    spread = (c // hpb == blk).astype(F32) + ((c + 1) // hpb == blk).astype(F32)
    return _dot_exact(pg, spread)


def _select_blocks(ps, pos, n_blk):
    blk = lax.broadcasted_iota(jnp.int32, ps.shape, 1)
    cur = pos // SEL_BLOCK
    forced = (blk == 0) | (blk == cur) | (blk == cur - 1)
    score = jnp.where(forced, BIG, jnp.where(blk <= cur, ps, NEG))
    score = jnp.where(blk < n_blk, score, PAD_SCORE)
    blk_f = blk.astype(F32)
    sel = jnp.zeros(ps.shape, F32)
    for _ in range(N_SEL):
        m = jnp.max(score, axis=-1, keepdims=True)
        first = jnp.min(jnp.where(score == m, blk_f, float(ps.shape[1])), axis=-1, keepdims=True)
        hit = blk_f == first
        sel = jnp.where(hit, 1.0, sel)
        score = jnp.where(hit, REMOVED, score)
    return sel


def _online_update(state, s, mask, v_bf):
    m, l, acc = state
    s = jnp.where(mask, s, NEG)
    m_new = jnp.maximum(m, jnp.max(s, axis=-1, keepdims=True))
    alpha = jnp.exp(m - m_new)
    p = jnp.where(mask, jnp.exp(s - m_new), 0.0)
    l = alpha * l + jnp.sum(p, axis=-1, keepdims=True)
    acc = alpha * acc + _dot(p.astype(BF16), v_bf)
    return m_new, l, acc


def _masked_attend(q_bf, k_bf, v_bf, mask):
    s = jnp.where(mask, _dot_nt(q_bf, k_bf), NEG)
    m = jnp.max(s, axis=-1, keepdims=True)
    e = jnp.where(mask, jnp.exp(s - m), 0.0)
    l = jnp.sum(e, axis=-1, keepdims=True)
    return _dot((e / l).astype(BF16), v_bf)


SEL_KEYS = 512
BLK_LANES = 128


def _nsa_prompt_kernel(q_ref, kc_ref, vc_ref, ks_ref, vs_ref, kw_ref, vw_ref, gate_ref, o_ref, *, t_len):
    qb = QUERY_BLOCK
    start = pl.program_id(2) * qb
    pos = start + lax.broadcasted_iota(jnp.int32, (qb, 1), 0)
    n_cmp = t_len // CMP_STRIDE - 1
    n_blk = max(-(-t_len // SEL_BLOCK), N_SEL)
    q = q_ref[...] * NSA_SCALE
    qh = [q[:, h * NSA_DK:(h + 1) * NSA_DK].astype(BF16) for h in range(NSA_HPG)]

    kc = kc_ref[...].astype(BF16)
    vc = vc_ref[...].astype(BF16)
    o_cmp = []
    pg = jnp.zeros((qb, kc.shape[0]), F32)
    for h in range(NSA_HPG):
        p = _cmp_probs(_dot_nt(qh[h], kc), pos, n_cmp)
        o_cmp.append(_dot(p.astype(BF16), vc))
        pg = pg + p
    sel = _select_blocks(_block_scores(pg, BLK_LANES), pos, n_blk).astype(BF16)

    n_steps = (start + qb + SEL_KEYS - 1) // SEL_KEYS

    def sel_step(j, carry):
        k0 = pl.multiple_of(j * SEL_KEYS, SEL_KEYS)
        k_bf = ks_ref[pl.ds(k0, SEL_KEYS), :].astype(BF16)
        v_bf = vs_ref[pl.ds(k0, SEL_KEYS), :].astype(BF16)
        key = k0 + lax.broadcasted_iota(jnp.int32, (1, SEL_KEYS), 1)
        blk_of_key = k0 // SEL_BLOCK + lax.broadcasted_iota(jnp.int32, (BLK_LANES, SEL_KEYS), 1) // SEL_BLOCK
        expand = (lax.broadcasted_iota(jnp.int32, (BLK_LANES, SEL_KEYS), 0) == blk_of_key).astype(BF16)
        mask = (_dot(sel, expand) > 0.5) & (key <= pos)
        return tuple(_online_update(carry[h], _dot_nt(qh[h], k_bf), mask, v_bf) for h in range(NSA_HPG))

    init = tuple((jnp.full((qb, 1), NEG, F32), jnp.zeros((qb, 1), F32), jnp.zeros((qb, NSA_DK), F32))
                 for _ in range(NSA_HPG))
    sel_state = lax.fori_loop(0, n_steps, sel_step, init)

    w_rows = WINDOW + qb
    base = pl.multiple_of(jnp.maximum(start - WINDOW, 0), qb)
    kw = kw_ref[pl.ds(base, w_rows), :].astype(BF16)
    vw = vw_ref[pl.ds(base, w_rows), :].astype(BF16)
    dist = pos - (base + lax.broadcasted_iota(jnp.int32, (1, w_rows), 1))
    w_mask = (dist >= 0) & (dist <= WINDOW)

    gates = jax.nn.sigmoid(gate_ref[...])
    outs = []
    for h in range(NSA_HPG):
        _, l, acc = sel_state[h]
        o_sel = acc / l
        o_win = _masked_attend(qh[h], kw, vw, w_mask)
        outs.append(gates[:, 3 * h:3 * h + 1] * o_cmp[h] + gates[:, 3 * h + 1:3 * h + 2] * o_sel
                    + gates[:, 3 * h + 2:3 * h + 3] * o_win)
    o_ref[...] = jnp.concatenate(outs, axis=1).astype(o_ref.dtype)


def nsa_prompt(z, kcvc, b_sz, t_len):
    assert t_len % SEL_KEYS == 0 and t_len >= WINDOW + QUERY_BLOCK
    nqb = t_len // QUERY_BLOCK
    n_half = t_len // CMP_STRIDE
    hd = NSA_HPG * NSA_DK

    def rows(col0):
        return pl.BlockSpec((t_len, NSA_DK), lambda b, g, i: (b, col0 // NSA_DK + g))

    def cmp_spec(kv):
        return pl.BlockSpec((None, None, None, n_half, NSA_DK), lambda b, g, i: (kv, b, g, 0, 0))

    return pl.pallas_call(
        functools.partial(_nsa_prompt_kernel, t_len=t_len),
        out_shape=jax.ShapeDtypeStruct((b_sz * t_len, NSA_HEADS * NSA_DK), BF16),
        grid=(b_sz, NSA_GROUPS, nqb),
        in_specs=[pl.BlockSpec((QUERY_BLOCK, hd), lambda b, g, i: (b * nqb + i, g)),
                  cmp_spec(0), cmp_spec(1),
                  rows(C_KS), rows(C_VS), rows(C_KW), rows(C_VW),
                  pl.BlockSpec((QUERY_BLOCK, LANES), lambda b, g, i: (b * nqb + i, C_TAIL // LANES + g))],
        out_specs=pl.BlockSpec((QUERY_BLOCK, hd), lambda b, g, i: (b * nqb + i, g)),
        compiler_params=_params(("parallel", "parallel", "arbitrary")),
    )(z, kcvc, kcvc, z, z, z, z, z)


QROWS = 16
S_PAD = 8


def _nsa_sample_kernel(pt_ref, q_ref, kc_ref, vc_ref, kn_ref, vn_ref, kw_ref, vw_ref, gate_ref, *refs,
                       pages, past, s_len, n_cmp, n_blk, blk_lanes, win_rows):
    kpages, vpages = refs[:pages], refs[pages:2 * pages]
    o_ref = refs[2 * pages]
    sel_ref, m_ref, l_ref, acc_ref, oc_ref = refs[2 * pages + 1:]
    j = pl.program_id(1)
    n_j = pl.num_programs(1)
    rows = NSA_HPG * s_len
    step = lax.broadcasted_iota(jnp.int32, (rows, 1), 0) % s_len
    pos = past + step
    qg = [(q_ref[g] * NSA_SCALE).astype(BF16) for g in range(NSA_GROUPS)]

    @pl.when(j == 0)
    def _():
        row_i = lax.broadcasted_iota(jnp.int32, (S_PAD, rows), 1) % s_len
        fold = (row_i == lax.broadcasted_iota(jnp.int32, (S_PAD, rows), 0)).astype(F32)
        row_o = lax.broadcasted_iota(jnp.int32, (rows, S_PAD), 0) % s_len
        unfold = (row_o == lax.broadcasted_iota(jnp.int32, (rows, S_PAD), 1)).astype(F32)
        pos_s = past + lax.broadcasted_iota(jnp.int32, (S_PAD, 1), 0)
        for g in range(NSA_GROUPS):
            kc = kc_ref[g].astype(BF16)
            p = _cmp_probs(_dot_nt(qg[g], kc), pos, n_cmp)
            oc_ref[g] = _dot(p.astype(BF16), vc_ref[g].astype(BF16))
            pg = _dot_exact(fold, p)
            sel = _select_blocks(_block_scores(pg, blk_lanes), pos_s, n_blk)
            sel_ref[g] = _dot_exact(unfold, sel)
            m_ref[g] = jnp.full((rows, 1), NEG, F32)
            l_ref[g] = jnp.zeros((rows, 1), F32)
            acc_ref[g] = jnp.zeros((rows, NSA_DK), F32)

    sel_all = jnp.concatenate([sel_ref[g] for g in range(NSA_GROUPS)], axis=0).astype(BF16)
    for p in range(pages):
        page = j * pages + p
        key = page * PAGE_SIZE + lax.broadcasted_iota(jnp.int32, (1, PAGE_SIZE), 1)
        blk_of_key = page * (PAGE_SIZE // SEL_BLOCK) + lax.broadcasted_iota(
            jnp.int32, (blk_lanes, PAGE_SIZE), 1) // SEL_BLOCK
        expand = (lax.broadcasted_iota(jnp.int32, (blk_lanes, PAGE_SIZE), 0) == blk_of_key).astype(BF16)
        chosen = _dot(sel_all, expand) > 0.5
        for g in range(NSA_GROUPS):
            k_bf = kpages[p][pl.ds(g, PAGE_SIZE, stride=NSA_GROUPS), :].astype(BF16)
            v_bf = vpages[p][pl.ds(g, PAGE_SIZE, stride=NSA_GROUPS), :].astype(BF16)
            mask = chosen[g * rows:(g + 1) * rows] & (key <= pos)
            m, l, acc = _online_update((m_ref[g], l_ref[g], acc_ref[g]), _dot_nt(qg[g], k_bf), mask, v_bf)
            m_ref[g] = m
            l_ref[g] = l
            acc_ref[g] = acc

    @pl.when(j == n_j - 1)
    def _():
        gates = jax.nn.sigmoid(gate_ref[...])
        new_i = lax.broadcasted_iota(jnp.int32, (1, S_PAD), 1)
        new_key = past + new_i
        new_blk = past // SEL_BLOCK
        w_key = past - (win_rows - S_PAD) + lax.broadcasted_iota(jnp.int32, (1, win_rows), 1)
        dist = pos - w_key
        w_mask = (dist >= 0) & (dist <= WINDOW) & (w_key < past + s_len)
        for g in range(NSA_GROUPS):
            chosen_new = sel_ref[g][:, new_blk:new_blk + 1] > 0.5
            mask = chosen_new & (new_key <= pos) & (new_i < s_len)
            _, l, acc = _online_update((m_ref[g], l_ref[g], acc_ref[g]),
                                       _dot_nt(qg[g], kn_ref[g].astype(BF16)), mask, vn_ref[g].astype(BF16))
            o_sel = acc / l
            kw = kw_ref[pl.ds(g, win_rows, stride=NSA_GROUPS), :].astype(BF16)
            vw = vw_ref[pl.ds(g, win_rows, stride=NSA_GROUPS), :].astype(BF16)
            o_win = _masked_attend(qg[g], kw, vw, w_mask)
            gt = gates[g]
            o_ref[g] = gt[:, 0:1] * oc_ref[g] + gt[:, 1:2] * o_sel + gt[:, 2:3] * o_win


def nsa_sample(q, kc, vc, k_new, v_new, kw_all, vw_all, gate, pool_k, pool_v, page_table, s_len, pages):
    bd, n_pages = page_table.shape
    past = n_pages * PAGE_SIZE
    rows = NSA_HPG * s_len
    n_cmp = past // CMP_STRIDE - 1
    n_blk = max(-(-(past + s_len) // SEL_BLOCK), N_SEL)
    assert past % SEL_BLOCK == 0 and s_len <= S_PAD and s_len < CMP_STRIDE and n_pages % pages == 0
    blk_lanes = -(-n_blk // LANES) * LANES
    ncp = kc.shape[2]
    win_rows = kw_all.shape[1] // NSA_GROUPS

    def per_b(shape):
        return pl.BlockSpec((None,) + shape, lambda b, j, pt: (b,) + (0,) * len(shape))

    def page_spec(i):
        return pl.BlockSpec((None, PAGE_SIZE * NSA_GROUPS, NSA_DK), lambda b, j, pt: (pt[b, j * pages + i], 0, 0))

    grid_spec = pltpu.PrefetchScalarGridSpec(
        num_scalar_prefetch=1,
        grid=(bd, n_pages // pages),
        in_specs=[per_b((NSA_GROUPS, rows, NSA_DK)), per_b((NSA_GROUPS, ncp, NSA_DK)), per_b((NSA_GROUPS, ncp, NSA_DK)),
                  per_b((NSA_GROUPS, S_PAD, NSA_DK)), per_b((NSA_GROUPS, S_PAD, NSA_DK)),
                  per_b((win_rows * NSA_GROUPS, NSA_DK)), per_b((win_rows * NSA_GROUPS, NSA_DK)),
                  per_b((NSA_GROUPS, rows, LANES))]
        + [page_spec(i) for i in range(pages)] * 2,
        out_specs=per_b((NSA_GROUPS, rows, NSA_DK)),
        scratch_shapes=[pltpu.VMEM((NSA_GROUPS, rows, blk_lanes), F32),
                        pltpu.VMEM((NSA_GROUPS, rows, 1), F32),
                        pltpu.VMEM((NSA_GROUPS, rows, 1), F32),
                        pltpu.VMEM((NSA_GROUPS, rows, NSA_DK), F32),
                        pltpu.VMEM((NSA_GROUPS, rows, NSA_DK), F32)],
    )
    return pl.pallas_call(
        functools.partial(_nsa_sample_kernel, pages=pages, past=past, s_len=s_len, n_cmp=n_cmp, n_blk=n_blk,
                          blk_lanes=blk_lanes, win_rows=win_rows),
        out_shape=jax.ShapeDtypeStruct((bd, NSA_GROUPS, rows, NSA_DK), F32),
        grid_spec=grid_spec,
        compiler_params=_params(("parallel", "arbitrary")),
    )(page_table, q, kc, vc, k_new, v_new, kw_all, vw_all, gate, *([pool_k] * pages), *([pool_v] * pages))


def _gla_kernel(q_ref, k_ref, v_ref, gr_ref, lr_ref, wlr_ref, blr_ref, ng_ref, s0_ref, o_ref, sT_out_ref,
                st_ref, b_ref, qs_ref, *, chunk, t_valid):
    c = pl.program_id(2)

    @pl.when(c == 0)
    def _():
        st_ref[...] = s0_ref[...].T

    row = lax.broadcasted_iota(jnp.int32, (chunk, 1), 0)
    x = _dot(lr_ref[...].astype(BF16), wlr_ref[...]) + blr_ref[...]
    log_a = jnp.where(row < t_valid, jax.nn.log_sigmoid(x) / GLA_TAU, 0.0)
    tri = (lax.broadcasted_iota(jnp.int32, (chunk, chunk), 1) <= lax.broadcasted_iota(jnp.int32, (chunk, chunk), 0))
    b = _dot_exact(tri.astype(F32), log_a)
    qs = q_ref[...] * (GLA_DK ** -0.5)
    k = k_ref[...]
    b_ref[...] = b
    qs_ref[...] = qs
    lane_t = lax.broadcasted_iota(jnp.int32, (chunk, LANES), 1)

    def intra(t, a_t):
        diff = b_ref[pl.ds(t, 1), :] - b
        w = jnp.where(row <= t, jnp.exp(jnp.where(row <= t, diff, 0.0)), 0.0)
        col = jnp.sum(k * w * qs_ref[pl.ds(t, 1), :], axis=-1, keepdims=True)
        return jnp.where(lane_t == t, col, a_t)

    a_t = lax.fori_loop(0, chunk, intra, jnp.zeros((chunk, LANES), F32))
    v = v_ref[...]
    v_bf = v.astype(BF16)
    st = st_ref[...]
    o = _dot_tn(a_t[:, 0:chunk].astype(BF16), v_bf) + _dot_nt((qs * jnp.exp(b)).astype(BF16), st.astype(BF16))
    b_last = b[chunk - 1:chunk, :]
    k_dec = (k * jnp.exp(b_last - b)).astype(BF16)
    st_new = st * jnp.exp(b_last) + _dot_tn(v_bf, k_dec)
    st_ref[...] = st_new
    o = _rms(o, ng_ref[...])
    gr = gr_ref[...]
    o_ref[...] = (o * (gr * jax.nn.sigmoid(gr))).astype(o_ref.dtype)

    @pl.when(c == pl.num_programs(2) - 1)
    def _():
        sT_out_ref[...] = st_new.T


def gla(z, wlr, blr, norm_g, s0, b_sz, t_len, chunk, t_valid):
    assert chunk <= LANES and chunk % 8 == 0 and t_len % chunk == 0
    nck = t_len // chunk

    def seg(col0, width):
        return pl.BlockSpec((chunk, width), lambda b, h, c: (b * nck + c, col0 // width + h))

    return pl.pallas_call(
        functools.partial(_gla_kernel, chunk=chunk, t_valid=t_valid),
        out_shape=(jax.ShapeDtypeStruct((b_sz * t_len, GLA_HEADS * GLA_DV), BF16),
                   jax.ShapeDtypeStruct((b_sz, GLA_HEADS, GLA_DK, GLA_DV), F32)),
        grid=(b_sz, GLA_HEADS, nck),
        in_specs=[seg(C_GQ, GLA_DK), seg(C_GK, GLA_DK), seg(C_GV, GLA_DV), seg(C_GR, GLA_DV),
                  pl.BlockSpec((chunk, LANES), lambda b, h, c: (b * nck + c, C_TAIL // LANES)),
                  pl.BlockSpec((None, LANES, GLA_DK), lambda b, h, c: (h, 0, 0)),
                  pl.BlockSpec((None, 1, GLA_DK), lambda b, h, c: (h, 0, 0)),
                  pl.BlockSpec((1, GLA_DV), lambda b, h, c: (0, 0)),
                  pl.BlockSpec((None, None, GLA_DK, GLA_DV), lambda b, h, c: (b, h, 0, 0))],
        out_specs=(pl.BlockSpec((chunk, GLA_DV), lambda b, h, c: (b * nck + c, h)),
                   pl.BlockSpec((None, None, GLA_DK, GLA_DV), lambda b, h, c: (b, h, 0, 0))),
        scratch_shapes=[pltpu.VMEM((GLA_DV, GLA_DK), F32), pltpu.VMEM((chunk, GLA_DK), F32),
                        pltpu.VMEM((chunk, GLA_DK), F32)],
        compiler_params=_params(("parallel", "parallel", "arbitrary")),
    )(z, z, z, z, z, wlr, blr, norm_g.reshape(1, GLA_DV), s0)


def _merge_kernel(on_ref, og_ref, wn_ref, wg_ref, ma_ref, mb_ref, y_ref):
    y = (jax.nn.sigmoid(ma_ref[...]) * _dot(on_ref[...], wn_ref[...])
         + jax.nn.sigmoid(mb_ref[...]) * _dot(og_ref[...], wg_ref[...]))
    y_ref[...] = y.astype(y_ref.dtype)


def merge(o_nsa, o_gla, w_nsa, w_gla, z, tm, tn):
    n = o_nsa.shape[0]
    return pl.pallas_call(
        _merge_kernel,
        out_shape=jax.ShapeDtypeStruct((n, D_MODEL), BF16),
        grid=(n // tm, D_MODEL // tn),
        in_specs=[pl.BlockSpec((tm, o_nsa.shape[1]), lambda i, j: (i, 0)),
                  pl.BlockSpec((tm, o_gla.shape[1]), lambda i, j: (i, 0)),
                  pl.BlockSpec((w_nsa.shape[0], tn), lambda i, j: (0, j)),
                  pl.BlockSpec((w_gla.shape[0], tn), lambda i, j: (0, j)),
                  pl.BlockSpec((tm, tn), lambda i, j: (i, C_MA // tn + j)),
                  pl.BlockSpec((tm, tn), lambda i, j: (i, C_MB // tn + j))],
        out_specs=pl.BlockSpec((tm, tn), lambda i, j: (i, j)),
        compiler_params=_params(("parallel", "arbitrary")),
    )(o_nsa, o_gla, w_nsa, w_gla, z, z)


def _out_proj_kernel(y_ref, w_ref, x_ref, o_ref):
    o_ref[...] = x_ref[...] + _dot(y_ref[...], w_ref[...])


def out_proj(y, w_out, x, tm, tn):
    n = y.shape[0]
    return pl.pallas_call(
        _out_proj_kernel,
        out_shape=jax.ShapeDtypeStruct((n, D_MODEL), F32),
        grid=(n // tm, D_MODEL // tn),
        in_specs=[pl.BlockSpec((tm, D_MODEL), lambda i, j: (i, 0)),
                  pl.BlockSpec((D_MODEL, tn), lambda i, j: (0, j)),
                  pl.BlockSpec((tm, tn), lambda i, j: (i, j))],
        out_specs=pl.BlockSpec((tm, tn), lambda i, j: (i, j)),
        compiler_params=_params(("parallel", "arbitrary")),
    )(y, w_out, x)


def _top_values(s, count):
    idx = lax.broadcasted_iota(jnp.int32, s.shape, 0).astype(F32)
    vals = []
    for _ in range(count):
        m = jnp.max(s, axis=0, keepdims=True)
        first = jnp.min(jnp.where(s == m, idx, float(s.shape[0])), axis=0, keepdims=True)
        s = jnp.where(idx == first, REMOVED, s)
        vals.append(m)
    return vals


def _peer_scores_kernel(q_ref, k1_ref, k2_ref, s1_ref, s2_ref, e1_ref, e2_ref, tau_ref):
    taus = []
    for h in range(PEER_HEADS):
        q1 = q_ref[:, h * 2 * PEER_HALF:h * 2 * PEER_HALF + PEER_HALF].astype(BF16)
        q2 = q_ref[:, h * 2 * PEER_HALF + PEER_HALF:(h + 1) * 2 * PEER_HALF].astype(BF16)
        s1 = _dot_nt(k1_ref[h], q1)
        s2 = _dot_nt(k2_ref[h], q2)
        v1 = _top_values(s1, PEER_TOPK)
        v2 = _top_values(s2, PEER_TOPK)
        v2_all = jnp.concatenate(v2, axis=0)
        cand = jnp.concatenate([v1[i] + v2_all for i in range(PEER_TOPK)], axis=0)
        top = _top_values(cand, PEER_TOPK)
        z = sum(jnp.exp(t - top[0]) for t in top)
        s1_ref[h] = s1
        s2_ref[h] = s2
        e1_ref[h] = jnp.exp(s1 - v1[0])
        e2_ref[h] = jnp.exp(s2 - v2[0]) / z
        taus.append(top[PEER_TOPK - 1])
    tau_ref[...] = jnp.concatenate(taus, axis=0)


def peer_scores(qp, keys1, keys2, tn):
    n = qp.shape[0]
    big = jax.ShapeDtypeStruct((PEER_HEADS, PEER_NKEYS, n), F32)
    big_spec = pl.BlockSpec((PEER_HEADS, PEER_NKEYS, tn), lambda i: (0, 0, i))
    key_spec = pl.BlockSpec((PEER_HEADS, PEER_NKEYS, PEER_HALF), lambda i: (0, 0, 0))
    return pl.pallas_call(
        _peer_scores_kernel,
        out_shape=(big, big, big, big, jax.ShapeDtypeStruct((PEER_HEADS, n), F32)),
        grid=(n // tn,),
        in_specs=[pl.BlockSpec((tn, qp.shape[1]), lambda i: (i, 0)), key_spec, key_spec],
        out_specs=(big_spec, big_spec, big_spec, big_spec, pl.BlockSpec((PEER_HEADS, tn), lambda i: (0, i))),
        compiler_params=_params(("parallel",)),
    )(qp, keys1, keys2)


PEER_I1_PER_TILE = 8


def _peer_dense_kernel(x_ref, g2_ref, gf_ref, u_ref, v_ref, s1_ref, e1_ref, s2_ref, e2_ref, tau_ref, o_ref,
                       h_ref, acc_ref):
    e = pl.program_id(1)

    @pl.when(e == 0)
    def _():
        h_ref[...] = _rms(x_ref[...], g2_ref[...]).astype(BF16)
        acc_ref[...] = jnp.zeros_like(acc_ref)

    act = jax.nn.gelu(_dot_nt(u_ref[...], h_ref[...]))
    tau = tau_ref[...]
    parts = []
    for c in range(PEER_I1_PER_TILE):
        w = jnp.zeros((PEER_NKEYS, act.shape[1]), F32)
        for h in range(PEER_HEADS):
            keep = s1_ref[h, c:c + 1, :] + s2_ref[h] >= tau[h:h + 1, :]
            w = w + jnp.where(keep, e1_ref[h, c:c + 1, :] * e2_ref[h], 0.0)
        parts.append((w * act[c * PEER_NKEYS:(c + 1) * PEER_NKEYS]).astype(BF16))
    acc_ref[...] += _dot_tn(jnp.concatenate(parts, axis=0), v_ref[...])

    @pl.when(e == pl.num_programs(1) - 1)
    def _():
        o_ref[...] = _rms(x_ref[...] + acc_ref[...], gf_ref[...])


def peer_dense(x, g2, gf, u_bf, v_bf, s1, s2, e1, e2, tau, tn):
    n = x.shape[0]
    te = PEER_I1_PER_TILE * PEER_NKEYS
    n_exp = u_bf.shape[0]
    sub = pl.BlockSpec((PEER_HEADS, PEER_I1_PER_TILE, tn), lambda i, e: (0, e, i))
    full = pl.BlockSpec((PEER_HEADS, PEER_NKEYS, tn), lambda i, e: (0, 0, i))
    vec = pl.BlockSpec((1, D_MODEL), lambda i, e: (0, 0))
    return pl.pallas_call(
        _peer_dense_kernel,
        out_shape=jax.ShapeDtypeStruct((n, D_MODEL), F32),
        grid=(n // tn, n_exp // te),
        in_specs=[pl.BlockSpec((tn, D_MODEL), lambda i, e: (i, 0)), vec, vec,
                  pl.BlockSpec((te, D_MODEL), lambda i, e: (e, 0)),
                  pl.BlockSpec((te, D_MODEL), lambda i, e: (e, 0)),
                  sub, sub, full, full,
                  pl.BlockSpec((PEER_HEADS, tn), lambda i, e: (0, i))],
        out_specs=pl.BlockSpec((tn, D_MODEL), lambda i, e: (i, 0)),
        scratch_shapes=[pltpu.VMEM((tn, D_MODEL), BF16), pltpu.VMEM((tn, D_MODEL), F32)],
        compiler_params=_params(("parallel", "arbitrary")),
    )(x, g2.reshape(1, D_MODEL), gf.reshape(1, D_MODEL), u_bf, v_bf, s1, e1, s2, e2, tau)


def _pack_w_in(w_in):
    offs = np.concatenate([[0], np.cumsum(IN_SPLITS)])
    seg = lambda i: w_in[:, offs[i]:offs[i + 1]]
    g_nsa, glr = seg(7), seg(12)
    per_group = 3 * NSA_HPG
    zeros = lambda n: jnp.zeros((w_in.shape[0], n), w_in.dtype)
    tail = []
    for g in range(NSA_GROUPS):
        tail.append(g_nsa[:, g * per_group:(g + 1) * per_group])
        if g == 0:
            tail += [zeros(GLR_LANE - per_group), glr, zeros(LANES - GLR_LANE - GLA_RANK)]
        else:
            tail.append(zeros(LANES - per_group))
    packed = jnp.concatenate([seg(i) for i in (0, 1, 2, 3, 4, 5, 6, 8, 9, 10, 11, 13, 14)] + tail, axis=1)
    assert packed.shape[1] == Z_COLS
    return packed.astype(BF16)


def _pack_cmp(pe, w1, w2):
    w1cat = jnp.concatenate([w1[:CMP_STRIDE].reshape(CMP_STRIDE * NSA_DK, CMP_HIDDEN),
                             w1[CMP_STRIDE:].reshape(CMP_STRIDE * NSA_DK, CMP_HIDDEN)], axis=1).astype(BF16)
    return (w1cat, pe.reshape(1, CMP_LEN * NSA_DK).astype(BF16),
            w1.reshape(CMP_LEN * NSA_DK, CMP_HIDDEN).astype(BF16), w2.astype(BF16))


def _row_tile(n, cap):
    t = min(n, cap)
    while n % t:
        t //= 2
    return t


def _channel_tail(x2d, z, o_nsa, o_gla, wts):
    n = x2d.shape[0]
    tm = _row_tile(n, 512)
    y = merge(o_nsa, o_gla, wts["w_nsa"], wts["w_gla"], z, tm, 512)
    x1 = out_proj(y, wts["w_out"], x2d, tm, 512)
    qp = norm_matmul(x1, wts["norm2_g"], wts["w_q"], tm, 512)
    tn = _row_tile(n, 256)
    s1, s2, e1, e2, tau = peer_scores(qp, wts["keys1"], wts["keys2"], tn)
    return peer_dense(x1, wts["norm2_g"], wts["norm_f_g"], wts["u"], wts["v"], s1, s2, e1, e2, tau, _row_tile(n, 512))


def kernel(x_prompt, x_sample, cache_k_cmp, cache_v_cmp, cache_k_slc, cache_v_slc, cache_k_win, cache_v_win, state_gla, page_table, norm1_g, w_in, cmp_pe_k, cmp_w1_k, cmp_w2_k, cmp_pe_v, cmp_w1_v, cmp_w2_v, gla_w_lr2, gla_b_lr, gla_norm_g, w_nsa_proj, w_gla_proj, w_out, norm2_g, peer_w_q, peer_keys1, peer_keys2, peer_u, peer_v, norm_f_g):
    b_sz, t_len, _ = x_prompt.shape
    bd, s_len, _ = x_sample.shape
    n_pool = cache_k_cmp.shape[0]
    wb = cache_k_win.shape[1]

    w_pack = _pack_w_in(w_in)
    cmp_k = _pack_cmp(cmp_pe_k, cmp_w1_k, cmp_w2_k)
    cmp_v = _pack_cmp(cmp_pe_v, cmp_w1_v, cmp_w2_v)
    cmp_kv = [jnp.stack([a, b]) for a, b in zip(cmp_k, cmp_v)]
    wlr = jnp.zeros((LANES, GLA_HEADS * GLA_DK), F32).at[GLR_LANE:GLR_LANE + GLA_RANK].set(gla_w_lr2)
    wlr = wlr.reshape(LANES, GLA_HEADS, GLA_DK).transpose(1, 0, 2).astype(BF16)
    blr = gla_b_lr.reshape(GLA_HEADS, 1, GLA_DK)
    wts = dict(w_nsa=w_nsa_proj.astype(BF16), w_gla=w_gla_proj.astype(BF16), w_out=w_out.astype(BF16),
               w_q=peer_w_q.astype(BF16), norm2_g=norm2_g, norm_f_g=norm_f_g,
               keys1=peer_keys1.astype(BF16), keys2=peer_keys2.astype(BF16),
               u=peer_u.astype(BF16), v=peer_v.astype(BF16))

    def heads_out(z, col0, lead):
        return z[:, col0:col0 + GD].reshape(lead + (NSA_GROUPS, NSA_DK))

    n_p = b_sz * t_len
    xp = x_prompt.reshape(n_p, D_MODEL)
    z_p = norm_matmul(xp, norm1_g, w_pack, _row_tile(n_p, 1024), 512)
    fs_p = half_proj_dense(z_p, cmp_kv[0], b_sz, t_len)
    kcvc_p = compress_finish(fs_p, cmp_kv[1], cmp_kv[2], cmp_kv[3])
    o_nsa_p = nsa_prompt(z_p, kcvc_p, b_sz, t_len)
    s0 = jnp.zeros((b_sz, GLA_HEADS, GLA_DK, GLA_DV), F32)
    o_gla_p, gla_state_p = gla(z_p, wlr, blr, gla_norm_g, s0, b_sz, t_len, GLA_CHUNK, GLA_CHUNK)
    y_prompt = _channel_tail(xp, z_p, o_nsa_p, o_gla_p, wts).reshape(b_sz, t_len, D_MODEL)
    lead_p = (b_sz, t_len)
    kcr_p, vcr_p, ksr_p, vsr_p, kwr_p, vwr_p = (heads_out(z_p, c, lead_p) for c in (C_KC, C_VC, C_KS, C_VS, C_KW, C_VW))
    wl = min(WINDOW, t_len)
    k_win_p = kwr_p[:, t_len - wl:]
    v_win_p = vwr_p[:, t_len - wl:]

    n_s = bd * s_len
    xs = x_sample.reshape(n_s, D_MODEL)
    z_s = norm_matmul(xs, norm1_g, w_pack, _row_tile(n_s, 1024), 512)
    lead_s = (bd, s_len)
    kcr_s, vcr_s, ksr_s, vsr_s, kwr_s, vwr_s = (heads_out(z_s, c, lead_s) for c in (C_KC, C_VC, C_KS, C_VS, C_KW, C_VW))

    pool2d = lambda p: p.reshape(n_pool, PAGE_SIZE * NSA_GROUPS, NSA_DK)
    pages = 16
    kc_s = compress_finish(half_proj_paged(pool2d(cache_k_cmp), page_table, cmp_k[0], pages)[None],
                           cmp_k[1][None], cmp_k[2][None], cmp_k[3][None])[0]
    vc_s = compress_finish(half_proj_paged(pool2d(cache_v_cmp), page_table, cmp_v[0], pages)[None],
                           cmp_v[1][None], cmp_v[2][None], cmp_v[3][None])[0]

    kw_all = jnp.concatenate([cache_k_win, kwr_s], axis=1)
    vw_all = jnp.concatenate([cache_v_win, vwr_s], axis=1)
    k_win_s = kw_all[:, s_len:s_len + wb]
    v_win_s = vw_all[:, s_len:s_len + wb]
    win_rows = wb + S_PAD

    def win_rows_2d(a):
        a = jnp.pad(a, ((0, 0), (0, win_rows - a.shape[1]), (0, 0), (0, 0)))
        return a.reshape(bd, win_rows * NSA_GROUPS, NSA_DK)

    def new_rows(a):
        return jnp.pad(a.transpose(0, 2, 1, 3), ((0, 0), (0, 0), (0, S_PAD - s_len), (0, 0)))

    rows = NSA_HPG * s_len
    q_s = z_s[:, C_Q:C_Q + NSA_HEADS * NSA_DK].reshape(bd, s_len, NSA_GROUPS, NSA_HPG, NSA_DK)
    q_s = q_s.transpose(0, 2, 3, 1, 4).reshape(bd, NSA_GROUPS, rows, NSA_DK)
    gate_s = z_s[:, C_TAIL:C_TAIL + 512].reshape(bd, s_len, NSA_GROUPS, LANES)[..., :3 * NSA_HPG]
    gate_s = gate_s.reshape(bd, s_len, NSA_GROUPS, NSA_HPG, 3).transpose(0, 2, 3, 1, 4).reshape(bd, NSA_GROUPS, rows, 3)
    gate_s = jnp.pad(gate_s, ((0, 0), (0, 0), (0, 0), (0, LANES - 3)))
    o_s = nsa_sample(q_s, kc_s, vc_s, new_rows(ksr_s), new_rows(vsr_s), win_rows_2d(kw_all), win_rows_2d(vw_all),
                     gate_s, pool2d(cache_k_slc), pool2d(cache_v_slc), page_table, s_len, 8)
    o_nsa_s = o_s.reshape(bd, NSA_GROUPS, NSA_HPG, s_len, NSA_DK).transpose(0, 3, 1, 2, 4)
    o_nsa_s = o_nsa_s.reshape(n_s, NSA_HEADS * NSA_DK).astype(BF16)

    z_s_pad = jnp.pad(z_s.reshape(bd, s_len, Z_COLS), ((0, 0), (0, S_PAD - s_len), (0, 0))).reshape(bd * S_PAD, Z_COLS)
    o_gla_s, gla_state_s = gla(z_s_pad, wlr, blr, gla_norm_g, state_gla, bd, S_PAD, S_PAD, s_len)
    o_gla_s = o_gla_s.reshape(bd, S_PAD, GLA_HEADS * GLA_DV)[:, :s_len].reshape(n_s, GLA_HEADS * GLA_DV)
    y_sample = _channel_tail(xs, z_s, o_nsa_s, o_gla_s, wts).reshape(bd, s_len, D_MODEL)

    return (y_prompt, y_sample, kcr_p, vcr_p, ksr_p, vsr_p, k_win_p, v_win_p, gla_state_p,
            kcr_s, vcr_s, ksr_s, vsr_s, k_win_s, v_win_s, gla_state_s)
```

```python
import functools

import numpy as np
import jax
import jax.numpy as jnp
from jax import lax
from jax.experimental import pallas as pl
from jax.experimental.pallas import tpu as pltpu

F32 = jnp.float32
BF16 = jnp.bfloat16

D_MODEL = 2048
PAGE_SIZE = 128
NSA_HEADS = 16
NSA_GROUPS = 4
NSA_HPG = NSA_HEADS // NSA_GROUPS
NSA_DK = 128
NSA_SCALE = NSA_DK ** -0.5
CMP_LEN = 32
CMP_STRIDE = 16
CMP_HIDDEN = 256
SEL_BLOCK = 64
N_SEL = 16
WINDOW = 512
QUERY_BLOCK = 128
GLA_HEADS = 4
GLA_DK = 256
GLA_DV = 512
GLA_RANK = 16
GLA_TAU = 16.0
GLA_CHUNK = 64
PEER_HEADS = 8
PEER_NKEYS = 128
PEER_HALF = 128
PEER_TOPK = 16
NORM_EPS = 1e-6
NEG = -1e30
BIG = 1e30
PAD_SCORE = -3e38
REMOVED = -float("inf")

IN_SPLITS = (NSA_HEADS * NSA_DK,) + (NSA_GROUPS * NSA_DK,) * 6 + (
    3 * NSA_HEADS, GLA_HEADS * GLA_DK, GLA_HEADS * GLA_DK, GLA_HEADS * GLA_DV,
    GLA_HEADS * GLA_DV, GLA_RANK, D_MODEL, D_MODEL)

GD = NSA_GROUPS * NSA_DK
C_Q = 0
C_KC = C_Q + NSA_HEADS * NSA_DK
C_VC = C_KC + GD
C_KS = C_VC + GD
C_VS = C_KS + GD
C_KW = C_VS + GD
C_VW = C_KW + GD
C_GQ = C_VW + GD
C_GK = C_GQ + GLA_HEADS * GLA_DK
C_GV = C_GK + GLA_HEADS * GLA_DK
C_GR = C_GV + GLA_HEADS * GLA_DV
C_MA = C_GR + GLA_HEADS * GLA_DV
C_MB = C_MA + D_MODEL
C_TAIL = C_MB + D_MODEL
Z_COLS = C_TAIL + 512
GLR_LANE = 16
LANES = 128

VMEM_LIMIT = 56 * 1024 * 1024


def _params(sem):
    return pltpu.CompilerParams(dimension_semantics=sem, vmem_limit_bytes=VMEM_LIMIT)


def _dot(a, b):
    return jnp.dot(a, b, preferred_element_type=F32)


def _dot_nt(a, b):
    return lax.dot_general(a, b, (((1,), (1,)), ((), ())), preferred_element_type=F32)


def _dot_tn(a, b):
    return lax.dot_general(a, b, (((0,), (0,)), ((), ())), preferred_element_type=F32)


def _dot_exact(a, b):
    return jnp.dot(a, b, preferred_element_type=F32, precision=lax.Precision.HIGHEST)


def _rms(x, g):
    return x * lax.rsqrt(jnp.mean(x * x, axis=-1, keepdims=True) + NORM_EPS) * g


def _norm_matmul_kernel(x_ref, g_ref, w_ref, o_ref, h_ref):
    @pl.when(pl.program_id(1) == 0)
    def _():
        h_ref[...] = _rms(x_ref[...], g_ref[...]).astype(BF16)

    o_ref[...] = _dot(h_ref[...], w_ref[...])


def norm_matmul(x, g, w, tm, tn):
    n, d = x.shape
    cols = w.shape[1]
    return pl.pallas_call(
        _norm_matmul_kernel,
        out_shape=jax.ShapeDtypeStruct((n, cols), F32),
        grid=(n // tm, cols // tn),
        in_specs=[pl.BlockSpec((tm, d), lambda i, j: (i, 0)),
                  pl.BlockSpec((1, d), lambda i, j: (0, 0)),
                  pl.BlockSpec((d, tn), lambda i, j: (0, j))],
        out_specs=pl.BlockSpec((tm, tn), lambda i, j: (i, j)),
        scratch_shapes=[pltpu.VMEM((tm, d), BF16)],
        compiler_params=_params(("parallel", "arbitrary")),
        name="norm_matmul",
    )(x, g.reshape(1, d), w)


def _half_rows(ref, first_row, n_half, row_stride, lane0):
    parts = [ref[pl.ds(first_row + r * row_stride, n_half, stride=CMP_STRIDE * row_stride), pl.ds(lane0, NSA_DK)]
             for r in range(CMP_STRIDE)]
    return jnp.concatenate(parts, axis=1)


def _half_proj_dense_kernel(x_ref, w_ref, o_ref, *, n_half):
    o_ref[...] = _dot(_half_rows(x_ref, 0, n_half, 1, 0).astype(BF16), w_ref[...])


def half_proj_dense(z, w1cat, b_sz, t_len):
    n_half = t_len // CMP_STRIDE
    return pl.pallas_call(
        functools.partial(_half_proj_dense_kernel, n_half=n_half),
        out_shape=jax.ShapeDtypeStruct((2, b_sz, NSA_GROUPS, n_half, 2 * CMP_HIDDEN), F32),
        grid=(2, b_sz, NSA_GROUPS),
        in_specs=[pl.BlockSpec((t_len, NSA_DK), lambda kv, b, g: (b, C_KC // NSA_DK + kv * NSA_GROUPS + g)),
                  pl.BlockSpec((None, CMP_STRIDE * NSA_DK, 2 * CMP_HIDDEN), lambda kv, b, g: (kv, 0, 0))],
        out_specs=pl.BlockSpec((None, None, None, n_half, 2 * CMP_HIDDEN), lambda kv, b, g: (kv, b, g, 0, 0)),
        compiler_params=_params(("parallel", "parallel", "parallel")),
        name="half_proj_dense",
    )(z, w1cat)


HALVES_PER_PAGE = PAGE_SIZE // CMP_STRIDE


def _half_proj_paged_kernel(pt_ref, *refs, pages):
    page_refs, w_ref, o_ref = refs[:pages], refs[pages], refs[pages + 1]
    tile = 2 * NSA_GROUPS
    rows_per_half = CMP_STRIDE * NSA_GROUPS
    low = lax.broadcasted_iota(jnp.int32, (tile, NSA_DK), 0) < NSA_GROUPS
    acc = None
    for r2 in range(CMP_STRIDE // 2):
        even, odd = [], []
        for p in range(pages):
            for n in range(0, HALVES_PER_PAGE, 2):
                a = page_refs[p][pl.ds(n * rows_per_half + tile * r2, tile), :]
                b = page_refs[p][pl.ds((n + 1) * rows_per_half + tile * r2, tile), :]
                even.append(jnp.where(low, a, pltpu.roll(b, NSA_GROUPS, 0)))
                odd.append(jnp.where(low, pltpu.roll(a, NSA_GROUPS, 0), b))
        lhs = jnp.concatenate([jnp.concatenate(even, axis=0), jnp.concatenate(odd, axis=0)], axis=1).astype(BF16)
        part = _dot(lhs, w_ref[pl.ds(r2 * 2 * NSA_DK, 2 * NSA_DK), :])
        acc = part if acc is None else acc + part
    o_ref[...] = acc


def half_proj_paged(pool2d, page_table, w1cat_one, pages):
    bd, n_pages = page_table.shape
    n_half = n_pages * HALVES_PER_PAGE
    rows = pages * HALVES_PER_PAGE * NSA_GROUPS

    def page_spec(i):
        return pl.BlockSpec((None, PAGE_SIZE * NSA_GROUPS, NSA_DK),
                            lambda b, j, pt: (pt[b, j * pages + i], 0, 0))

    grid_spec = pltpu.PrefetchScalarGridSpec(
        num_scalar_prefetch=1,
        grid=(bd, n_pages // pages),
        in_specs=[page_spec(i) for i in range(pages)]
        + [pl.BlockSpec((CMP_STRIDE * NSA_DK, 2 * CMP_HIDDEN), lambda b, j, pt: (0, 0))],
        out_specs=pl.BlockSpec((None, rows, 2 * CMP_HIDDEN), lambda b, j, pt: (b, j, 0)),
    )
    return pl.pallas_call(
        functools.partial(_half_proj_paged_kernel, pages=pages),
        out_shape=jax.ShapeDtypeStruct((bd, n_half * NSA_GROUPS, 2 * CMP_HIDDEN), F32),
        grid_spec=grid_spec,
        compiler_params=_params(("parallel", "arbitrary")),
        name="half_proj_paged",
    )(page_table, *([pool2d] * pages), w1cat_one)


def _compress_finish_kernel(fs_ref, pe_ref, w1_ref, w2_ref, o_ref, *, n_rows, step):
    pe = jnp.broadcast_to(pe_ref[...], (8, CMP_LEN * NSA_DK))
    c = _dot(pe, w1_ref[...])[0:1]
    first = fs_ref[:, 0:CMP_HIDDEN]
    second_next = pltpu.roll(fs_ref[:, CMP_HIDDEN:2 * CMP_HIDDEN], n_rows - step, 0)
    hid = jax.nn.gelu(first + second_next + c)
    out = _dot(hid.astype(BF16), w2_ref[...])
    row = lax.broadcasted_iota(jnp.int32, (n_rows, 1), 0)
    o_ref[...] = jnp.where(row < n_rows - step, out, 0.0)


def compress_finish(fs, pe, w1, w2, step):
    kv, b_sz, n_sets, n_rows, _ = fs.shape
    return pl.pallas_call(
        functools.partial(_compress_finish_kernel, n_rows=n_rows, step=step),
        out_shape=jax.ShapeDtypeStruct((kv, b_sz, n_sets, n_rows, NSA_DK), F32),
        grid=(kv, b_sz, n_sets),
        in_specs=[pl.BlockSpec((None, None, None, n_rows, 2 * CMP_HIDDEN), lambda k, b, g: (k, b, g, 0, 0)),
                  pl.BlockSpec((None, 1, CMP_LEN * NSA_DK), lambda k, b, g: (k, 0, 0)),
                  pl.BlockSpec((None, CMP_LEN * NSA_DK, CMP_HIDDEN), lambda k, b, g: (k, 0, 0)),
                  pl.BlockSpec((None, CMP_HIDDEN, NSA_DK), lambda k, b, g: (k, 0, 0))],
        out_specs=pl.BlockSpec((None, None, None, n_rows, NSA_DK), lambda k, b, g: (k, b, g, 0, 0)),
        compiler_params=_params(("parallel", "parallel", "parallel")),
        name="compress_finish",
    )(fs, pe, w1, w2)


def _cmp_probs(s, pos, n_cmp):
    c = lax.broadcasted_iota(jnp.int32, (1, s.shape[1]), 1)
    mask = (c * CMP_STRIDE + (CMP_LEN - 1) <= pos) & (c < n_cmp)
    s = jnp.where(mask, s, NEG)
    m = jnp.max(s, axis=-1, keepdims=True)
    e = jnp.where(mask, jnp.exp(s - m), 0.0)
    l = jnp.sum(e, axis=-1, keepdims=True)
    return e / jnp.where(l > 0.0, l, 1.0)


def _block_scores_t(pg, n_rows):
    ncp = pg.shape[1]
    blk = lax.broadcasted_iota(jnp.int32, (n_rows, ncp), 0)
    c = lax.broadcasted_iota(jnp.int32, (n_rows, ncp), 1)
    hpb = SEL_BLOCK // CMP_STRIDE
    spread = (c // hpb == blk).astype(F32) + ((c + 1) // hpb == blk).astype(F32)
    return lax.dot_general(spread, pg, (((1,), (1,)), ((), ())), preferred_element_type=F32,
                           precision=lax.Precision.HIGHEST)


def _select_blocks_t(ps_t, pos_row, n_blk):
    blk = lax.broadcasted_iota(jnp.int32, ps_t.shape, 0)
    cur = pos_row // SEL_BLOCK
    forced = (blk == 0) | (blk == cur) | (blk == cur - 1)
    score = jnp.where(forced, BIG, jnp.where(blk <= cur, ps_t, NEG))
    score = jnp.where(blk < n_blk, score, PAD_SCORE)
    blk_f = blk.astype(F32)
    sel = jnp.zeros(ps_t.shape, F32)
    for _ in range(N_SEL):
        m = jnp.max(score, axis=0, keepdims=True)
        first = jnp.min(jnp.where(score == m, blk_f, float(ps_t.shape[0])), axis=0, keepdims=True)
        hit = blk_f == first
        sel = jnp.where(hit, 1.0, sel)
        score = jnp.where(hit, REMOVED, score)
    return sel


def _lane_groups(x, op):
    parts = [x[:, i * LANES:(i + 1) * LANES] for i in range(x.shape[1] // LANES)]
    while len(parts) > 1:
        parts = [op(parts[i], parts[i + 1]) for i in range(0, len(parts) - 1, 2)] + (
            [parts[-1]] if len(parts) % 2 else [])
    return parts[0]


def _block_expand(n_blocks, n_keys):
    blk = lax.broadcasted_iota(jnp.int32, (n_blocks, n_keys), 0)
    key = lax.broadcasted_iota(jnp.int32, (n_blocks, n_keys), 1)
    return (key // SEL_BLOCK == blk).astype(BF16)


def _online_update(state, s, mask, v_bf):
    m, l, acc = state
    s = jnp.where(mask, s, NEG)
    m_new = jnp.maximum(m, jnp.max(s, axis=-1, keepdims=True))
    alpha = jnp.exp(m - m_new)
    p = jnp.where(mask, jnp.exp(s - m_new), 0.0)
    l = alpha * l + jnp.sum(p, axis=-1, keepdims=True)
    acc = alpha * acc + _dot(p.astype(BF16), v_bf)
    return m_new, l, acc


def _masked_attend(q_bf, k_bf, v_bf, mask):
    s = jnp.where(mask, _dot_nt(q_bf, k_bf), NEG)
    m = jnp.max(s, axis=-1, keepdims=True)
    e = jnp.where(mask, jnp.exp(s - m), 0.0)
    l = jnp.sum(e, axis=-1, keepdims=True)
    return _dot((e / l).astype(BF16), v_bf)


SEL_KEYS = 512
BLK_LANES = 128


def _nsa_prompt_kernel(q_ref, kc_ref, vc_ref, ks_ref, vs_ref, kw_ref, vw_ref, gate_ref, o_ref, sel_ref, s_ref,
                       *, t_len):
    qb = QUERY_BLOCK
    start = pl.program_id(2) * qb
    pos = start + lax.broadcasted_iota(jnp.int32, (qb, 1), 0)
    pos_row = start + lax.broadcasted_iota(jnp.int32, (1, qb), 1)
    n_cmp = t_len // CMP_STRIDE - 1
    n_blk = max(-(-t_len // SEL_BLOCK), N_SEL)
    q = q_ref[...] * NSA_SCALE
    qh = [q[:, h * NSA_DK:(h + 1) * NSA_DK].astype(BF16) for h in range(NSA_HPG)]

    kc = kc_ref[...].astype(BF16)
    vc = vc_ref[...].astype(BF16)
    o_cmp = []
    pg = jnp.zeros((qb, kc.shape[0]), F32)
    for h in range(NSA_HPG):
        p = _cmp_probs(_dot_nt(qh[h], kc), pos, n_cmp)
        o_cmp.append(_dot(p.astype(BF16), vc))
        pg = pg + p
    sel_ref[...] = _select_blocks_t(_block_scores_t(pg, BLK_LANES), pos_row, n_blk)

    w_rows = WINDOW + qb
    base = pl.multiple_of(jnp.maximum(start - WINDOW, 0), qb)
    kw = kw_ref[pl.ds(base, w_rows), :].astype(BF16)
    vw = vw_ref[pl.ds(base, w_rows), :].astype(BF16)
    dist = pos - (base + lax.broadcasted_iota(jnp.int32, (1, w_rows), 1))
    w_mask = (dist >= 0) & (dist <= WINDOW)
    o_win = [_masked_attend(qh[h], kw, vw, w_mask) for h in range(NSA_HPG)]

    n_steps = (start + qb + SEL_KEYS - 1) // SEL_KEYS
    blocks_per_step = SEL_KEYS // SEL_BLOCK
    expand = _block_expand(blocks_per_step, SEL_KEYS)

    def scores_step(j, m_run):
        k0 = pl.multiple_of(j * SEL_KEYS, SEL_KEYS)
        k_bf = ks_ref[pl.ds(k0, SEL_KEYS), :].astype(BF16)
        key = k0 + lax.broadcasted_iota(jnp.int32, (1, SEL_KEYS), 1)
        chosen = sel_ref[pl.ds(pl.multiple_of(j * blocks_per_step, blocks_per_step), blocks_per_step), :]
        mask = (_dot_tn(chosen.astype(BF16), expand) > 0.5) & (key <= pos)
        out = []
        for h in range(NSA_HPG):
            s = jnp.where(mask, _dot_nt(qh[h], k_bf), NEG)
            s_ref[j, h] = s
            out.append(jnp.maximum(m_run[h], _lane_groups(s, jnp.maximum)))
        return tuple(out)

    m_run = lax.fori_loop(0, n_steps, scores_step, tuple(jnp.full((qb, LANES), NEG, F32) for _ in range(NSA_HPG)))
    m_fin = [jnp.max(m, axis=-1, keepdims=True) for m in m_run]

    def values_step(j, carry):
        k0 = pl.multiple_of(j * SEL_KEYS, SEL_KEYS)
        v_bf = vs_ref[pl.ds(k0, SEL_KEYS), :].astype(BF16)
        out = []
        for h in range(NSA_HPG):
            l_run, acc = carry[h]
            p = jnp.exp(s_ref[j, h] - m_fin[h])
            out.append((l_run + _lane_groups(p, jnp.add), acc + _dot(p.astype(BF16), v_bf)))
        return tuple(out)

    sel_state = lax.fori_loop(0, n_steps, values_step,
                              tuple((jnp.zeros((qb, LANES), F32), jnp.zeros((qb, NSA_DK), F32))
                                    for _ in range(NSA_HPG)))

    gates = jax.nn.sigmoid(gate_ref[...])
    outs = []
    for h in range(NSA_HPG):
        l_run, acc = sel_state[h]
        o_sel = acc / jnp.sum(l_run, axis=-1, keepdims=True)
        outs.append(gates[:, 3 * h:3 * h + 1] * o_cmp[h] + gates[:, 3 * h + 1:3 * h + 2] * o_sel
                    + gates[:, 3 * h + 2:3 * h + 3] * o_win[h])
    o_ref[...] = jnp.concatenate(outs, axis=1).astype(o_ref.dtype)


def nsa_prompt(z, kcvc, b_sz, t_len):
    assert t_len % SEL_KEYS == 0 and t_len >= WINDOW + QUERY_BLOCK and t_len // SEL_BLOCK <= BLK_LANES
    nqb = t_len // QUERY_BLOCK
    n_half = t_len // CMP_STRIDE
    hd = NSA_HPG * NSA_DK

    def rows(col0):
        return pl.BlockSpec((t_len, NSA_DK), lambda b, g, i: (b, col0 // NSA_DK + g))

    def cmp_spec(kv):
        return pl.BlockSpec((None, None, None, n_half, NSA_DK), lambda b, g, i: (kv, b, g, 0, 0))

    return pl.pallas_call(
        functools.partial(_nsa_prompt_kernel, t_len=t_len),
        out_shape=jax.ShapeDtypeStruct((b_sz * t_len, NSA_HEADS * NSA_DK), BF16),
        grid=(b_sz, NSA_GROUPS, nqb),
        in_specs=[pl.BlockSpec((QUERY_BLOCK, hd), lambda b, g, i: (b * nqb + i, g)),
                  cmp_spec(0), cmp_spec(1),
                  rows(C_KS), rows(C_VS), rows(C_KW), rows(C_VW),
                  pl.BlockSpec((QUERY_BLOCK, LANES), lambda b, g, i: (b * nqb + i, C_TAIL // LANES + g))],
        out_specs=pl.BlockSpec((QUERY_BLOCK, hd), lambda b, g, i: (b * nqb + i, g)),
        scratch_shapes=[pltpu.VMEM((BLK_LANES, QUERY_BLOCK), F32),
                        pltpu.VMEM((t_len // SEL_KEYS, NSA_HPG, QUERY_BLOCK, SEL_KEYS), F32)],
        compiler_params=_params(("parallel", "parallel", "arbitrary")),
        name="nsa_prompt",
    )(z, kcvc, kcvc, z, z, z, z, z)


S_PAD = 8


def _nsa_sample_kernel(pt_ref, q_ref, kc_ref, vc_ref, kn_ref, vn_ref, kw_ref, vw_ref, gate_ref, *refs,
                       pages, past, s_len, n_cmp, ncp, n_blk, blk_rows, win_rows):
    kpages, vpages = refs[:pages], refs[pages:2 * pages]
    o_ref = refs[2 * pages]
    sel_ref, m_ref, l_ref, acc_ref, oc_ref = refs[2 * pages + 1:]
    j = pl.program_id(1)
    n_j = pl.num_programs(1)
    rows = NSA_HPG * s_len
    step = lax.broadcasted_iota(jnp.int32, (rows, 1), 0) % s_len
    pos = past + step
    qg = [(q_ref[g] * NSA_SCALE).astype(BF16) for g in range(NSA_GROUPS)]

    @pl.when(j == 0)
    def _():
        row_i = lax.broadcasted_iota(jnp.int32, (LANES, rows), 1) % s_len
        fold = (row_i == lax.broadcasted_iota(jnp.int32, (LANES, rows), 0)).astype(F32)
        row_o = lax.broadcasted_iota(jnp.int32, (rows, LANES), 0) % s_len
        unfold = (row_o == lax.broadcasted_iota(jnp.int32, (rows, LANES), 1)).astype(BF16)
        pos_row = past + lax.broadcasted_iota(jnp.int32, (1, LANES), 1)
        for g in range(NSA_GROUPS):
            kc = kc_ref[pl.ds(g, ncp, stride=NSA_GROUPS), :].astype(BF16)
            vc = vc_ref[pl.ds(g, ncp, stride=NSA_GROUPS), :].astype(BF16)
            p = _cmp_probs(_dot_nt(qg[g], kc), pos, n_cmp)
            oc_ref[g] = _dot(p.astype(BF16), vc)
            pg = _dot_exact(fold, p)
            sel_t = _select_blocks_t(_block_scores_t(pg, blk_rows), pos_row, n_blk)
            sel_ref[g] = _dot_nt(sel_t.astype(BF16), unfold)
            m_ref[g] = jnp.full((rows, 1), NEG, F32)
            l_ref[g] = jnp.zeros((rows, 1), F32)
            acc_ref[g] = jnp.zeros((rows, NSA_DK), F32)

    n_keys = pages * PAGE_SIZE
    blocks_per_step = n_keys // SEL_BLOCK
    expand = _block_expand(blocks_per_step, n_keys)
    key = j * n_keys + lax.broadcasted_iota(jnp.int32, (1, n_keys), 1)
    blk0 = pl.multiple_of(j * blocks_per_step, blocks_per_step)
    for g in range(NSA_GROUPS):
        k_bf = jnp.concatenate([kpages[p][pl.ds(g, PAGE_SIZE, stride=NSA_GROUPS), :] for p in range(pages)],
                               axis=0).astype(BF16)
        v_bf = jnp.concatenate([vpages[p][pl.ds(g, PAGE_SIZE, stride=NSA_GROUPS), :] for p in range(pages)],
                               axis=0).astype(BF16)
        chosen = _dot_tn(sel_ref[g, pl.ds(blk0, blocks_per_step), :].astype(BF16), expand) > 0.5
        m, l, acc = _online_update((m_ref[g], l_ref[g], acc_ref[g]), _dot_nt(qg[g], k_bf),
                                   chosen & (key <= pos), v_bf)
        m_ref[g] = m
        l_ref[g] = l
        acc_ref[g] = acc

    @pl.when(j == n_j - 1)
    def _():
        gates = jax.nn.sigmoid(gate_ref[...])
        new_i = lax.broadcasted_iota(jnp.int32, (1, S_PAD), 1)
        new_key = past + new_i
        new_blk = past // SEL_BLOCK
        first_row = (lax.broadcasted_iota(jnp.int32, (8, S_PAD), 0) == 0).astype(BF16)
        w_key = past - (win_rows - S_PAD) + lax.broadcasted_iota(jnp.int32, (1, win_rows), 1)
        dist = pos - w_key
        w_mask = (dist >= 0) & (dist <= WINDOW) & (w_key < past + s_len)
        for g in range(NSA_GROUPS):
            chosen_new = _dot_tn(sel_ref[g, new_blk:new_blk + 8, :].astype(BF16), first_row) > 0.5
            mask = chosen_new & (new_key <= pos) & (new_i < s_len)
            _, l, acc = _online_update((m_ref[g], l_ref[g], acc_ref[g]),
                                       _dot_nt(qg[g], kn_ref[g].astype(BF16)), mask, vn_ref[g].astype(BF16))
            o_sel = acc / l
            kw = kw_ref[pl.ds(g, win_rows, stride=NSA_GROUPS), :].astype(BF16)
            vw = vw_ref[pl.ds(g, win_rows, stride=NSA_GROUPS), :].astype(BF16)
            o_win = _masked_attend(qg[g], kw, vw, w_mask)
            gt = gates[g]
            o_ref[g] = gt[:, 0:1] * oc_ref[g] + gt[:, 1:2] * o_sel + gt[:, 2:3] * o_win


def nsa_sample(q, kc, vc, k_new, v_new, kw_all, vw_all, gate, pool_k, pool_v, page_table, s_len, pages):
    bd, n_pages = page_table.shape
    past = n_pages * PAGE_SIZE
    rows = NSA_HPG * s_len
    n_cmp = past // CMP_STRIDE - 1
    n_blk = max(-(-(past + s_len) // SEL_BLOCK), N_SEL)
    assert s_len <= S_PAD and s_len < CMP_STRIDE and n_pages % pages == 0
    assert (past // SEL_BLOCK) % 8 == 0 and (pages * PAGE_SIZE // SEL_BLOCK) % 8 == 0
    blk_rows = -(-(past // SEL_BLOCK + 8) // LANES) * LANES
    ncp = kc.shape[1] // NSA_GROUPS
    win_rows = kw_all.shape[1] // NSA_GROUPS

    def per_b(shape):
        return pl.BlockSpec((None,) + shape, lambda b, j, pt: (b,) + (0,) * len(shape))

    def page_spec(i):
        return pl.BlockSpec((None, PAGE_SIZE * NSA_GROUPS, NSA_DK), lambda b, j, pt: (pt[b, j * pages + i], 0, 0))

    grid_spec = pltpu.PrefetchScalarGridSpec(
        num_scalar_prefetch=1,
        grid=(bd, n_pages // pages),
        in_specs=[per_b((NSA_GROUPS, rows, NSA_DK)), per_b((ncp * NSA_GROUPS, NSA_DK)), per_b((ncp * NSA_GROUPS, NSA_DK)),
                  per_b((NSA_GROUPS, S_PAD, NSA_DK)), per_b((NSA_GROUPS, S_PAD, NSA_DK)),
                  per_b((win_rows * NSA_GROUPS, NSA_DK)), per_b((win_rows * NSA_GROUPS, NSA_DK)),
                  per_b((NSA_GROUPS, rows, LANES))]
        + [page_spec(i) for i in range(pages)] * 2,
        out_specs=per_b((NSA_GROUPS, rows, NSA_DK)),
        scratch_shapes=[pltpu.VMEM((NSA_GROUPS, blk_rows, rows), F32),
                        pltpu.VMEM((NSA_GROUPS, rows, 1), F32),
                        pltpu.VMEM((NSA_GROUPS, rows, 1), F32),
                        pltpu.VMEM((NSA_GROUPS, rows, NSA_DK), F32),
                        pltpu.VMEM((NSA_GROUPS, rows, NSA_DK), F32)],
    )
    return pl.pallas_call(
        functools.partial(_nsa_sample_kernel, pages=pages, past=past, s_len=s_len, n_cmp=n_cmp, ncp=ncp, n_blk=n_blk,
                          blk_rows=blk_rows, win_rows=win_rows),
        out_shape=jax.ShapeDtypeStruct((bd, NSA_GROUPS, rows, NSA_DK), F32),
        grid_spec=grid_spec,
        compiler_params=_params(("parallel", "arbitrary")),
        name="nsa_sample",
    )(page_table, q, kc, vc, k_new, v_new, kw_all, vw_all, gate, *([pool_k] * pages), *([pool_v] * pages))


def _gla_kernel(q_ref, k_ref, v_ref, gr_ref, lr_ref, wlr_ref, blr_ref, ng_ref, s0_ref, o_ref, sT_out_ref,
                st_ref, *, chunk, t_valid):
    c = pl.program_id(2)

    @pl.when(c == 0)
    def _():
        st_ref[...] = s0_ref[...].T

    row = lax.broadcasted_iota(jnp.int32, (chunk, 1), 0)
    x = _dot(lr_ref[...].astype(BF16), wlr_ref[...]) + blr_ref[...]
    log_a = jnp.where(row < t_valid, jax.nn.log_sigmoid(x) / GLA_TAU, 0.0)
    tri = (lax.broadcasted_iota(jnp.int32, (chunk, chunk), 1) <= lax.broadcasted_iota(jnp.int32, (chunk, chunk), 0))
    b = _dot_exact(tri.astype(F32), log_a)
    qs = q_ref[...] * (GLA_DK ** -0.5)
    k = k_ref[...]
    a_t = jnp.zeros((chunk, LANES), F32)
    for t in range(min(chunk, t_valid)):
        n_s = -(-(t + 1) // 8) * 8
        seen = lax.broadcasted_iota(jnp.int32, (n_s, GLA_DK), 0) <= t
        w = jnp.where(seen, jnp.exp(jnp.where(seen, b[t:t + 1] - b[0:n_s], 0.0)), 0.0)
        col = jnp.sum(k[0:n_s] * w * qs[t:t + 1], axis=-1, keepdims=True)
        filled = jnp.where(lax.broadcasted_iota(jnp.int32, (n_s, LANES), 1) == t, col, a_t[0:n_s])
        a_t = filled if n_s == chunk else jnp.concatenate([filled, a_t[n_s:]], axis=0)
    v = v_ref[...]
    v_bf = v.astype(BF16)
    st = st_ref[...]
    o = _dot_tn(a_t[:, 0:chunk].astype(BF16), v_bf) + _dot_nt((qs * jnp.exp(b)).astype(BF16), st.astype(BF16))
    b_last = b[chunk - 1:chunk, :]
    k_dec = (k * jnp.exp(b_last - b)).astype(BF16)
    st_new = st * jnp.exp(b_last) + _dot_tn(v_bf, k_dec)
    st_ref[...] = st_new
    o = _rms(o, ng_ref[...])
    gr = gr_ref[...]
    o_ref[...] = (o * (gr * jax.nn.sigmoid(gr))).astype(o_ref.dtype)

    @pl.when(c == pl.num_programs(2) - 1)
    def _():
        sT_out_ref[...] = st_new.T


def gla(z, wlr, blr, norm_g, s0, b_sz, t_len, chunk, t_valid):
    assert chunk <= LANES and chunk % 8 == 0 and t_len % chunk == 0
    nck = t_len // chunk

    def seg(col0, width):
        return pl.BlockSpec((chunk, width), lambda b, h, c: (b * nck + c, col0 // width + h))

    return pl.pallas_call(
        functools.partial(_gla_kernel, chunk=chunk, t_valid=t_valid),
        out_shape=(jax.ShapeDtypeStruct((b_sz * t_len, GLA_HEADS * GLA_DV), BF16),
                   jax.ShapeDtypeStruct((b_sz, GLA_HEADS, GLA_DK, GLA_DV), F32)),
        grid=(b_sz, GLA_HEADS, nck),
        in_specs=[seg(C_GQ, GLA_DK), seg(C_GK, GLA_DK), seg(C_GV, GLA_DV), seg(C_GR, GLA_DV),
                  pl.BlockSpec((chunk, LANES), lambda b, h, c: (b * nck + c, C_TAIL // LANES)),
                  pl.BlockSpec((None, LANES, GLA_DK), lambda b, h, c: (h, 0, 0)),
                  pl.BlockSpec((None, 1, GLA_DK), lambda b, h, c: (h, 0, 0)),
                  pl.BlockSpec((1, GLA_DV), lambda b, h, c: (0, 0)),
                  pl.BlockSpec((None, None, GLA_DK, GLA_DV), lambda b, h, c: (b, h, 0, 0))],
        out_specs=(pl.BlockSpec((chunk, GLA_DV), lambda b, h, c: (b * nck + c, h)),
                   pl.BlockSpec((None, None, GLA_DK, GLA_DV), lambda b, h, c: (b, h, 0, 0))),
        scratch_shapes=[pltpu.VMEM((GLA_DV, GLA_DK), F32)],
        compiler_params=_params(("parallel", "parallel", "arbitrary")),
        name="gla",
    )(z, z, z, z, z, wlr, blr, norm_g.reshape(1, GLA_DV), s0)


def _merge_kernel(on_ref, og_ref, wn_ref, wg_ref, ma_ref, mb_ref, y_ref):
    y = (jax.nn.sigmoid(ma_ref[...]) * _dot(on_ref[...], wn_ref[...])
         + jax.nn.sigmoid(mb_ref[...]) * _dot(og_ref[...], wg_ref[...]))
    y_ref[...] = y.astype(y_ref.dtype)


def merge(o_nsa, o_gla, w_nsa, w_gla, z, tm, tn):
    n = o_nsa.shape[0]
    return pl.pallas_call(
        _merge_kernel,
        out_shape=jax.ShapeDtypeStruct((n, D_MODEL), BF16),
        grid=(n // tm, D_MODEL // tn),
        in_specs=[pl.BlockSpec((tm, o_nsa.shape[1]), lambda i, j: (i, 0)),
                  pl.BlockSpec((tm, o_gla.shape[1]), lambda i, j: (i, 0)),
                  pl.BlockSpec((w_nsa.shape[0], tn), lambda i, j: (0, j)),
                  pl.BlockSpec((w_gla.shape[0], tn), lambda i, j: (0, j)),
                  pl.BlockSpec((tm, tn), lambda i, j: (i, C_MA // tn + j)),
                  pl.BlockSpec((tm, tn), lambda i, j: (i, C_MB // tn + j))],
        out_specs=pl.BlockSpec((tm, tn), lambda i, j: (i, j)),
        compiler_params=_params(("parallel", "arbitrary")),
        name="merge",
    )(o_nsa, o_gla, w_nsa, w_gla, z, z)


def _out_proj_kernel(y_ref, w_ref, x_ref, o_ref):
    o_ref[...] = x_ref[...] + _dot(y_ref[...], w_ref[...])


def out_proj(y, w_out, x, tm, tn):
    n = y.shape[0]
    return pl.pallas_call(
        _out_proj_kernel,
        out_shape=jax.ShapeDtypeStruct((n, D_MODEL), F32),
        grid=(n // tm, D_MODEL // tn),
        in_specs=[pl.BlockSpec((tm, D_MODEL), lambda i, j: (i, 0)),
                  pl.BlockSpec((D_MODEL, tn), lambda i, j: (0, j)),
                  pl.BlockSpec((tm, tn), lambda i, j: (i, j))],
        out_specs=pl.BlockSpec((tm, tn), lambda i, j: (i, j)),
        compiler_params=_params(("parallel", "arbitrary")),
        name="out_proj",
    )(y, w_out, x)


def _top_values(s, count):
    idx = lax.broadcasted_iota(jnp.int32, s.shape, 0).astype(F32)
    vals = []
    for _ in range(count):
        m = jnp.max(s, axis=0, keepdims=True)
        first = jnp.min(jnp.where(s == m, idx, float(s.shape[0])), axis=0, keepdims=True)
        s = jnp.where(idx == first, REMOVED, s)
        vals.append(m)
    return vals


def _peer_scores_kernel(q_ref, k1_ref, k2_ref, s1_ref, s2_ref, e1_ref, e2_ref, tau_ref):
    taus = []
    for h in range(PEER_HEADS):
        q1 = q_ref[:, h * 2 * PEER_HALF:h * 2 * PEER_HALF + PEER_HALF].astype(BF16)
        q2 = q_ref[:, h * 2 * PEER_HALF + PEER_HALF:(h + 1) * 2 * PEER_HALF].astype(BF16)
        s1 = _dot_nt(k1_ref[h], q1)
        s2 = _dot_nt(k2_ref[h], q2)
        v1 = _top_values(s1, PEER_TOPK)
        v2 = _top_values(s2, PEER_TOPK)
        v2_all = jnp.concatenate(v2, axis=0)
        cand = jnp.concatenate([v1[i] + v2_all for i in range(PEER_TOPK)], axis=0)
        top = _top_values(cand, PEER_TOPK)
        z = sum(jnp.exp(t - top[0]) for t in top)
        s1_ref[h] = s1
        s2_ref[h] = s2
        e1_ref[h] = jnp.exp(s1 - v1[0])
        e2_ref[h] = jnp.exp(s2 - v2[0]) / z
        taus.append(top[PEER_TOPK - 1])
    tau_ref[...] = jnp.concatenate(taus, axis=0)


def peer_scores(qp, keys1, keys2, tn):
    n = qp.shape[0]
    big = jax.ShapeDtypeStruct((PEER_HEADS, PEER_NKEYS, n), F32)
    big_spec = pl.BlockSpec((PEER_HEADS, PEER_NKEYS, tn), lambda i: (0, 0, i))
    key_spec = pl.BlockSpec((PEER_HEADS, PEER_NKEYS, PEER_HALF), lambda i: (0, 0, 0))
    return pl.pallas_call(
        _peer_scores_kernel,
        out_shape=(big, big, big, big, jax.ShapeDtypeStruct((PEER_HEADS, n), F32)),
        grid=(n // tn,),
        in_specs=[pl.BlockSpec((tn, qp.shape[1]), lambda i: (i, 0)), key_spec, key_spec],
        out_specs=(big_spec, big_spec, big_spec, big_spec, pl.BlockSpec((PEER_HEADS, tn), lambda i: (0, i))),
        compiler_params=_params(("parallel",)),
        name="peer_scores",
    )(qp, keys1, keys2)


PEER_I1_PER_TILE = 8


def _peer_dense_kernel(x_ref, g2_ref, gf_ref, u_ref, v_ref, s1_ref, e1_ref, s2_ref, e2_ref, tau_ref, o_ref,
                       h_ref, acc_ref):
    e = pl.program_id(1)

    @pl.when(e == 0)
    def _():
        h_ref[...] = _rms(x_ref[...], g2_ref[...]).astype(BF16)
        acc_ref[...] = jnp.zeros_like(acc_ref)

    act = jax.nn.gelu(_dot_nt(u_ref[...], h_ref[...]))
    tau = tau_ref[...]
    parts = []
    for c in range(PEER_I1_PER_TILE):
        w = jnp.zeros((PEER_NKEYS, act.shape[1]), F32)
        for h in range(PEER_HEADS):
            keep = s1_ref[h, c:c + 1, :] + s2_ref[h] >= tau[h:h + 1, :]
            w = w + jnp.where(keep, e1_ref[h, c:c + 1, :] * e2_ref[h], 0.0)
        parts.append((w * act[c * PEER_NKEYS:(c + 1) * PEER_NKEYS]).astype(BF16))
    acc_ref[...] += _dot_tn(jnp.concatenate(parts, axis=0), v_ref[...])

    @pl.when(e == pl.num_programs(1) - 1)
    def _():
        o_ref[...] = _rms(x_ref[...] + acc_ref[...], gf_ref[...])


def peer_dense(x, g2, gf, u_bf, v_bf, s1, s2, e1, e2, tau, tn):
    n = x.shape[0]
    te = PEER_I1_PER_TILE * PEER_NKEYS
    n_exp = u_bf.shape[0]
    sub = pl.BlockSpec((PEER_HEADS, PEER_I1_PER_TILE, tn), lambda i, e: (0, e, i))
    full = pl.BlockSpec((PEER_HEADS, PEER_NKEYS, tn), lambda i, e: (0, 0, i))
    vec = pl.BlockSpec((1, D_MODEL), lambda i, e: (0, 0))
    return pl.pallas_call(
        _peer_dense_kernel,
        out_shape=jax.ShapeDtypeStruct((n, D_MODEL), F32),
        grid=(n // tn, n_exp // te),
        in_specs=[pl.BlockSpec((tn, D_MODEL), lambda i, e: (i, 0)), vec, vec,
                  pl.BlockSpec((te, D_MODEL), lambda i, e: (e, 0)),
                  pl.BlockSpec((te, D_MODEL), lambda i, e: (e, 0)),
                  sub, sub, full, full,
                  pl.BlockSpec((PEER_HEADS, tn), lambda i, e: (0, i))],
        out_specs=pl.BlockSpec((tn, D_MODEL), lambda i, e: (i, 0)),
        scratch_shapes=[pltpu.VMEM((tn, D_MODEL), BF16), pltpu.VMEM((tn, D_MODEL), F32)],
        compiler_params=_params(("parallel", "arbitrary")),
        name="peer_dense",
    )(x, g2.reshape(1, D_MODEL), gf.reshape(1, D_MODEL), u_bf, v_bf, s1, e1, s2, e2, tau)


def _pack_w_in(w_in):
    offs = np.concatenate([[0], np.cumsum(IN_SPLITS)])
    seg = lambda i: w_in[:, offs[i]:offs[i + 1]]
    g_nsa, glr = seg(7), seg(12)
    per_group = 3 * NSA_HPG
    zeros = lambda n: jnp.zeros((w_in.shape[0], n), w_in.dtype)
    tail = []
    for g in range(NSA_GROUPS):
        tail.append(g_nsa[:, g * per_group:(g + 1) * per_group])
        if g == 0:
            tail += [zeros(GLR_LANE - per_group), glr, zeros(LANES - GLR_LANE - GLA_RANK)]
        else:
            tail.append(zeros(LANES - per_group))
    packed = jnp.concatenate([seg(i) for i in (0, 1, 2, 3, 4, 5, 6, 8, 9, 10, 11, 13, 14)] + tail, axis=1)
    assert packed.shape[1] == Z_COLS
    return packed.astype(BF16)


def _pack_cmp(pe, w1, w2):
    w1cat = jnp.concatenate([w1[:CMP_STRIDE].reshape(CMP_STRIDE * NSA_DK, CMP_HIDDEN),
                             w1[CMP_STRIDE:].reshape(CMP_STRIDE * NSA_DK, CMP_HIDDEN)], axis=1).astype(BF16)
    return (w1cat, pe.reshape(1, CMP_LEN * NSA_DK).astype(BF16),
            w1.reshape(CMP_LEN * NSA_DK, CMP_HIDDEN).astype(BF16), w2.astype(BF16))


def _row_tile(n, cap):
    t = min(n, cap)
    while n % t:
        t //= 2
    return t


def _channel_tail(x2d, z, o_nsa, o_gla, wts):
    n = x2d.shape[0]
    tm = _row_tile(n, 512)
    y = merge(o_nsa, o_gla, wts["w_nsa"], wts["w_gla"], z, tm, 512)
    x1 = out_proj(y, wts["w_out"], x2d, tm, 512)
    qp = norm_matmul(x1, wts["norm2_g"], wts["w_q"], tm, 512)
    tn = _row_tile(n, 256)
    s1, s2, e1, e2, tau = peer_scores(qp, wts["keys1"], wts["keys2"], tn)
    return peer_dense(x1, wts["norm2_g"], wts["norm_f_g"], wts["u"], wts["v"], s1, s2, e1, e2, tau, _row_tile(n, 512))


def kernel(x_prompt, x_sample, cache_k_cmp, cache_v_cmp, cache_k_slc, cache_v_slc, cache_k_win, cache_v_win, state_gla, page_table, norm1_g, w_in, cmp_pe_k, cmp_w1_k, cmp_w2_k, cmp_pe_v, cmp_w1_v, cmp_w2_v, gla_w_lr2, gla_b_lr, gla_norm_g, w_nsa_proj, w_gla_proj, w_out, norm2_g, peer_w_q, peer_keys1, peer_keys2, peer_u, peer_v, norm_f_g):
    b_sz, t_len, _ = x_prompt.shape
    bd, s_len, _ = x_sample.shape
    n_pool = cache_k_cmp.shape[0]
    wb = cache_k_win.shape[1]

    w_pack = _pack_w_in(w_in)
    cmp_k = _pack_cmp(cmp_pe_k, cmp_w1_k, cmp_w2_k)
    cmp_v = _pack_cmp(cmp_pe_v, cmp_w1_v, cmp_w2_v)
    cmp_kv = [jnp.stack([a, b]) for a, b in zip(cmp_k, cmp_v)]
    wlr = jnp.zeros((LANES, GLA_HEADS * GLA_DK), F32).at[GLR_LANE:GLR_LANE + GLA_RANK].set(gla_w_lr2)
    wlr = wlr.reshape(LANES, GLA_HEADS, GLA_DK).transpose(1, 0, 2).astype(BF16)
    blr = gla_b_lr.reshape(GLA_HEADS, 1, GLA_DK)
    wts = dict(w_nsa=w_nsa_proj.astype(BF16), w_gla=w_gla_proj.astype(BF16), w_out=w_out.astype(BF16),
               w_q=peer_w_q.astype(BF16), norm2_g=norm2_g, norm_f_g=norm_f_g,
               keys1=peer_keys1.astype(BF16), keys2=peer_keys2.astype(BF16),
               u=peer_u.astype(BF16), v=peer_v.astype(BF16))

    def heads_out(z, col0, lead):
        return z[:, col0:col0 + GD].reshape(lead + (NSA_GROUPS, NSA_DK))

    n_p = b_sz * t_len
    xp = x_prompt.reshape(n_p, D_MODEL)
    z_p = norm_matmul(xp, norm1_g, w_pack, _row_tile(n_p, 1024), 512)
    fs_p = half_proj_dense(z_p, cmp_kv[0], b_sz, t_len)
    kcvc_p = compress_finish(fs_p, cmp_kv[1], cmp_kv[2], cmp_kv[3], 1)
    o_nsa_p = nsa_prompt(z_p, kcvc_p, b_sz, t_len)
    s0 = jnp.zeros((b_sz, GLA_HEADS, GLA_DK, GLA_DV), F32)
    o_gla_p, gla_state_p = gla(z_p, wlr, blr, gla_norm_g, s0, b_sz, t_len, GLA_CHUNK, GLA_CHUNK)
    y_prompt = _channel_tail(xp, z_p, o_nsa_p, o_gla_p, wts).reshape(b_sz, t_len, D_MODEL)
    lead_p = (b_sz, t_len)
    kcr_p, vcr_p, ksr_p, vsr_p, kwr_p, vwr_p = (heads_out(z_p, c, lead_p) for c in (C_KC, C_VC, C_KS, C_VS, C_KW, C_VW))
    wl = min(WINDOW, t_len)
    k_win_p = kwr_p[:, t_len - wl:]
    v_win_p = vwr_p[:, t_len - wl:]

    n_s = bd * s_len
    xs = x_sample.reshape(n_s, D_MODEL)
    z_s = norm_matmul(xs, norm1_g, w_pack, _row_tile(n_s, 1024), 512)
    lead_s = (bd, s_len)
    kcr_s, vcr_s, ksr_s, vsr_s, kwr_s, vwr_s = (heads_out(z_s, c, lead_s) for c in (C_KC, C_VC, C_KS, C_VS, C_KW, C_VW))

    pool2d = lambda p: p.reshape(n_pool, PAGE_SIZE * NSA_GROUPS, NSA_DK)
    pages = 16

    def compress_pool(pool, prm):
        fs = half_proj_paged(pool2d(pool), page_table, prm[0], pages)
        return compress_finish(fs[None, :, None], prm[1][None], prm[2][None], prm[3][None], NSA_GROUPS)[0, :, 0]

    kc_s = compress_pool(cache_k_cmp, cmp_k)
    vc_s = compress_pool(cache_v_cmp, cmp_v)

    kw_all = jnp.concatenate([cache_k_win, kwr_s], axis=1)
    vw_all = jnp.concatenate([cache_v_win, vwr_s], axis=1)
    k_win_s = kw_all[:, s_len:s_len + wb]
    v_win_s = vw_all[:, s_len:s_len + wb]
    win_rows = wb + S_PAD

    def win_rows_2d(a):
        a = jnp.pad(a, ((0, 0), (0, win_rows - a.shape[1]), (0, 0), (0, 0)))
        return a.reshape(bd, win_rows * NSA_GROUPS, NSA_DK)

    def new_rows(a):
        return jnp.pad(a.transpose(0, 2, 1, 3), ((0, 0), (0, 0), (0, S_PAD - s_len), (0, 0)))

    rows = NSA_HPG * s_len
    q_s = z_s[:, C_Q:C_Q + NSA_HEADS * NSA_DK].reshape(bd, s_len, NSA_GROUPS, NSA_HPG, NSA_DK)
    q_s = q_s.transpose(0, 2, 3, 1, 4).reshape(bd, NSA_GROUPS, rows, NSA_DK)
    gate_s = z_s[:, C_TAIL:C_TAIL + 512].reshape(bd, s_len, NSA_GROUPS, LANES)[..., :3 * NSA_HPG]
    gate_s = gate_s.reshape(bd, s_len, NSA_GROUPS, NSA_HPG, 3).transpose(0, 2, 3, 1, 4).reshape(bd, NSA_GROUPS, rows, 3)
    gate_s = jnp.pad(gate_s, ((0, 0), (0, 0), (0, 0), (0, LANES - 3)))
    o_s = nsa_sample(q_s, kc_s, vc_s, new_rows(ksr_s), new_rows(vsr_s), win_rows_2d(kw_all), win_rows_2d(vw_all),
                     gate_s, pool2d(cache_k_slc), pool2d(cache_v_slc), page_table, s_len, pages)
    o_nsa_s = o_s.reshape(bd, NSA_GROUPS, NSA_HPG, s_len, NSA_DK).transpose(0, 3, 1, 2, 4)
    o_nsa_s = o_nsa_s.reshape(n_s, NSA_HEADS * NSA_DK).astype(BF16)

    z_s_pad = jnp.pad(z_s.reshape(bd, s_len, Z_COLS), ((0, 0), (0, S_PAD - s_len), (0, 0))).reshape(bd * S_PAD, Z_COLS)
    o_gla_s, gla_state_s = gla(z_s_pad, wlr, blr, gla_norm_g, state_gla, bd, S_PAD, S_PAD, s_len)
    o_gla_s = o_gla_s.reshape(bd, S_PAD, GLA_HEADS * GLA_DV)[:, :s_len].reshape(n_s, GLA_HEADS * GLA_DV)
    y_sample = _channel_tail(xs, z_s, o_nsa_s, o_gla_s, wts).reshape(bd, s_len, D_MODEL)

    return (y_prompt, y_sample, kcr_p, vcr_p, ksr_p, vsr_p, k_win_p, v_win_p, gla_state_p,
            kcr_s, vcr_s, ksr_s, vsr_s, k_win_s, v_win_s, gla_state_s)
```

```python
import functools

import numpy as np
import jax
import jax.numpy as jnp
from jax import lax
from jax.experimental import pallas as pl
from jax.experimental.pallas import tpu as pltpu

F32 = jnp.float32
BF16 = jnp.bfloat16

D_MODEL = 2048
PAGE_SIZE = 128
NSA_HEADS = 16
NSA_GROUPS = 4
NSA_HPG = NSA_HEADS // NSA_GROUPS
NSA_DK = 128
NSA_SCALE = NSA_DK ** -0.5
CMP_LEN = 32
CMP_STRIDE = 16
CMP_HIDDEN = 256
SEL_BLOCK = 64
N_SEL = 16
WINDOW = 512
QUERY_BLOCK = 128
GLA_HEADS = 4
GLA_DK = 256
GLA_DV = 512
GLA_RANK = 16
GLA_TAU = 16.0
GLA_CHUNK = 64
PEER_HEADS = 8
PEER_NKEYS = 128
PEER_HALF = 128
PEER_TOPK = 16
NORM_EPS = 1e-6
NEG = -1e30
BIG = 1e30
PAD_SCORE = -3e38
REMOVED = -float("inf")

IN_SPLITS = (NSA_HEADS * NSA_DK,) + (NSA_GROUPS * NSA_DK,) * 6 + (
    3 * NSA_HEADS, GLA_HEADS * GLA_DK, GLA_HEADS * GLA_DK, GLA_HEADS * GLA_DV,
    GLA_HEADS * GLA_DV, GLA_RANK, D_MODEL, D_MODEL)

GD = NSA_GROUPS * NSA_DK
C_Q = 0
C_KC = C_Q + NSA_HEADS * NSA_DK
C_VC = C_KC + GD
C_KS = C_VC + GD
C_VS = C_KS + GD
C_KW = C_VS + GD
C_VW = C_KW + GD
C_GQ = C_VW + GD
C_GK = C_GQ + GLA_HEADS * GLA_DK
C_GV = C_GK + GLA_HEADS * GLA_DK
C_GR = C_GV + GLA_HEADS * GLA_DV
C_MA = C_GR + GLA_HEADS * GLA_DV
C_MB = C_MA + D_MODEL
C_TAIL = C_MB + D_MODEL
Z_COLS = C_TAIL + 512
GLR_LANE = 16
LANES = 128

VMEM_LIMIT = 56 * 1024 * 1024


def _params(sem):
    return pltpu.CompilerParams(dimension_semantics=sem, vmem_limit_bytes=VMEM_LIMIT)


def _dot(a, b):
    return jnp.dot(a, b, preferred_element_type=F32)


def _dot_nt(a, b):
    return lax.dot_general(a, b, (((1,), (1,)), ((), ())), preferred_element_type=F32)


def _dot_tn(a, b):
    return lax.dot_general(a, b, (((0,), (0,)), ((), ())), preferred_element_type=F32)


def _dot_exact(a, b):
    return jnp.dot(a, b, preferred_element_type=F32, precision=lax.Precision.HIGHEST)


def _rms(x, g):
    return x * lax.rsqrt(jnp.mean(x * x, axis=-1, keepdims=True) + NORM_EPS) * g


GELU_C0 = 2.0 * (2.0 / np.pi) ** 0.5
GELU_C1 = GELU_C0 * 0.044715


def _gelu_tanh(x):
    return x / (1.0 + jnp.exp(-(x * (GELU_C0 + GELU_C1 * (x * x)))))


def _dot_nt_split(small_ints_bf16, x):
    hi = x.astype(BF16)
    rest = x - hi.astype(F32)
    mid = rest.astype(BF16)
    lo = (rest - mid.astype(F32)).astype(BF16)
    return _dot_nt(small_ints_bf16, hi) + _dot_nt(small_ints_bf16, mid) + _dot_nt(small_ints_bf16, lo)


def _norm_matmul_kernel(x_ref, g_ref, w_ref, o_ref, h_ref):
    @pl.when(pl.program_id(1) == 0)
    def _():
        h_ref[...] = _rms(x_ref[...], g_ref[...]).astype(BF16)

    o_ref[...] = _dot(h_ref[...], w_ref[...])


def norm_matmul(x, g, w, tm, tn):
    n, d = x.shape
    cols = w.shape[1]
    return pl.pallas_call(
        _norm_matmul_kernel,
        out_shape=jax.ShapeDtypeStruct((n, cols), F32),
        grid=(n // tm, cols // tn),
        in_specs=[pl.BlockSpec((tm, d), lambda i, j: (i, 0)),
                  pl.BlockSpec((1, d), lambda i, j: (0, 0)),
                  pl.BlockSpec((d, tn), lambda i, j: (0, j))],
        out_specs=pl.BlockSpec((tm, tn), lambda i, j: (i, j)),
        scratch_shapes=[pltpu.VMEM((tm, d), BF16)],
        compiler_params=_params(("parallel", "arbitrary")),
        name="norm_matmul",
    )(x, g.reshape(1, d), w)


def _half_rows(ref, first_row, n_half, row_stride, lane0):
    parts = [ref[pl.ds(first_row + r * row_stride, n_half, stride=CMP_STRIDE * row_stride), pl.ds(lane0, NSA_DK)]
             for r in range(CMP_STRIDE)]
    return jnp.concatenate(parts, axis=1)


def _half_proj_dense_kernel(x_ref, w_ref, o_ref, *, n_half):
    o_ref[...] = _dot(_half_rows(x_ref, 0, n_half, 1, 0).astype(BF16), w_ref[...])


def half_proj_dense(z, w1cat, b_sz, t_len):
    n_half = t_len // CMP_STRIDE
    return pl.pallas_call(
        functools.partial(_half_proj_dense_kernel, n_half=n_half),
        out_shape=jax.ShapeDtypeStruct((2, b_sz, NSA_GROUPS, n_half, 2 * CMP_HIDDEN), F32),
        grid=(2, b_sz, NSA_GROUPS),
        in_specs=[pl.BlockSpec((t_len, NSA_DK), lambda kv, b, g: (b, C_KC // NSA_DK + kv * NSA_GROUPS + g)),
                  pl.BlockSpec((None, CMP_STRIDE * NSA_DK, 2 * CMP_HIDDEN), lambda kv, b, g: (kv, 0, 0))],
        out_specs=pl.BlockSpec((None, None, None, n_half, 2 * CMP_HIDDEN), lambda kv, b, g: (kv, b, g, 0, 0)),
        compiler_params=_params(("parallel", "parallel", "parallel")),
        name="half_proj_dense",
    )(z, w1cat)


HALVES_PER_PAGE = PAGE_SIZE // CMP_STRIDE


def _half_proj_paged_kernel(pt_ref, *refs, pages):
    page_refs, w_ref, o_ref = refs[:pages], refs[pages], refs[pages + 1]
    tile = 2 * NSA_GROUPS
    rows_per_half = CMP_STRIDE * NSA_GROUPS
    low = lax.broadcasted_iota(jnp.int32, (tile, NSA_DK), 0) < NSA_GROUPS
    acc = None
    for r2 in range(CMP_STRIDE // 2):
        even, odd = [], []
        for p in range(pages):
            for n in range(0, HALVES_PER_PAGE, 2):
                a = page_refs[p][pl.ds(n * rows_per_half + tile * r2, tile), :]
                b = page_refs[p][pl.ds((n + 1) * rows_per_half + tile * r2, tile), :]
                even.append(jnp.where(low, a, pltpu.roll(b, NSA_GROUPS, 0)))
                odd.append(jnp.where(low, pltpu.roll(a, NSA_GROUPS, 0), b))
        lhs = jnp.concatenate([jnp.concatenate(even, axis=0), jnp.concatenate(odd, axis=0)], axis=1).astype(BF16)
        part = _dot(lhs, w_ref[pl.ds(r2 * 2 * NSA_DK, 2 * NSA_DK), :])
        acc = part if acc is None else acc + part
    o_ref[...] = acc


def half_proj_paged(pool2d, page_table, w1cat_one, pages):
    bd, n_pages = page_table.shape
    n_half = n_pages * HALVES_PER_PAGE
    rows = pages * HALVES_PER_PAGE * NSA_GROUPS

    def page_spec(i):
        return pl.BlockSpec((None, PAGE_SIZE * NSA_GROUPS, NSA_DK),
                            lambda b, j, pt: (pt[b, j * pages + i], 0, 0))

    grid_spec = pltpu.PrefetchScalarGridSpec(
        num_scalar_prefetch=1,
        grid=(bd, n_pages // pages),
        in_specs=[page_spec(i) for i in range(pages)]
        + [pl.BlockSpec((CMP_STRIDE * NSA_DK, 2 * CMP_HIDDEN), lambda b, j, pt: (0, 0))],
        out_specs=pl.BlockSpec((None, rows, 2 * CMP_HIDDEN), lambda b, j, pt: (b, j, 0)),
    )
    return pl.pallas_call(
        functools.partial(_half_proj_paged_kernel, pages=pages),
        out_shape=jax.ShapeDtypeStruct((bd, n_half * NSA_GROUPS, 2 * CMP_HIDDEN), F32),
        grid_spec=grid_spec,
        compiler_params=_params(("parallel", "arbitrary")),
        name="half_proj_paged",
    )(page_table, *([pool2d] * pages), w1cat_one)


def _compress_finish_kernel(fs_ref, pe_ref, w1_ref, w2_ref, o_ref, *, n_rows, step):
    pe = jnp.broadcast_to(pe_ref[...], (8, CMP_LEN * NSA_DK))
    c = _dot(pe, w1_ref[...])[0:1]
    first = fs_ref[:, 0:CMP_HIDDEN]
    second_next = pltpu.roll(fs_ref[:, CMP_HIDDEN:2 * CMP_HIDDEN], n_rows - step, 0)
    hid = _gelu_tanh(first + second_next + c)
    out = _dot(hid.astype(BF16), w2_ref[...])
    row = lax.broadcasted_iota(jnp.int32, (n_rows, 1), 0)
    o_ref[...] = jnp.where(row < n_rows - step, out, 0.0)


def compress_finish(fs, pe, w1, w2, step):
    kv, b_sz, n_sets, n_rows, _ = fs.shape
    return pl.pallas_call(
        functools.partial(_compress_finish_kernel, n_rows=n_rows, step=step),
        out_shape=jax.ShapeDtypeStruct((kv, b_sz, n_sets, n_rows, NSA_DK), F32),
        grid=(kv, b_sz, n_sets),
        in_specs=[pl.BlockSpec((None, None, None, n_rows, 2 * CMP_HIDDEN), lambda k, b, g: (k, b, g, 0, 0)),
                  pl.BlockSpec((None, 1, CMP_LEN * NSA_DK), lambda k, b, g: (k, 0, 0)),
                  pl.BlockSpec((None, CMP_LEN * NSA_DK, CMP_HIDDEN), lambda k, b, g: (k, 0, 0)),
                  pl.BlockSpec((None, CMP_HIDDEN, NSA_DK), lambda k, b, g: (k, 0, 0))],
        out_specs=pl.BlockSpec((None, None, None, n_rows, NSA_DK), lambda k, b, g: (k, b, g, 0, 0)),
        compiler_params=_params(("parallel", "parallel", "parallel")),
        name="compress_finish",
    )(fs, pe, w1, w2)


def _cmp_probs(s, pos, n_cmp):
    c = lax.broadcasted_iota(jnp.int32, (1, s.shape[1]), 1)
    mask = (c * CMP_STRIDE + (CMP_LEN - 1) <= pos) & (c < n_cmp)
    s = jnp.where(mask, s, NEG)
    m = jnp.max(s, axis=-1, keepdims=True)
    e = jnp.where(mask, jnp.exp(s - m), 0.0)
    l = jnp.sum(e, axis=-1, keepdims=True)
    return e / jnp.where(l > 0.0, l, 1.0)


def _block_scores_t(pg, n_rows):
    ncp = pg.shape[1]
    blk = lax.broadcasted_iota(jnp.int32, (n_rows, ncp), 0)
    c = lax.broadcasted_iota(jnp.int32, (n_rows, ncp), 1)
    hpb = SEL_BLOCK // CMP_STRIDE
    spread = (c // hpb == blk).astype(F32) + ((c + 1) // hpb == blk).astype(F32)
    return _dot_nt_split(spread.astype(BF16), pg)


def _select_blocks_t(ps_t, pos_row, n_blk):
    blk = lax.broadcasted_iota(jnp.int32, ps_t.shape, 0)
    cur = pos_row // SEL_BLOCK
    forced = (blk == 0) | (blk == cur) | (blk == cur - 1)
    score = jnp.where(forced, BIG, jnp.where(blk <= cur, ps_t, NEG))
    score = jnp.where(blk < n_blk, score, PAD_SCORE)
    blk_f = blk.astype(F32)
    sel = jnp.zeros(ps_t.shape, F32)
    for _ in range(N_SEL):
        m = jnp.max(score, axis=0, keepdims=True)
        first = jnp.min(jnp.where(score == m, blk_f, float(ps_t.shape[0])), axis=0, keepdims=True)
        hit = blk_f == first
        sel = jnp.where(hit, 1.0, sel)
        score = jnp.where(hit, REMOVED, score)
    return sel


def _lane_groups(x, op):
    parts = [x[:, i * LANES:(i + 1) * LANES] for i in range(x.shape[1] // LANES)]
    while len(parts) > 1:
        parts = [op(parts[i], parts[i + 1]) for i in range(0, len(parts) - 1, 2)] + (
            [parts[-1]] if len(parts) % 2 else [])
    return parts[0]


def _block_expand(n_blocks, n_keys):
    blk = lax.broadcasted_iota(jnp.int32, (n_blocks, n_keys), 0)
    key = lax.broadcasted_iota(jnp.int32, (n_blocks, n_keys), 1)
    return (key // SEL_BLOCK == blk).astype(BF16)


def _online_update(state, s, mask, v_bf):
    m, l, acc = state
    s = jnp.where(mask, s, NEG)
    m_new = jnp.maximum(m, jnp.max(s, axis=-1, keepdims=True))
    alpha = jnp.exp(m - m_new)
    p = jnp.where(mask, jnp.exp(s - m_new), 0.0)
    l = alpha * l + jnp.sum(p, axis=-1, keepdims=True)
    acc = alpha * acc + _dot(p.astype(BF16), v_bf)
    return m_new, l, acc


def _masked_attend(q_bf, k_bf, v_bf, mask):
    s = jnp.where(mask, _dot_nt(q_bf, k_bf), NEG)
    m = jnp.max(s, axis=-1, keepdims=True)
    e = jnp.where(mask, jnp.exp(s - m), 0.0)
    l = jnp.sum(e, axis=-1, keepdims=True)
    return _dot((e / l).astype(BF16), v_bf)


SEL_KEYS = 512
BLK_LANES = 128


def _nsa_prompt_kernel(q_ref, kc_ref, vc_ref, ks_ref, vs_ref, kw_ref, vw_ref, gate_ref, o_ref, sel_ref, s_ref,
                       *, t_len):
    qb = QUERY_BLOCK
    start = pl.program_id(2) * qb
    pos = start + lax.broadcasted_iota(jnp.int32, (qb, 1), 0)
    pos_row = start + lax.broadcasted_iota(jnp.int32, (1, qb), 1)
    n_cmp = t_len // CMP_STRIDE - 1
    n_blk = max(-(-t_len // SEL_BLOCK), N_SEL)
    q = q_ref[...] * NSA_SCALE
    qh = [q[:, h * NSA_DK:(h + 1) * NSA_DK].astype(BF16) for h in range(NSA_HPG)]

    kc = kc_ref[...].astype(BF16)
    vc = vc_ref[...].astype(BF16)
    o_cmp = []
    pg = jnp.zeros((qb, kc.shape[0]), F32)
    for h in range(NSA_HPG):
        p = _cmp_probs(_dot_nt(qh[h], kc), pos, n_cmp)
        o_cmp.append(_dot(p.astype(BF16), vc))
        pg = pg + p
    sel_ref[...] = _select_blocks_t(_block_scores_t(pg, BLK_LANES), pos_row, n_blk)

    w_rows = WINDOW + qb
    base = pl.multiple_of(jnp.maximum(start - WINDOW, 0), qb)
    kw = kw_ref[pl.ds(base, w_rows), :].astype(BF16)
    vw = vw_ref[pl.ds(base, w_rows), :].astype(BF16)
    dist = pos - (base + lax.broadcasted_iota(jnp.int32, (1, w_rows), 1))
    w_mask = (dist >= 0) & (dist <= WINDOW)
    o_win = [_masked_attend(qh[h], kw, vw, w_mask) for h in range(NSA_HPG)]

    n_steps = (start + qb + SEL_KEYS - 1) // SEL_KEYS
    blocks_per_step = SEL_KEYS // SEL_BLOCK
    expand = _block_expand(blocks_per_step, SEL_KEYS)

    def scores_step(j, m_run):
        k0 = pl.multiple_of(j * SEL_KEYS, SEL_KEYS)
        k_bf = ks_ref[pl.ds(k0, SEL_KEYS), :].astype(BF16)
        key = k0 + lax.broadcasted_iota(jnp.int32, (1, SEL_KEYS), 1)
        chosen = sel_ref[pl.ds(pl.multiple_of(j * blocks_per_step, blocks_per_step), blocks_per_step), :]
        mask = (_dot_tn(chosen.astype(BF16), expand) > 0.5) & (key <= pos)
        out = []
        for h in range(NSA_HPG):
            s = jnp.where(mask, _dot_nt(qh[h], k_bf), NEG)
            s_ref[j, h] = s
            out.append(jnp.maximum(m_run[h], _lane_groups(s, jnp.maximum)))
        return tuple(out)

    m_run = lax.fori_loop(0, n_steps, scores_step, tuple(jnp.full((qb, LANES), NEG, F32) for _ in range(NSA_HPG)))
    m_fin = [jnp.max(m, axis=-1, keepdims=True) for m in m_run]

    def values_step(j, carry):
        k0 = pl.multiple_of(j * SEL_KEYS, SEL_KEYS)
        v_bf = vs_ref[pl.ds(k0, SEL_KEYS), :].astype(BF16)
        out = []
        for h in range(NSA_HPG):
            l_run, acc = carry[h]
            p = jnp.exp(s_ref[j, h] - m_fin[h])
            out.append((l_run + _lane_groups(p, jnp.add), acc + _dot(p.astype(BF16), v_bf)))
        return tuple(out)

    sel_state = lax.fori_loop(0, n_steps, values_step,
                              tuple((jnp.zeros((qb, LANES), F32), jnp.zeros((qb, NSA_DK), F32))
                                    for _ in range(NSA_HPG)))

    gates = jax.nn.sigmoid(gate_ref[...])
    outs = []
    for h in range(NSA_HPG):
        l_run, acc = sel_state[h]
        o_sel = acc / jnp.sum(l_run, axis=-1, keepdims=True)
        outs.append(gates[:, 3 * h:3 * h + 1] * o_cmp[h] + gates[:, 3 * h + 1:3 * h + 2] * o_sel
                    + gates[:, 3 * h + 2:3 * h + 3] * o_win[h])
    o_ref[...] = jnp.concatenate(outs, axis=1).astype(o_ref.dtype)


def nsa_prompt(z, kcvc, b_sz, t_len):
    assert t_len % SEL_KEYS == 0 and t_len >= WINDOW + QUERY_BLOCK and t_len // SEL_BLOCK <= BLK_LANES
    nqb = t_len // QUERY_BLOCK
    n_half = t_len // CMP_STRIDE
    hd = NSA_HPG * NSA_DK

    def rows(col0):
        return pl.BlockSpec((t_len, NSA_DK), lambda b, g, i: (b, col0 // NSA_DK + g))

    def cmp_spec(kv):
        return pl.BlockSpec((None, None, None, n_half, NSA_DK), lambda b, g, i: (kv, b, g, 0, 0))

    return pl.pallas_call(
        functools.partial(_nsa_prompt_kernel, t_len=t_len),
        out_shape=jax.ShapeDtypeStruct((b_sz * t_len, NSA_HEADS * NSA_DK), BF16),
        grid=(b_sz, NSA_GROUPS, nqb),
        in_specs=[pl.BlockSpec((QUERY_BLOCK, hd), lambda b, g, i: (b * nqb + i, g)),
                  cmp_spec(0), cmp_spec(1),
                  rows(C_KS), rows(C_VS), rows(C_KW), rows(C_VW),
                  pl.BlockSpec((QUERY_BLOCK, LANES), lambda b, g, i: (b * nqb + i, C_TAIL // LANES + g))],
        out_specs=pl.BlockSpec((QUERY_BLOCK, hd), lambda b, g, i: (b * nqb + i, g)),
        scratch_shapes=[pltpu.VMEM((BLK_LANES, QUERY_BLOCK), F32),
                        pltpu.VMEM((t_len // SEL_KEYS, NSA_HPG, QUERY_BLOCK, SEL_KEYS), F32)],
        compiler_params=_params(("parallel", "parallel", "arbitrary")),
        name="nsa_prompt",
    )(z, kcvc, kcvc, z, z, z, z, z)


S_PAD = 8


def _nsa_sample_kernel(pt_ref, q_ref, kc_ref, vc_ref, kn_ref, vn_ref, kw_ref, vw_ref, gate_ref, *refs,
                       pages, past, s_len, n_cmp, ncp, n_blk, blk_rows, win_rows):
    kpages, vpages = refs[:pages], refs[pages:2 * pages]
    o_ref = refs[2 * pages]
    sel_ref, m_ref, l_ref, acc_ref, oc_ref = refs[2 * pages + 1:]
    j = pl.program_id(1)
    n_j = pl.num_programs(1)
    rows = NSA_HPG * s_len
    step = lax.broadcasted_iota(jnp.int32, (rows, 1), 0) % s_len
    pos = past + step
    qg = [(q_ref[g] * NSA_SCALE).astype(BF16) for g in range(NSA_GROUPS)]

    @pl.when(j == 0)
    def _():
        step_i = lax.broadcasted_iota(jnp.int32, (LANES, rows), 1) % s_len
        lane_i = lax.broadcasted_iota(jnp.int32, (LANES, rows), 0)
        step_o = lax.broadcasted_iota(jnp.int32, (rows, LANES), 0) % s_len
        lane_o = lax.broadcasted_iota(jnp.int32, (rows, LANES), 1)
        pos_row = past + lax.broadcasted_iota(jnp.int32, (1, LANES), 1) % S_PAD
        pg = jnp.zeros((LANES, ncp), F32)
        for g in range(NSA_GROUPS):
            kc = kc_ref[pl.ds(g, ncp, stride=NSA_GROUPS), :].astype(BF16)
            vc = vc_ref[pl.ds(g, ncp, stride=NSA_GROUPS), :].astype(BF16)
            p = _cmp_probs(_dot_nt(qg[g], kc), pos, n_cmp)
            oc_ref[g] = _dot(p.astype(BF16), vc)
            fold = (step_i + g * S_PAD == lane_i).astype(F32)
            pg = pg + _dot_exact(fold, p)
            m_ref[g] = jnp.full((rows, 1), NEG, F32)
            l_ref[g] = jnp.zeros((rows, 1), F32)
            acc_ref[g] = jnp.zeros((rows, NSA_DK), F32)
        sel_t = _select_blocks_t(_block_scores_t(pg, blk_rows), pos_row, n_blk).astype(BF16)
        for g in range(NSA_GROUPS):
            unfold = (step_o + g * S_PAD == lane_o).astype(BF16)
            sel_ref[g] = _dot_nt(sel_t, unfold)

    n_keys = pages * PAGE_SIZE
    blocks_per_step = n_keys // SEL_BLOCK
    expand = _block_expand(blocks_per_step, n_keys)
    key = j * n_keys + lax.broadcasted_iota(jnp.int32, (1, n_keys), 1)
    blk0 = pl.multiple_of(j * blocks_per_step, blocks_per_step)
    for g in range(NSA_GROUPS):
        k_bf = jnp.concatenate([kpages[p][pl.ds(g, PAGE_SIZE, stride=NSA_GROUPS), :] for p in range(pages)],
                               axis=0).astype(BF16)
        v_bf = jnp.concatenate([vpages[p][pl.ds(g, PAGE_SIZE, stride=NSA_GROUPS), :] for p in range(pages)],
                               axis=0).astype(BF16)
        chosen = _dot_tn(sel_ref[g, pl.ds(blk0, blocks_per_step), :].astype(BF16), expand) > 0.5
        m, l, acc = _online_update((m_ref[g], l_ref[g], acc_ref[g]), _dot_nt(qg[g], k_bf),
                                   chosen & (key <= pos), v_bf)
        m_ref[g] = m
        l_ref[g] = l
        acc_ref[g] = acc

    @pl.when(j == n_j - 1)
    def _():
        gates = jax.nn.sigmoid(gate_ref[...])
        new_i = lax.broadcasted_iota(jnp.int32, (1, S_PAD), 1)
        new_key = past + new_i
        new_blk = past // SEL_BLOCK
        first_row = (lax.broadcasted_iota(jnp.int32, (8, S_PAD), 0) == 0).astype(BF16)
        w_key = past - (win_rows - S_PAD) + lax.broadcasted_iota(jnp.int32, (1, win_rows), 1)
        dist = pos - w_key
        w_mask = (dist >= 0) & (dist <= WINDOW) & (w_key < past + s_len)
        for g in range(NSA_GROUPS):
            chosen_new = _dot_tn(sel_ref[g, new_blk:new_blk + 8, :].astype(BF16), first_row) > 0.5
            mask = chosen_new & (new_key <= pos) & (new_i < s_len)
            _, l, acc = _online_update((m_ref[g], l_ref[g], acc_ref[g]),
                                       _dot_nt(qg[g], kn_ref[g].astype(BF16)), mask, vn_ref[g].astype(BF16))
            o_sel = acc / l
            kw = kw_ref[pl.ds(g, win_rows, stride=NSA_GROUPS), :].astype(BF16)
            vw = vw_ref[pl.ds(g, win_rows, stride=NSA_GROUPS), :].astype(BF16)
            o_win = _masked_attend(qg[g], kw, vw, w_mask)
            gt = gates[g]
            o_ref[g] = gt[:, 0:1] * oc_ref[g] + gt[:, 1:2] * o_sel + gt[:, 2:3] * o_win


def nsa_sample(q, kc, vc, k_new, v_new, kw_all, vw_all, gate, pool_k, pool_v, page_table, s_len, pages):
    bd, n_pages = page_table.shape
    past = n_pages * PAGE_SIZE
    rows = NSA_HPG * s_len
    n_cmp = past // CMP_STRIDE - 1
    n_blk = max(-(-(past + s_len) // SEL_BLOCK), N_SEL)
    assert s_len <= S_PAD and s_len < CMP_STRIDE and n_pages % pages == 0
    assert (past // SEL_BLOCK) % 8 == 0 and (pages * PAGE_SIZE // SEL_BLOCK) % 8 == 0
    blk_rows = -(-(past // SEL_BLOCK + 8) // LANES) * LANES
    ncp = kc.shape[1] // NSA_GROUPS
    win_rows = kw_all.shape[1] // NSA_GROUPS

    def per_b(shape):
        return pl.BlockSpec((None,) + shape, lambda b, j, pt: (b,) + (0,) * len(shape))

    def page_spec(i):
        return pl.BlockSpec((None, PAGE_SIZE * NSA_GROUPS, NSA_DK), lambda b, j, pt: (pt[b, j * pages + i], 0, 0))

    grid_spec = pltpu.PrefetchScalarGridSpec(
        num_scalar_prefetch=1,
        grid=(bd, n_pages // pages),
        in_specs=[per_b((NSA_GROUPS, rows, NSA_DK)), per_b((ncp * NSA_GROUPS, NSA_DK)), per_b((ncp * NSA_GROUPS, NSA_DK)),
                  per_b((NSA_GROUPS, S_PAD, NSA_DK)), per_b((NSA_GROUPS, S_PAD, NSA_DK)),
                  per_b((win_rows * NSA_GROUPS, NSA_DK)), per_b((win_rows * NSA_GROUPS, NSA_DK)),
                  per_b((NSA_GROUPS, rows, LANES))]
        + [page_spec(i) for i in range(pages)] * 2,
        out_specs=per_b((NSA_GROUPS, rows, NSA_DK)),
        scratch_shapes=[pltpu.VMEM((NSA_GROUPS, blk_rows, rows), F32),
                        pltpu.VMEM((NSA_GROUPS, rows, 1), F32),
                        pltpu.VMEM((NSA_GROUPS, rows, 1), F32),
                        pltpu.VMEM((NSA_GROUPS, rows, NSA_DK), F32),
                        pltpu.VMEM((NSA_GROUPS, rows, NSA_DK), F32)],
    )
    return pl.pallas_call(
        functools.partial(_nsa_sample_kernel, pages=pages, past=past, s_len=s_len, n_cmp=n_cmp, ncp=ncp, n_blk=n_blk,
                          blk_rows=blk_rows, win_rows=win_rows),
        out_shape=jax.ShapeDtypeStruct((bd, NSA_GROUPS, rows, NSA_DK), F32),
        grid_spec=grid_spec,
        compiler_params=_params(("parallel", "arbitrary")),
        name="nsa_sample",
    )(page_table, q, kc, vc, k_new, v_new, kw_all, vw_all, gate, *([pool_k] * pages), *([pool_v] * pages))


def _gla_kernel(q_ref, k_ref, v_ref, gr_ref, lr_ref, wlr_ref, blr_ref, ng_ref, s0_ref, o_ref, sT_out_ref,
                st_ref, *, chunk, t_valid):
    c = pl.program_id(2)

    @pl.when(c == 0)
    def _():
        st_ref[...] = s0_ref[...].T

    row = lax.broadcasted_iota(jnp.int32, (chunk, 1), 0)
    x = _dot(lr_ref[...].astype(BF16), wlr_ref[...]) + blr_ref[...]
    log_a = jnp.where(row < t_valid, jax.nn.log_sigmoid(x) / GLA_TAU, 0.0)
    tri = (lax.broadcasted_iota(jnp.int32, (chunk, chunk), 1) <= lax.broadcasted_iota(jnp.int32, (chunk, chunk), 0))
    b = _dot_exact(tri.astype(F32), log_a)
    qs = q_ref[...] * (GLA_DK ** -0.5)
    k = k_ref[...]
    a_t = jnp.zeros((chunk, LANES), F32)
    for t in range(min(chunk, t_valid)):
        n_s = -(-(t + 1) // 8) * 8
        seen = lax.broadcasted_iota(jnp.int32, (n_s, GLA_DK), 0) <= t
        w = jnp.where(seen, jnp.exp(jnp.where(seen, b[t:t + 1] - b[0:n_s], 0.0)), 0.0)
        col = jnp.sum(k[0:n_s] * w * qs[t:t + 1], axis=-1, keepdims=True)
        filled = jnp.where(lax.broadcasted_iota(jnp.int32, (n_s, LANES), 1) == t, col, a_t[0:n_s])
        a_t = filled if n_s == chunk else jnp.concatenate([filled, a_t[n_s:]], axis=0)
    v = v_ref[...]
    v_bf = v.astype(BF16)
    st = st_ref[...]
    o = _dot_tn(a_t[:, 0:chunk].astype(BF16), v_bf) + _dot_nt((qs * jnp.exp(b)).astype(BF16), st.astype(BF16))
    b_last = b[chunk - 1:chunk, :]
    k_dec = (k * jnp.exp(b_last - b)).astype(BF16)
    st_new = st * jnp.exp(b_last) + _dot_tn(v_bf, k_dec)
    st_ref[...] = st_new
    o = _rms(o, ng_ref[...])
    gr = gr_ref[...]
    o_ref[...] = (o * (gr * jax.nn.sigmoid(gr))).astype(o_ref.dtype)

    @pl.when(c == pl.num_programs(2) - 1)
    def _():
        sT_out_ref[...] = st_new.T


def gla(z, wlr, blr, norm_g, s0, b_sz, t_len, chunk, t_valid):
    assert chunk <= LANES and chunk % 8 == 0 and t_len % chunk == 0
    nck = t_len // chunk

    def seg(col0, width):
        return pl.BlockSpec((chunk, width), lambda b, h, c: (b * nck + c, col0 // width + h))

    return pl.pallas_call(
        functools.partial(_gla_kernel, chunk=chunk, t_valid=t_valid),
        out_shape=(jax.ShapeDtypeStruct((b_sz * t_len, GLA_HEADS * GLA_DV), BF16),
                   jax.ShapeDtypeStruct((b_sz, GLA_HEADS, GLA_DK, GLA_DV), F32)),
        grid=(b_sz, GLA_HEADS, nck),
        in_specs=[seg(C_GQ, GLA_DK), seg(C_GK, GLA_DK), seg(C_GV, GLA_DV), seg(C_GR, GLA_DV),
                  pl.BlockSpec((chunk, LANES), lambda b, h, c: (b * nck + c, C_TAIL // LANES)),
                  pl.BlockSpec((None, LANES, GLA_DK), lambda b, h, c: (h, 0, 0)),
                  pl.BlockSpec((None, 1, GLA_DK), lambda b, h, c: (h, 0, 0)),
                  pl.BlockSpec((1, GLA_DV), lambda b, h, c: (0, 0)),
                  pl.BlockSpec((None, None, GLA_DK, GLA_DV), lambda b, h, c: (b, h, 0, 0))],
        out_specs=(pl.BlockSpec((chunk, GLA_DV), lambda b, h, c: (b * nck + c, h)),
                   pl.BlockSpec((None, None, GLA_DK, GLA_DV), lambda b, h, c: (b, h, 0, 0))),
        scratch_shapes=[pltpu.VMEM((GLA_DV, GLA_DK), F32)],
        compiler_params=_params(("parallel", "parallel", "arbitrary")),
        name="gla",
    )(z, z, z, z, z, wlr, blr, norm_g.reshape(1, GLA_DV), s0)


def _merge_kernel(on_ref, og_ref, wn_ref, wg_ref, ma_ref, mb_ref, y_ref):
    y = (jax.nn.sigmoid(ma_ref[...]) * _dot(on_ref[...], wn_ref[...])
         + jax.nn.sigmoid(mb_ref[...]) * _dot(og_ref[...], wg_ref[...]))
    y_ref[...] = y.astype(y_ref.dtype)


def merge(o_nsa, o_gla, w_nsa, w_gla, z, tm, tn):
    n = o_nsa.shape[0]
    return pl.pallas_call(
        _merge_kernel,
        out_shape=jax.ShapeDtypeStruct((n, D_MODEL), BF16),
        grid=(n // tm, D_MODEL // tn),
        in_specs=[pl.BlockSpec((tm, o_nsa.shape[1]), lambda i, j: (i, 0)),
                  pl.BlockSpec((tm, o_gla.shape[1]), lambda i, j: (i, 0)),
                  pl.BlockSpec((w_nsa.shape[0], tn), lambda i, j: (0, j)),
                  pl.BlockSpec((w_gla.shape[0], tn), lambda i, j: (0, j)),
                  pl.BlockSpec((tm, tn), lambda i, j: (i, C_MA // tn + j)),
                  pl.BlockSpec((tm, tn), lambda i, j: (i, C_MB // tn + j))],
        out_specs=pl.BlockSpec((tm, tn), lambda i, j: (i, j)),
        compiler_params=_params(("parallel", "arbitrary")),
        name="merge",
    )(o_nsa, o_gla, w_nsa, w_gla, z, z)


def _out_proj_kernel(y_ref, w_ref, x_ref, o_ref):
    o_ref[...] = x_ref[...] + _dot(y_ref[...], w_ref[...])


def out_proj(y, w_out, x, tm, tn):
    n = y.shape[0]
    return pl.pallas_call(
        _out_proj_kernel,
        out_shape=jax.ShapeDtypeStruct((n, D_MODEL), F32),
        grid=(n // tm, D_MODEL // tn),
        in_specs=[pl.BlockSpec((tm, D_MODEL), lambda i, j: (i, 0)),
                  pl.BlockSpec((D_MODEL, tn), lambda i, j: (0, j)),
                  pl.BlockSpec((tm, tn), lambda i, j: (i, j))],
        out_specs=pl.BlockSpec((tm, tn), lambda i, j: (i, j)),
        compiler_params=_params(("parallel", "arbitrary")),
        name="out_proj",
    )(y, w_out, x)


def _top_values(s, count):
    vals = []
    for _ in range(count):
        m = jnp.max(s, axis=0, keepdims=True)
        s = jnp.where(s == m, REMOVED, s)
        vals.append(m)
    return vals


def _peer_scores_kernel(q_ref, k1_ref, k2_ref, s1_ref, s2_ref, e1_ref, e2_ref, tau_ref):
    taus = []
    for h in range(PEER_HEADS):
        q1 = q_ref[:, h * 2 * PEER_HALF:h * 2 * PEER_HALF + PEER_HALF].astype(BF16)
        q2 = q_ref[:, h * 2 * PEER_HALF + PEER_HALF:(h + 1) * 2 * PEER_HALF].astype(BF16)
        s1 = _dot_nt(k1_ref[h], q1)
        s2 = _dot_nt(k2_ref[h], q2)
        v1 = _top_values(s1, PEER_TOPK)
        v2 = _top_values(s2, PEER_TOPK)
        v2_all = jnp.concatenate(v2, axis=0)
        cand = jnp.concatenate([v1[i] + v2_all for i in range(PEER_TOPK)], axis=0)
        top = _top_values(cand, PEER_TOPK)
        z = sum(jnp.exp(t - top[0]) for t in top)
        s1_ref[h] = s1
        s2_ref[h] = s2
        e1_ref[h] = jnp.exp(s1 - v1[0])
        e2_ref[h] = jnp.exp(s2 - v2[0]) / z
        taus.append(top[PEER_TOPK - 1])
    tau_ref[...] = jnp.concatenate(taus, axis=0)


def peer_scores(qp, keys1, keys2, tn):
    n = qp.shape[0]
    big = jax.ShapeDtypeStruct((PEER_HEADS, PEER_NKEYS, n), F32)
    big_spec = pl.BlockSpec((PEER_HEADS, PEER_NKEYS, tn), lambda i: (0, 0, i))
    key_spec = pl.BlockSpec((PEER_HEADS, PEER_NKEYS, PEER_HALF), lambda i: (0, 0, 0))
    return pl.pallas_call(
        _peer_scores_kernel,
        out_shape=(big, big, big, big, jax.ShapeDtypeStruct((PEER_HEADS, n), F32)),
        grid=(n // tn,),
        in_specs=[pl.BlockSpec((tn, qp.shape[1]), lambda i: (i, 0)), key_spec, key_spec],
        out_specs=(big_spec, big_spec, big_spec, big_spec, pl.BlockSpec((PEER_HEADS, tn), lambda i: (0, i))),
        compiler_params=_params(("parallel",)),
        name="peer_scores",
    )(qp, keys1, keys2)


PEER_I1_PER_TILE = 8


def _peer_dense_kernel(x_ref, g2_ref, gf_ref, u_ref, v_ref, s1_ref, e1_ref, s2_ref, e2_ref, tau_ref, o_ref,
                       h_ref, acc_ref):
    e = pl.program_id(1)

    @pl.when(e == 0)
    def _():
        h_ref[...] = _rms(x_ref[...], g2_ref[...]).astype(BF16)
        acc_ref[...] = jnp.zeros_like(acc_ref)

    pre = _dot_nt(u_ref[...], h_ref[...])
    n_tok = pre.shape[1]
    tile = min(LANES, n_tok)
    parts = []
    for c in range(PEER_I1_PER_TILE):
        cols = []
        for t0 in range(0, n_tok, tile):
            tok = slice(t0, t0 + tile)
            w = jnp.zeros((PEER_NKEYS, tile), F32)
            for h in range(PEER_HEADS):
                keep = s1_ref[h, c:c + 1, tok] + s2_ref[h, :, tok] >= tau_ref[h:h + 1, tok]
                w = w + jnp.where(keep, e1_ref[h, c:c + 1, tok] * e2_ref[h, :, tok], 0.0)
            cols.append(w * _gelu_tanh(pre[c * PEER_NKEYS:(c + 1) * PEER_NKEYS, tok]))
        parts.append(jnp.concatenate(cols, axis=1).astype(BF16))
    acc_ref[...] += _dot_tn(jnp.concatenate(parts, axis=0), v_ref[...])

    @pl.when(e == pl.num_programs(1) - 1)
    def _():
        o_ref[...] = _rms(x_ref[...] + acc_ref[...], gf_ref[...])


def peer_dense(x, g2, gf, u_bf, v_bf, s1, s2, e1, e2, tau, tn):
    n = x.shape[0]
    te = PEER_I1_PER_TILE * PEER_NKEYS
    n_exp = u_bf.shape[0]
    sub = pl.BlockSpec((PEER_HEADS, PEER_I1_PER_TILE, tn), lambda i, e: (0, e, i))
    full = pl.BlockSpec((PEER_HEADS, PEER_NKEYS, tn), lambda i, e: (0, 0, i))
    vec = pl.BlockSpec((1, D_MODEL), lambda i, e: (0, 0))
    return pl.pallas_call(
        _peer_dense_kernel,
        out_shape=jax.ShapeDtypeStruct((n, D_MODEL), F32),
        grid=(n // tn, n_exp // te),
        in_specs=[pl.BlockSpec((tn, D_MODEL), lambda i, e: (i, 0)), vec, vec,
                  pl.BlockSpec((te, D_MODEL), lambda i, e: (e, 0)),
                  pl.BlockSpec((te, D_MODEL), lambda i, e: (e, 0)),
                  sub, sub, full, full,
                  pl.BlockSpec((PEER_HEADS, tn), lambda i, e: (0, i))],
        out_specs=pl.BlockSpec((tn, D_MODEL), lambda i, e: (i, 0)),
        scratch_shapes=[pltpu.VMEM((tn, D_MODEL), BF16), pltpu.VMEM((tn, D_MODEL), F32)],
        compiler_params=_params(("parallel", "arbitrary")),
        name="peer_dense",
    )(x, g2.reshape(1, D_MODEL), gf.reshape(1, D_MODEL), u_bf, v_bf, s1, e1, s2, e2, tau)


def _pack_w_in(w_in):
    offs = np.concatenate([[0], np.cumsum(IN_SPLITS)])
    seg = lambda i: w_in[:, offs[i]:offs[i + 1]]
    g_nsa, glr = seg(7), seg(12)
    per_group = 3 * NSA_HPG
    zeros = lambda n: jnp.zeros((w_in.shape[0], n), w_in.dtype)
    tail = []
    for g in range(NSA_GROUPS):
        tail.append(g_nsa[:, g * per_group:(g + 1) * per_group])
        if g == 0:
            tail += [zeros(GLR_LANE - per_group), glr, zeros(LANES - GLR_LANE - GLA_RANK)]
        else:
            tail.append(zeros(LANES - per_group))
    packed = jnp.concatenate([seg(i) for i in (0, 1, 2, 3, 4, 5, 6, 8, 9, 10, 11, 13, 14)] + tail, axis=1)
    assert packed.shape[1] == Z_COLS
    return packed.astype(BF16)


def _pack_cmp(pe, w1, w2):
    w1cat = jnp.concatenate([w1[:CMP_STRIDE].reshape(CMP_STRIDE * NSA_DK, CMP_HIDDEN),
                             w1[CMP_STRIDE:].reshape(CMP_STRIDE * NSA_DK, CMP_HIDDEN)], axis=1).astype(BF16)
    return (w1cat, pe.reshape(1, CMP_LEN * NSA_DK).astype(BF16),
            w1.reshape(CMP_LEN * NSA_DK, CMP_HIDDEN).astype(BF16), w2.astype(BF16))


def _row_tile(n, cap):
    t = min(n, cap)
    while n % t:
        t //= 2
    return t


def _channel_tail(x2d, z, o_nsa, o_gla, wts):
    n = x2d.shape[0]
    tm = _row_tile(n, 512)
    y = merge(o_nsa, o_gla, wts["w_nsa"], wts["w_gla"], z, tm, 512)
    x1 = out_proj(y, wts["w_out"], x2d, tm, 512)
    qp = norm_matmul(x1, wts["norm2_g"], wts["w_q"], tm, 512)
    tn = _row_tile(n, 256)
    s1, s2, e1, e2, tau = peer_scores(qp, wts["keys1"], wts["keys2"], tn)
    return peer_dense(x1, wts["norm2_g"], wts["norm_f_g"], wts["u"], wts["v"], s1, s2, e1, e2, tau, _row_tile(n, 512))


def kernel(x_prompt, x_sample, cache_k_cmp, cache_v_cmp, cache_k_slc, cache_v_slc, cache_k_win, cache_v_win, state_gla, page_table, norm1_g, w_in, cmp_pe_k, cmp_w1_k, cmp_w2_k, cmp_pe_v, cmp_w1_v, cmp_w2_v, gla_w_lr2, gla_b_lr, gla_norm_g, w_nsa_proj, w_gla_proj, w_out, norm2_g, peer_w_q, peer_keys1, peer_keys2, peer_u, peer_v, norm_f_g):
    b_sz, t_len, _ = x_prompt.shape
    bd, s_len, _ = x_sample.shape
    n_pool = cache_k_cmp.shape[0]
    wb = cache_k_win.shape[1]

    w_pack = _pack_w_in(w_in)
    cmp_k = _pack_cmp(cmp_pe_k, cmp_w1_k, cmp_w2_k)
    cmp_v = _pack_cmp(cmp_pe_v, cmp_w1_v, cmp_w2_v)
    cmp_kv = [jnp.stack([a, b]) for a, b in zip(cmp_k, cmp_v)]
    wlr = jnp.zeros((LANES, GLA_HEADS * GLA_DK), F32).at[GLR_LANE:GLR_LANE + GLA_RANK].set(gla_w_lr2)
    wlr = wlr.reshape(LANES, GLA_HEADS, GLA_DK).transpose(1, 0, 2).astype(BF16)
    blr = gla_b_lr.reshape(GLA_HEADS, 1, GLA_DK)
    wts = dict(w_nsa=w_nsa_proj.astype(BF16), w_gla=w_gla_proj.astype(BF16), w_out=w_out.astype(BF16),
               w_q=peer_w_q.astype(BF16), norm2_g=norm2_g, norm_f_g=norm_f_g,
               keys1=peer_keys1.astype(BF16), keys2=peer_keys2.astype(BF16),
               u=peer_u.astype(BF16), v=peer_v.astype(BF16))

    def heads_out(z, col0, lead):
        return z[:, col0:col0 + GD].reshape(lead + (NSA_GROUPS, NSA_DK))

    n_p = b_sz * t_len
    xp = x_prompt.reshape(n_p, D_MODEL)
    z_p = norm_matmul(xp, norm1_g, w_pack, _row_tile(n_p, 1024), 512)
    fs_p = half_proj_dense(z_p, cmp_kv[0], b_sz, t_len)
    kcvc_p = compress_finish(fs_p, cmp_kv[1], cmp_kv[2], cmp_kv[3], 1)
    o_nsa_p = nsa_prompt(z_p, kcvc_p, b_sz, t_len)
    s0 = jnp.zeros((b_sz, GLA_HEADS, GLA_DK, GLA_DV), F32)
    o_gla_p, gla_state_p = gla(z_p, wlr, blr, gla_norm_g, s0, b_sz, t_len, GLA_CHUNK, GLA_CHUNK)
    y_prompt = _channel_tail(xp, z_p, o_nsa_p, o_gla_p, wts).reshape(b_sz, t_len, D_MODEL)
    lead_p = (b_sz, t_len)
    kcr_p, vcr_p, ksr_p, vsr_p, kwr_p, vwr_p = (heads_out(z_p, c, lead_p) for c in (C_KC, C_VC, C_KS, C_VS, C_KW, C_VW))
    wl = min(WINDOW, t_len)
    k_win_p = kwr_p[:, t_len - wl:]
    v_win_p = vwr_p[:, t_len - wl:]

    n_s = bd * s_len
    xs = x_sample.reshape(n_s, D_MODEL)
    z_s = norm_matmul(xs, norm1_g, w_pack, _row_tile(n_s, 1024), 512)
    lead_s = (bd, s_len)
    kcr_s, vcr_s, ksr_s, vsr_s, kwr_s, vwr_s = (heads_out(z_s, c, lead_s) for c in (C_KC, C_VC, C_KS, C_VS, C_KW, C_VW))

    pool2d = lambda p: p.reshape(n_pool, PAGE_SIZE * NSA_GROUPS, NSA_DK)
    pages = 16

    def compress_pool(pool, prm):
        fs = half_proj_paged(pool2d(pool), page_table, prm[0], pages)
        return compress_finish(fs[None, :, None], prm[1][None], prm[2][None], prm[3][None], NSA_GROUPS)[0, :, 0]

    kc_s = compress_pool(cache_k_cmp, cmp_k)
    vc_s = compress_pool(cache_v_cmp, cmp_v)

    kw_all = jnp.concatenate([cache_k_win, kwr_s], axis=1)
    vw_all = jnp.concatenate([cache_v_win, vwr_s], axis=1)
    k_win_s = kw_all[:, s_len:s_len + wb]
    v_win_s = vw_all[:, s_len:s_len + wb]
    win_rows = wb + S_PAD

    def win_rows_2d(a):
        a = jnp.pad(a, ((0, 0), (0, win_rows - a.shape[1]), (0, 0), (0, 0)))
        return a.reshape(bd, win_rows * NSA_GROUPS, NSA_DK)

    def new_rows(a):
        return jnp.pad(a.transpose(0, 2, 1, 3), ((0, 0), (0, 0), (0, S_PAD - s_len), (0, 0)))

    rows = NSA_HPG * s_len
    q_s = z_s[:, C_Q:C_Q + NSA_HEADS * NSA_DK].reshape(bd, s_len, NSA_GROUPS, NSA_HPG, NSA_DK)
    q_s = q_s.transpose(0, 2, 3, 1, 4).reshape(bd, NSA_GROUPS, rows, NSA_DK)
    gate_s = z_s[:, C_TAIL:C_TAIL + 512].reshape(bd, s_len, NSA_GROUPS, LANES)[..., :3 * NSA_HPG]
    gate_s = gate_s.reshape(bd, s_len, NSA_GROUPS, NSA_HPG, 3).transpose(0, 2, 3, 1, 4).reshape(bd, NSA_GROUPS, rows, 3)
    gate_s = jnp.pad(gate_s, ((0, 0), (0, 0), (0, 0), (0, LANES - 3)))
    o_s = nsa_sample(q_s, kc_s, vc_s, new_rows(ksr_s), new_rows(vsr_s), win_rows_2d(kw_all), win_rows_2d(vw_all),
                     gate_s, pool2d(cache_k_slc), pool2d(cache_v_slc), page_table, s_len, pages)
    o_nsa_s = o_s.reshape(bd, NSA_GROUPS, NSA_HPG, s_len, NSA_DK).transpose(0, 3, 1, 2, 4)
    o_nsa_s = o_nsa_s.reshape(n_s, NSA_HEADS * NSA_DK).astype(BF16)

    z_s_pad = jnp.pad(z_s.reshape(bd, s_len, Z_COLS), ((0, 0), (0, S_PAD - s_len), (0, 0))).reshape(bd * S_PAD, Z_COLS)
    o_gla_s, gla_state_s = gla(z_s_pad, wlr, blr, gla_norm_g, state_gla, bd, S_PAD, S_PAD, s_len)
    o_gla_s = o_gla_s.reshape(bd, S_PAD, GLA_HEADS * GLA_DV)[:, :s_len].reshape(n_s, GLA_HEADS * GLA_DV)
    y_sample = _channel_tail(xs, z_s, o_nsa_s, o_gla_s, wts).reshape(bd, s_len, D_MODEL)

    return (y_prompt, y_sample, kcr_p, vcr_p, ksr_p, vsr_p, k_win_p, v_win_p, gla_state_p,
            kcr_s, vcr_s, ksr_s, vsr_s, k_win_s, v_win_s, gla_state_s)
```

```python
import functools

import numpy as np
import jax
import jax.numpy as jnp
from jax import lax
from jax.experimental import pallas as pl
from jax.experimental.pallas import tpu as pltpu

F32 = jnp.float32
BF16 = jnp.bfloat16

D_MODEL = 2048
PAGE_SIZE = 128
NSA_HEADS = 16
NSA_GROUPS = 4
NSA_HPG = NSA_HEADS // NSA_GROUPS
NSA_DK = 128
NSA_SCALE = NSA_DK ** -0.5
CMP_LEN = 32
CMP_STRIDE = 16
CMP_HIDDEN = 256
SEL_BLOCK = 64
N_SEL = 16
WINDOW = 512
QUERY_BLOCK = 256
GLA_HEADS = 4
GLA_DK = 256
GLA_DV = 512
GLA_RANK = 16
GLA_TAU = 16.0
GLA_CHUNK = 64
PEER_HEADS = 8
PEER_NKEYS = 128
PEER_HALF = 128
PEER_TOPK = 16
NORM_EPS = 1e-6
NEG = -1e30
BIG = 1e30
PAD_SCORE = -3e38
REMOVED = -float("inf")

IN_SPLITS = (NSA_HEADS * NSA_DK,) + (NSA_GROUPS * NSA_DK,) * 6 + (
    3 * NSA_HEADS, GLA_HEADS * GLA_DK, GLA_HEADS * GLA_DK, GLA_HEADS * GLA_DV,
    GLA_HEADS * GLA_DV, GLA_RANK, D_MODEL, D_MODEL)

GD = NSA_GROUPS * NSA_DK
C_Q = 0
C_KC = C_Q + NSA_HEADS * NSA_DK
C_VC = C_KC + GD
C_KS = C_VC + GD
C_VS = C_KS + GD
C_KW = C_VS + GD
C_VW = C_KW + GD
C_GQ = C_VW + GD
C_GK = C_GQ + GLA_HEADS * GLA_DK
C_GV = C_GK + GLA_HEADS * GLA_DK
C_GR = C_GV + GLA_HEADS * GLA_DV
C_MA = C_GR + GLA_HEADS * GLA_DV
C_MB = C_MA + D_MODEL
C_TAIL = C_MB + D_MODEL
Z_COLS = C_TAIL + 512
GLR_LANE = 16
LANES = 128

VMEM_LIMIT = 56 * 1024 * 1024


def _params(sem):
    return pltpu.CompilerParams(dimension_semantics=sem, vmem_limit_bytes=VMEM_LIMIT)


def _dot(a, b):
    return jnp.dot(a, b, preferred_element_type=F32)


def _dot_nt(a, b):
    return lax.dot_general(a, b, (((1,), (1,)), ((), ())), preferred_element_type=F32)


def _dot_tn(a, b):
    return lax.dot_general(a, b, (((0,), (0,)), ((), ())), preferred_element_type=F32)


def _rms(x, g):
    return x * lax.rsqrt(jnp.mean(x * x, axis=-1, keepdims=True) + NORM_EPS) * g


GELU_C0 = 2.0 * (2.0 / np.pi) ** 0.5
GELU_C1 = GELU_C0 * 0.044715


def _gelu_tanh(x):
    return x / (1.0 + jnp.exp(-(x * (GELU_C0 + GELU_C1 * (x * x)))))


def _bf16_pieces(x):
    hi = x.astype(BF16)
    rest = x - hi.astype(F32)
    mid = rest.astype(BF16)
    lo = (rest - mid.astype(F32)).astype(BF16)
    return hi, mid, lo


def _dot_nt_split(small_ints_bf16, x):
    return sum(_dot_nt(small_ints_bf16, piece) for piece in _bf16_pieces(x))


def _dot_split(small_ints_bf16, x):
    return sum(_dot(small_ints_bf16, piece) for piece in _bf16_pieces(x))


def _norm_matmul_kernel(x_ref, g_ref, w_ref, o_ref, h_ref):
    @pl.when(pl.program_id(1) == 0)
    def _():
        h_ref[...] = _rms(x_ref[...], g_ref[...]).astype(BF16)

    o_ref[...] = _dot(h_ref[...], w_ref[...])


def norm_matmul(x, g, w, tm, tn):
    n, d = x.shape
    cols = w.shape[1]
    return pl.pallas_call(
        _norm_matmul_kernel,
        out_shape=jax.ShapeDtypeStruct((n, cols), F32),
        grid=(n // tm, cols // tn),
        in_specs=[pl.BlockSpec((tm, d), lambda i, j: (i, 0)),
                  pl.BlockSpec((1, d), lambda i, j: (0, 0)),
                  pl.BlockSpec((d, tn), lambda i, j: (0, j))],
        out_specs=pl.BlockSpec((tm, tn), lambda i, j: (i, j)),
        scratch_shapes=[pltpu.VMEM((tm, d), BF16)],
        compiler_params=_params(("parallel", "arbitrary")),
        name="norm_matmul",
    )(x, g.reshape(1, d), w)


KV_SEGMENTS = 6


def _in_proj_kernel(x_ref, g_ref, w_ref, o_ref, *rest):
    head_refs, h_ref = rest[:KV_SEGMENTS], rest[KV_SEGMENTS]
    j = pl.program_id(1)

    @pl.when(j == 0)
    def _():
        h_ref[...] = _rms(x_ref[...], g_ref[...]).astype(BF16)

    z = _dot(h_ref[...], w_ref[...])
    o_ref[...] = z
    for k in range(KV_SEGMENTS):
        @pl.when(j == C_KC // GD + k)
        def _(k=k):
            for g in range(NSA_GROUPS):
                head_refs[k][:, g, :] = z[:, g * NSA_DK:(g + 1) * NSA_DK]


def in_proj(x, g, w, tm):
    n, d = x.shape
    heads = jax.ShapeDtypeStruct((n, NSA_GROUPS, NSA_DK), F32)
    head_spec = pl.BlockSpec((tm, NSA_GROUPS, NSA_DK), lambda i, j: (i, 0, 0))
    return pl.pallas_call(
        _in_proj_kernel,
        out_shape=(jax.ShapeDtypeStruct((n, Z_COLS), F32),) + (heads,) * KV_SEGMENTS,
        grid=(n // tm, Z_COLS // GD),
        in_specs=[pl.BlockSpec((tm, d), lambda i, j: (i, 0)),
                  pl.BlockSpec((1, d), lambda i, j: (0, 0)),
                  pl.BlockSpec((d, GD), lambda i, j: (0, j))],
        out_specs=(pl.BlockSpec((tm, GD), lambda i, j: (i, j)),) + (head_spec,) * KV_SEGMENTS,
        scratch_shapes=[pltpu.VMEM((tm, d), BF16)],
        compiler_params=_params(("parallel", "arbitrary")),
        name="in_proj",
    )(x, g.reshape(1, d), w)


def _half_rows(ref, first_row, n_half, row_stride, lane0):
    parts = [ref[pl.ds(first_row + r * row_stride, n_half, stride=CMP_STRIDE * row_stride), pl.ds(lane0, NSA_DK)]
             for r in range(CMP_STRIDE)]
    return jnp.concatenate(parts, axis=1)


def _half_proj_dense_kernel(x_ref, w_ref, o_ref, *, n_half):
    o_ref[...] = _dot(_half_rows(x_ref, 0, n_half, 1, 0).astype(BF16), w_ref[...])


def half_proj_dense(z, w1cat, b_sz, t_len):
    n_half = t_len // CMP_STRIDE
    return pl.pallas_call(
        functools.partial(_half_proj_dense_kernel, n_half=n_half),
        out_shape=jax.ShapeDtypeStruct((2, b_sz, NSA_GROUPS, n_half, 2 * CMP_HIDDEN), F32),
        grid=(2, b_sz, NSA_GROUPS),
        in_specs=[pl.BlockSpec((t_len, NSA_DK), lambda kv, b, g: (b, C_KC // NSA_DK + kv * NSA_GROUPS + g)),
                  pl.BlockSpec((None, CMP_STRIDE * NSA_DK, 2 * CMP_HIDDEN), lambda kv, b, g: (kv, 0, 0))],
        out_specs=pl.BlockSpec((None, None, None, n_half, 2 * CMP_HIDDEN), lambda kv, b, g: (kv, b, g, 0, 0)),
        compiler_params=_params(("parallel", "parallel", "parallel")),
        name="half_proj_dense",
    )(z, w1cat)


HALVES_PER_PAGE = PAGE_SIZE // CMP_STRIDE


def _half_proj_paged_kernel(pt_ref, *refs, pages):
    page_refs, w_ref, o_ref = refs[:pages], refs[pages], refs[pages + 1]
    tile = 2 * NSA_GROUPS
    rows_per_half = CMP_STRIDE * NSA_GROUPS
    low = lax.broadcasted_iota(jnp.int32, (tile, NSA_DK), 0) < NSA_GROUPS
    acc = None
    for r2 in range(CMP_STRIDE // 2):
        even, odd = [], []
        for p in range(pages):
            for n in range(0, HALVES_PER_PAGE, 2):
                a = page_refs[p][pl.ds(n * rows_per_half + tile * r2, tile), :]
                b = page_refs[p][pl.ds((n + 1) * rows_per_half + tile * r2, tile), :]
                even.append(jnp.where(low, a, pltpu.roll(b, NSA_GROUPS, 0)))
                odd.append(jnp.where(low, pltpu.roll(a, NSA_GROUPS, 0), b))
        lhs = jnp.concatenate([jnp.concatenate(even, axis=0), jnp.concatenate(odd, axis=0)], axis=1).astype(BF16)
        part = _dot(lhs, w_ref[pl.ds(r2 * 2 * NSA_DK, 2 * NSA_DK), :])
        acc = part if acc is None else acc + part
    o_ref[...] = acc


def half_proj_paged(pool2d, page_table, w1cat_one, pages):
    bd, n_pages = page_table.shape
    n_half = n_pages * HALVES_PER_PAGE
    rows = pages * HALVES_PER_PAGE * NSA_GROUPS

    def page_spec(i):
        return pl.BlockSpec((None, PAGE_SIZE * NSA_GROUPS, NSA_DK),
                            lambda b, j, pt: (pt[b, j * pages + i], 0, 0))

    grid_spec = pltpu.PrefetchScalarGridSpec(
        num_scalar_prefetch=1,
        grid=(bd, n_pages // pages),
        in_specs=[page_spec(i) for i in range(pages)]
        + [pl.BlockSpec((CMP_STRIDE * NSA_DK, 2 * CMP_HIDDEN), lambda b, j, pt: (0, 0))],
        out_specs=pl.BlockSpec((None, rows, 2 * CMP_HIDDEN), lambda b, j, pt: (b, j, 0)),
    )
    return pl.pallas_call(
        functools.partial(_half_proj_paged_kernel, pages=pages),
        out_shape=jax.ShapeDtypeStruct((bd, n_half * NSA_GROUPS, 2 * CMP_HIDDEN), F32),
        grid_spec=grid_spec,
        compiler_params=_params(("parallel", "arbitrary")),
        name="half_proj_paged",
    )(page_table, *([pool2d] * pages), w1cat_one)


def _compress_finish_kernel(fs_ref, pe_ref, w1_ref, w2_ref, o_ref, *, n_rows, step):
    pe = jnp.broadcast_to(pe_ref[...], (8, CMP_LEN * NSA_DK))
    c = _dot(pe, w1_ref[...])[0:1]
    first = fs_ref[:, 0:CMP_HIDDEN]
    second_next = pltpu.roll(fs_ref[:, CMP_HIDDEN:2 * CMP_HIDDEN], n_rows - step, 0)
    hid = _gelu_tanh(first + second_next + c)
    out = _dot(hid.astype(BF16), w2_ref[...])
    row = lax.broadcasted_iota(jnp.int32, (n_rows, 1), 0)
    o_ref[...] = jnp.where(row < n_rows - step, out, 0.0)


def compress_finish(fs, pe, w1, w2, step):
    kv, b_sz, n_sets, n_rows, _ = fs.shape
    return pl.pallas_call(
        functools.partial(_compress_finish_kernel, n_rows=n_rows, step=step),
        out_shape=jax.ShapeDtypeStruct((kv, b_sz, n_sets, n_rows, NSA_DK), F32),
        grid=(kv, b_sz, n_sets),
        in_specs=[pl.BlockSpec((None, None, None, n_rows, 2 * CMP_HIDDEN), lambda k, b, g: (k, b, g, 0, 0)),
                  pl.BlockSpec((None, 1, CMP_LEN * NSA_DK), lambda k, b, g: (k, 0, 0)),
                  pl.BlockSpec((None, CMP_LEN * NSA_DK, CMP_HIDDEN), lambda k, b, g: (k, 0, 0)),
                  pl.BlockSpec((None, CMP_HIDDEN, NSA_DK), lambda k, b, g: (k, 0, 0))],
        out_specs=pl.BlockSpec((None, None, None, n_rows, NSA_DK), lambda k, b, g: (k, b, g, 0, 0)),
        compiler_params=_params(("parallel", "parallel", "parallel")),
        name="compress_finish",
    )(fs, pe, w1, w2)


def _cmp_probs(s, pos, n_cmp):
    c = lax.broadcasted_iota(jnp.int32, (1, s.shape[1]), 1)
    mask = (c * CMP_STRIDE + (CMP_LEN - 1) <= pos) & (c < n_cmp)
    s = jnp.where(mask, s, NEG)
    m = jnp.max(s, axis=-1, keepdims=True)
    e = jnp.where(mask, jnp.exp(s - m), 0.0)
    l = jnp.sum(e, axis=-1, keepdims=True)
    return e / jnp.where(l > 0.0, l, 1.0)


def _block_scores_t(pg, n_rows):
    ncp = pg.shape[1]
    blk = lax.broadcasted_iota(jnp.int32, (n_rows, ncp), 0)
    c = lax.broadcasted_iota(jnp.int32, (n_rows, ncp), 1)
    hpb = SEL_BLOCK // CMP_STRIDE
    spread = (c // hpb == blk).astype(F32) + ((c + 1) // hpb == blk).astype(F32)
    return _dot_nt_split(spread.astype(BF16), pg)


def _select_blocks_t(ps_t, pos_row, n_blk):
    blk = lax.broadcasted_iota(jnp.int32, ps_t.shape, 0)
    cur = pos_row // SEL_BLOCK
    forced = (blk == 0) | (blk == cur) | (blk == cur - 1)
    score = jnp.where(forced, BIG, jnp.where(blk <= cur, ps_t, NEG))
    score = jnp.where(blk < n_blk, score, PAD_SCORE)
    blk_f = blk.astype(F32)
    sel = jnp.zeros(ps_t.shape, F32)
    for _ in range(N_SEL):
        m = jnp.max(score, axis=0, keepdims=True)
        first = jnp.min(jnp.where(score == m, blk_f, float(ps_t.shape[0])), axis=0, keepdims=True)
        hit = blk_f == first
        sel = jnp.where(hit, 1.0, sel)
        score = jnp.where(hit, REMOVED, score)
    return sel


def _lane_groups(x, op):
    parts = [x[:, i * LANES:(i + 1) * LANES] for i in range(x.shape[1] // LANES)]
    while len(parts) > 1:
        parts = [op(parts[i], parts[i + 1]) for i in range(0, len(parts) - 1, 2)] + (
            [parts[-1]] if len(parts) % 2 else [])
    return parts[0]


def _block_expand(n_blocks, n_keys):
    blk = lax.broadcasted_iota(jnp.int32, (n_blocks, n_keys), 0)
    key = lax.broadcasted_iota(jnp.int32, (n_blocks, n_keys), 1)
    return (key // SEL_BLOCK == blk).astype(BF16)


def _online_update(state, s, mask, v_bf):
    m, l, acc = state
    s = jnp.where(mask, s, NEG)
    m_new = jnp.maximum(m, jnp.max(s, axis=-1, keepdims=True))
    alpha = jnp.exp(m - m_new)
    p = jnp.where(mask, jnp.exp(s - m_new), 0.0)
    l = alpha * l + jnp.sum(p, axis=-1, keepdims=True)
    acc = alpha * acc + _dot(p.astype(BF16), v_bf)
    return m_new, l, acc


def _masked_attend(q_bf, k_bf, v_bf, mask):
    s = jnp.where(mask, _dot_nt(q_bf, k_bf), NEG)
    m = jnp.max(s, axis=-1, keepdims=True)
    e = jnp.where(mask, jnp.exp(s - m), 0.0)
    l = jnp.sum(e, axis=-1, keepdims=True)
    return _dot((e / l).astype(BF16), v_bf)


SEL_KEYS = 512
BLK_LANES = 128


def _nsa_prompt_kernel(q_ref, kc_ref, vc_ref, ks_ref, vs_ref, kw_ref, vw_ref, gate_ref, o_ref, sel_ref, s_ref,
                       kv_ref, *, t_len):
    qb = QUERY_BLOCK
    start = pl.program_id(2) * qb

    @pl.when(pl.program_id(2) == 0)
    def _():
        for i, ref in enumerate((ks_ref, vs_ref, kw_ref, vw_ref)):
            kv_ref[i] = ref[...].astype(BF16)

    pos = start + lax.broadcasted_iota(jnp.int32, (qb, 1), 0)
    pos_row = start + lax.broadcasted_iota(jnp.int32, (1, qb), 1)
    n_cmp = t_len // CMP_STRIDE - 1
    n_blk = max(-(-t_len // SEL_BLOCK), N_SEL)
    q = q_ref[...] * NSA_SCALE
    qh = [q[:, h * NSA_DK:(h + 1) * NSA_DK].astype(BF16) for h in range(NSA_HPG)]

    kc = kc_ref[...].astype(BF16)
    vc = vc_ref[...].astype(BF16)
    o_cmp = []
    pg = jnp.zeros((qb, kc.shape[0]), F32)
    for h in range(NSA_HPG):
        p = _cmp_probs(_dot_nt(qh[h], kc), pos, n_cmp)
        o_cmp.append(_dot(p.astype(BF16), vc))
        pg = pg + p
    sel_ref[...] = _select_blocks_t(_block_scores_t(pg, BLK_LANES), pos_row, n_blk)

    w_rows = WINDOW + qb
    base = pl.multiple_of(jnp.maximum(start - WINDOW, 0), qb)
    kw = kv_ref[2, pl.ds(base, w_rows), :]
    vw = kv_ref[3, pl.ds(base, w_rows), :]
    dist = pos - (base + lax.broadcasted_iota(jnp.int32, (1, w_rows), 1))
    w_mask = (dist >= 0) & (dist <= WINDOW)
    o_win = [_masked_attend(qh[h], kw, vw, w_mask) for h in range(NSA_HPG)]

    n_steps = (start + qb + SEL_KEYS - 1) // SEL_KEYS
    blocks_per_step = SEL_KEYS // SEL_BLOCK
    expand = _block_expand(blocks_per_step, SEL_KEYS)

    def scores_step(j, m_run):
        k0 = pl.multiple_of(j * SEL_KEYS, SEL_KEYS)
        k_bf = kv_ref[0, pl.ds(k0, SEL_KEYS), :]
        key = k0 + lax.broadcasted_iota(jnp.int32, (1, SEL_KEYS), 1)
        chosen = sel_ref[pl.ds(pl.multiple_of(j * blocks_per_step, blocks_per_step), blocks_per_step), :]
        mask = (_dot_tn(chosen.astype(BF16), expand) > 0.5) & (key <= pos)
        out = []
        for h in range(NSA_HPG):
            s = jnp.where(mask, _dot_nt(qh[h], k_bf), NEG)
            s_ref[j, h] = s
            out.append(jnp.maximum(m_run[h], _lane_groups(s, jnp.maximum)))
        return tuple(out)

    m_run = lax.fori_loop(0, n_steps, scores_step, tuple(jnp.full((qb, LANES), NEG, F32) for _ in range(NSA_HPG)))
    m_fin = [jnp.max(m, axis=-1, keepdims=True) for m in m_run]

    def values_step(j, carry):
        k0 = pl.multiple_of(j * SEL_KEYS, SEL_KEYS)
        v_bf = kv_ref[1, pl.ds(k0, SEL_KEYS), :]
        out = []
        for h in range(NSA_HPG):
            l_run, acc = carry[h]
            p = jnp.exp(s_ref[j, h] - m_fin[h])
            out.append((l_run + _lane_groups(p, jnp.add), acc + _dot(p.astype(BF16), v_bf)))
        return tuple(out)

    sel_state = lax.fori_loop(0, n_steps, values_step,
                              tuple((jnp.zeros((qb, LANES), F32), jnp.zeros((qb, NSA_DK), F32))
                                    for _ in range(NSA_HPG)))

    gates = jax.nn.sigmoid(gate_ref[...])
    outs = []
    for h in range(NSA_HPG):
        l_run, acc = sel_state[h]
        o_sel = acc / jnp.sum(l_run, axis=-1, keepdims=True)
        outs.append(gates[:, 3 * h:3 * h + 1] * o_cmp[h] + gates[:, 3 * h + 1:3 * h + 2] * o_sel
                    + gates[:, 3 * h + 2:3 * h + 3] * o_win[h])
    o_ref[...] = jnp.concatenate(outs, axis=1).astype(o_ref.dtype)


def nsa_prompt(z, kcvc, b_sz, t_len):
    assert t_len % SEL_KEYS == 0 and t_len >= WINDOW + QUERY_BLOCK and t_len // SEL_BLOCK <= BLK_LANES
    nqb = t_len // QUERY_BLOCK
    n_half = t_len // CMP_STRIDE
    hd = NSA_HPG * NSA_DK

    def rows(col0):
        return pl.BlockSpec((t_len, NSA_DK), lambda b, g, i: (b, col0 // NSA_DK + g))

    def cmp_spec(kv):
        return pl.BlockSpec((None, None, None, n_half, NSA_DK), lambda b, g, i: (kv, b, g, 0, 0))

    return pl.pallas_call(
        functools.partial(_nsa_prompt_kernel, t_len=t_len),
        out_shape=jax.ShapeDtypeStruct((b_sz * t_len, NSA_HEADS * NSA_DK), BF16),
        grid=(b_sz, NSA_GROUPS, nqb),
        in_specs=[pl.BlockSpec((QUERY_BLOCK, hd), lambda b, g, i: (b * nqb + i, g)),
                  cmp_spec(0), cmp_spec(1),
                  rows(C_KS), rows(C_VS), rows(C_KW), rows(C_VW),
                  pl.BlockSpec((QUERY_BLOCK, LANES), lambda b, g, i: (b * nqb + i, C_TAIL // LANES + g))],
        out_specs=pl.BlockSpec((QUERY_BLOCK, hd), lambda b, g, i: (b * nqb + i, g)),
        scratch_shapes=[pltpu.VMEM((BLK_LANES, QUERY_BLOCK), F32),
                        pltpu.VMEM((t_len // SEL_KEYS, NSA_HPG, QUERY_BLOCK, SEL_KEYS), F32),
                        pltpu.VMEM((4, t_len, NSA_DK), BF16)],
        compiler_params=_params(("parallel", "parallel", "arbitrary")),
        name="nsa_prompt",
    )(z, kcvc, kcvc, z, z, z, z, z)


S_PAD = 8


def _nsa_sample_kernel(pt_ref, q_ref, kc_ref, vc_ref, kn_ref, vn_ref, kw_ref, vw_ref, kwn_ref, vwn_ref, gate_ref, *refs,
                       pages, past, s_len, n_cmp, ncp, n_blk, blk_rows, win_rows):
    kpages, vpages = refs[:pages], refs[pages:2 * pages]
    o_ref = refs[2 * pages]
    sel_ref, m_ref, l_ref, acc_ref, oc_ref = refs[2 * pages + 1:]
    j = pl.program_id(1)
    n_j = pl.num_programs(1)
    rows = NSA_HPG * s_len
    step = lax.broadcasted_iota(jnp.int32, (rows, 1), 0) % s_len
    pos = past + step
    qg = [(q_ref[g] * NSA_SCALE).astype(BF16) for g in range(NSA_GROUPS)]

    @pl.when(j == 0)
    def _():
        step_i = lax.broadcasted_iota(jnp.int32, (LANES, rows), 1) % s_len
        lane_i = lax.broadcasted_iota(jnp.int32, (LANES, rows), 0)
        step_o = lax.broadcasted_iota(jnp.int32, (rows, LANES), 0) % s_len
        lane_o = lax.broadcasted_iota(jnp.int32, (rows, LANES), 1)
        pos_row = past + lax.broadcasted_iota(jnp.int32, (1, LANES), 1) % S_PAD
        pg = jnp.zeros((LANES, ncp), F32)
        for g in range(NSA_GROUPS):
            kc = kc_ref[pl.ds(g, ncp, stride=NSA_GROUPS), :].astype(BF16)
            vc = vc_ref[pl.ds(g, ncp, stride=NSA_GROUPS), :].astype(BF16)
            p = _cmp_probs(_dot_nt(qg[g], kc), pos, n_cmp)
            oc_ref[g] = _dot(p.astype(BF16), vc)
            fold = (step_i + g * S_PAD == lane_i).astype(BF16)
            pg = pg + _dot_split(fold, p)
            m_ref[g] = jnp.full((rows, 1), NEG, F32)
            l_ref[g] = jnp.zeros((rows, 1), F32)
            acc_ref[g] = jnp.zeros((rows, NSA_DK), F32)
        sel_t = _select_blocks_t(_block_scores_t(pg, blk_rows), pos_row, n_blk).astype(BF16)
        for g in range(NSA_GROUPS):
            unfold = (step_o + g * S_PAD == lane_o).astype(BF16)
            sel_ref[g] = _dot_nt(sel_t, unfold)

    n_keys = pages * PAGE_SIZE
    blocks_per_step = n_keys // SEL_BLOCK
    expand = _block_expand(blocks_per_step, n_keys)
    key = j * n_keys + lax.broadcasted_iota(jnp.int32, (1, n_keys), 1)
    blk0 = pl.multiple_of(j * blocks_per_step, blocks_per_step)
    for g in range(NSA_GROUPS):
        k_bf = jnp.concatenate([kpages[p][pl.ds(g, PAGE_SIZE, stride=NSA_GROUPS), :] for p in range(pages)],
                               axis=0).astype(BF16)
        v_bf = jnp.concatenate([vpages[p][pl.ds(g, PAGE_SIZE, stride=NSA_GROUPS), :] for p in range(pages)],
                               axis=0).astype(BF16)
        chosen = _dot_tn(sel_ref[g, pl.ds(blk0, blocks_per_step), :].astype(BF16), expand) > 0.5
        m, l, acc = _online_update((m_ref[g], l_ref[g], acc_ref[g]), _dot_nt(qg[g], k_bf),
                                   chosen & (key <= pos), v_bf)
        m_ref[g] = m
        l_ref[g] = l
        acc_ref[g] = acc

    @pl.when(j == n_j - 1)
    def _():
        gates = jax.nn.sigmoid(gate_ref[...])
        new_i = lax.broadcasted_iota(jnp.int32, (1, S_PAD), 1)
        new_key = past + new_i
        new_blk = past // SEL_BLOCK
        first_row = (lax.broadcasted_iota(jnp.int32, (8, S_PAD), 0) == 0).astype(BF16)
        dist = pos - (past - win_rows + lax.broadcasted_iota(jnp.int32, (1, win_rows), 1))
        w_mask = (dist >= 0) & (dist <= WINDOW)
        n_mask = (new_key <= pos) & (pos - new_key <= WINDOW) & (new_i < s_len)
        for g in range(NSA_GROUPS):
            chosen_new = _dot_tn(sel_ref[g, new_blk:new_blk + 8, :].astype(BF16), first_row) > 0.5
            mask = chosen_new & (new_key <= pos) & (new_i < s_len)
            _, l, acc = _online_update((m_ref[g], l_ref[g], acc_ref[g]),
                                       _dot_nt(qg[g], kn_ref[g].astype(BF16)), mask, vn_ref[g].astype(BF16))
            o_sel = acc / l
            kw = kw_ref[pl.ds(g, win_rows, stride=NSA_GROUPS), :].astype(BF16)
            vw = vw_ref[pl.ds(g, win_rows, stride=NSA_GROUPS), :].astype(BF16)
            s_old = jnp.where(w_mask, _dot_nt(qg[g], kw), NEG)
            s_new = jnp.where(n_mask, _dot_nt(qg[g], kwn_ref[g].astype(BF16)), NEG)
            m_w = jnp.maximum(jnp.max(s_old, axis=-1, keepdims=True), jnp.max(s_new, axis=-1, keepdims=True))
            e_old = jnp.where(w_mask, jnp.exp(s_old - m_w), 0.0)
            e_new = jnp.where(n_mask, jnp.exp(s_new - m_w), 0.0)
            l_w = jnp.sum(e_old, axis=-1, keepdims=True) + jnp.sum(e_new, axis=-1, keepdims=True)
            o_win = (_dot(e_old.astype(BF16), vw) + _dot(e_new.astype(BF16), vwn_ref[g].astype(BF16))) / l_w
            gt = gates[g]
            o_ref[g] = gt[:, 0:1] * oc_ref[g] + gt[:, 1:2] * o_sel + gt[:, 2:3] * o_win


def nsa_sample(q, kc, vc, k_new, v_new, kw_old, vw_old, kw_new, vw_new, gate, pool_k, pool_v, page_table, s_len,
               pages):
    bd, n_pages = page_table.shape
    past = n_pages * PAGE_SIZE
    rows = NSA_HPG * s_len
    n_cmp = past // CMP_STRIDE - 1
    n_blk = max(-(-(past + s_len) // SEL_BLOCK), N_SEL)
    assert s_len <= S_PAD and s_len < CMP_STRIDE and n_pages % pages == 0
    assert (past // SEL_BLOCK) % 8 == 0 and (pages * PAGE_SIZE // SEL_BLOCK) % 8 == 0
    blk_rows = -(-(past // SEL_BLOCK + 8) // LANES) * LANES
    ncp = kc.shape[1] // NSA_GROUPS
    win_rows = kw_old.shape[1] // NSA_GROUPS
    new_spec = pl.BlockSpec((None, NSA_GROUPS, S_PAD, NSA_DK), lambda b, j, pt: (b, 0, 0, 0))

    def per_b(shape):
        return pl.BlockSpec((None,) + shape, lambda b, j, pt: (b,) + (0,) * len(shape))

    def page_spec(i):
        return pl.BlockSpec((None, PAGE_SIZE * NSA_GROUPS, NSA_DK), lambda b, j, pt: (pt[b, j * pages + i], 0, 0))

    grid_spec = pltpu.PrefetchScalarGridSpec(
        num_scalar_prefetch=1,
        grid=(bd, n_pages // pages),
        in_specs=[per_b((NSA_GROUPS, rows, NSA_DK)), per_b((ncp * NSA_GROUPS, NSA_DK)), per_b((ncp * NSA_GROUPS, NSA_DK)),
                  new_spec, new_spec,
                  per_b((win_rows * NSA_GROUPS, NSA_DK)), per_b((win_rows * NSA_GROUPS, NSA_DK)),
                  new_spec, new_spec,
                  per_b((NSA_GROUPS, rows, LANES))]
        + [page_spec(i) for i in range(pages)] * 2,
        out_specs=per_b((NSA_GROUPS, rows, NSA_DK)),
        scratch_shapes=[pltpu.VMEM((NSA_GROUPS, blk_rows, rows), F32),
                        pltpu.VMEM((NSA_GROUPS, rows, 1), F32),
                        pltpu.VMEM((NSA_GROUPS, rows, 1), F32),
                        pltpu.VMEM((NSA_GROUPS, rows, NSA_DK), F32),
                        pltpu.VMEM((NSA_GROUPS, rows, NSA_DK), F32)],
    )
    return pl.pallas_call(
        functools.partial(_nsa_sample_kernel, pages=pages, past=past, s_len=s_len, n_cmp=n_cmp, ncp=ncp, n_blk=n_blk,
                          blk_rows=blk_rows, win_rows=win_rows),
        out_shape=jax.ShapeDtypeStruct((bd, NSA_GROUPS, rows, NSA_DK), F32),
        grid_spec=grid_spec,
        compiler_params=_params(("parallel", "arbitrary")),
        name="nsa_sample",
    )(page_table, q, kc, vc, k_new, v_new, kw_old, vw_old, kw_new, vw_new, gate,
      *([pool_k] * pages), *([pool_v] * pages))


def _gla_kernel(q_ref, k_ref, v_ref, gr_ref, lr_ref, wlr_ref, blr_ref, ng_ref, s0_ref, o_ref, sT_out_ref,
                st_ref, *, chunk, t_valid):
    c = pl.program_id(2)

    @pl.when(c == 0)
    def _():
        st_ref[...] = s0_ref[...].T

    row = lax.broadcasted_iota(jnp.int32, (chunk, 1), 0)
    x = _dot(lr_ref[...].astype(BF16), wlr_ref[...]) + blr_ref[...]
    log_a = jnp.where(row < t_valid, jax.nn.log_sigmoid(x) / GLA_TAU, 0.0)
    tri = (lax.broadcasted_iota(jnp.int32, (chunk, chunk), 1) <= lax.broadcasted_iota(jnp.int32, (chunk, chunk), 0))
    b = _dot_split(tri.astype(BF16), log_a)
    qs = q_ref[...] * (GLA_DK ** -0.5)
    k = k_ref[...]
    a_t = jnp.zeros((chunk, LANES), F32)
    for t in range(min(chunk, t_valid)):
        n_s = -(-(t + 1) // 8) * 8
        seen = lax.broadcasted_iota(jnp.int32, (n_s, GLA_DK), 0) <= t
        w = jnp.where(seen, jnp.exp(jnp.where(seen, b[t:t + 1] - b[0:n_s], 0.0)), 0.0)
        col = jnp.sum(k[0:n_s] * w * qs[t:t + 1], axis=-1, keepdims=True)
        filled = jnp.where(lax.broadcasted_iota(jnp.int32, (n_s, LANES), 1) == t, col, a_t[0:n_s])
        a_t = filled if n_s == chunk else jnp.concatenate([filled, a_t[n_s:]], axis=0)
    v = v_ref[...]
    v_bf = v.astype(BF16)
    st = st_ref[...]
    o = _dot_tn(a_t[:, 0:chunk].astype(BF16), v_bf) + _dot_nt((qs * jnp.exp(b)).astype(BF16), st.astype(BF16))
    b_last = b[chunk - 1:chunk, :]
    k_dec = (k * jnp.exp(b_last - b)).astype(BF16)
    st_new = st * jnp.exp(b_last) + _dot_tn(v_bf, k_dec)
    st_ref[...] = st_new
    o = _rms(o, ng_ref[...])
    gr = gr_ref[...]
    o_ref[...] = (o * (gr * jax.nn.sigmoid(gr))).astype(o_ref.dtype)

    @pl.when(c == pl.num_programs(2) - 1)
    def _():
        sT_out_ref[...] = st_new.T


def gla(z, wlr, blr, norm_g, s0, b_sz, t_len, chunk, t_valid):
    assert chunk <= LANES and chunk % 8 == 0 and t_len % chunk == 0
    nck = t_len // chunk

    def seg(col0, width):
        return pl.BlockSpec((chunk, width), lambda b, h, c: (b * nck + c, col0 // width + h))

    return pl.pallas_call(
        functools.partial(_gla_kernel, chunk=chunk, t_valid=t_valid),
        out_shape=(jax.ShapeDtypeStruct((b_sz * t_len, GLA_HEADS * GLA_DV), BF16),
                   jax.ShapeDtypeStruct((b_sz, GLA_HEADS, GLA_DK, GLA_DV), F32)),
        grid=(b_sz, GLA_HEADS, nck),
        in_specs=[seg(C_GQ, GLA_DK), seg(C_GK, GLA_DK), seg(C_GV, GLA_DV), seg(C_GR, GLA_DV),
                  pl.BlockSpec((chunk, LANES), lambda b, h, c: (b * nck + c, C_TAIL // LANES)),
                  pl.BlockSpec((None, LANES, GLA_DK), lambda b, h, c: (h, 0, 0)),
                  pl.BlockSpec((None, 1, GLA_DK), lambda b, h, c: (h, 0, 0)),
                  pl.BlockSpec((1, GLA_DV), lambda b, h, c: (0, 0)),
                  pl.BlockSpec((None, None, GLA_DK, GLA_DV), lambda b, h, c: (b, h, 0, 0))],
        out_specs=(pl.BlockSpec((chunk, GLA_DV), lambda b, h, c: (b * nck + c, h)),
                   pl.BlockSpec((None, None, GLA_DK, GLA_DV), lambda b, h, c: (b, h, 0, 0))),
        scratch_shapes=[pltpu.VMEM((GLA_DV, GLA_DK), F32)],
        compiler_params=_params(("parallel", "parallel", "arbitrary")),
        name="gla",
    )(z, z, z, z, z, wlr, blr, norm_g.reshape(1, GLA_DV), s0)


def _merge_kernel(on_ref, og_ref, wn_ref, wg_ref, ma_ref, mb_ref, y_ref):
    y = (jax.nn.sigmoid(ma_ref[...]) * _dot(on_ref[...], wn_ref[...])
         + jax.nn.sigmoid(mb_ref[...]) * _dot(og_ref[...], wg_ref[...]))
    y_ref[...] = y.astype(y_ref.dtype)


def merge(o_nsa, o_gla, w_nsa, w_gla, z, tm, tn):
    n = o_nsa.shape[0]
    return pl.pallas_call(
        _merge_kernel,
        out_shape=jax.ShapeDtypeStruct((n, D_MODEL), BF16),
        grid=(n // tm, D_MODEL // tn),
        in_specs=[pl.BlockSpec((tm, o_nsa.shape[1]), lambda i, j: (i, 0)),
                  pl.BlockSpec((tm, o_gla.shape[1]), lambda i, j: (i, 0)),
                  pl.BlockSpec((w_nsa.shape[0], tn), lambda i, j: (0, j)),
                  pl.BlockSpec((w_gla.shape[0], tn), lambda i, j: (0, j)),
                  pl.BlockSpec((tm, tn), lambda i, j: (i, C_MA // tn + j)),
                  pl.BlockSpec((tm, tn), lambda i, j: (i, C_MB // tn + j))],
        out_specs=pl.BlockSpec((tm, tn), lambda i, j: (i, j)),
        compiler_params=_params(("parallel", "arbitrary")),
        name="merge",
    )(o_nsa, o_gla, w_nsa, w_gla, z, z)


def _out_proj_kernel(y_ref, w_ref, x_ref, o_ref):
    o_ref[...] = x_ref[...] + _dot(y_ref[...], w_ref[...])


def out_proj(y, w_out, x, tm, tn):
    n = y.shape[0]
    return pl.pallas_call(
        _out_proj_kernel,
        out_shape=jax.ShapeDtypeStruct((n, D_MODEL), F32),
        grid=(n // tm, D_MODEL // tn),
        in_specs=[pl.BlockSpec((tm, D_MODEL), lambda i, j: (i, 0)),
                  pl.BlockSpec((D_MODEL, tn), lambda i, j: (0, j)),
                  pl.BlockSpec((tm, tn), lambda i, j: (i, j))],
        out_specs=pl.BlockSpec((tm, tn), lambda i, j: (i, j)),
        compiler_params=_params(("parallel", "arbitrary")),
        name="out_proj",
    )(y, w_out, x)


def _top_values(s, count):
    vals = []
    for _ in range(count):
        m = jnp.max(s, axis=0, keepdims=True)
        s = jnp.where(s == m, REMOVED, s)
        vals.append(m)
    return vals


def _peer_scores_kernel(q_ref, k1_ref, k2_ref, s1_ref, s2_ref, e1_ref, e2_ref, tau_ref):
    taus = []
    for h in range(PEER_HEADS):
        q1 = q_ref[:, h * 2 * PEER_HALF:h * 2 * PEER_HALF + PEER_HALF].astype(BF16)
        q2 = q_ref[:, h * 2 * PEER_HALF + PEER_HALF:(h + 1) * 2 * PEER_HALF].astype(BF16)
        s1 = _dot_nt(k1_ref[h], q1)
        s2 = _dot_nt(k2_ref[h], q2)
        v1 = _top_values(s1, PEER_TOPK)
        v2 = _top_values(s2, PEER_TOPK)
        v2_all = jnp.concatenate(v2, axis=0)
        cand = jnp.concatenate([v1[i] + v2_all for i in range(PEER_TOPK)], axis=0)
        top = _top_values(cand, PEER_TOPK)
        z = sum(jnp.exp(t - top[0]) for t in top)
        s1_ref[h] = s1
        s2_ref[h] = s2
        e1_ref[h] = jnp.exp(s1 - v1[0])
        e2_ref[h] = jnp.exp(s2 - v2[0]) / z
        taus.append(top[PEER_TOPK - 1])
    tau_ref[...] = jnp.concatenate(taus, axis=0)


def peer_scores(qp, keys1, keys2, tn):
    n = qp.shape[0]
    big = jax.ShapeDtypeStruct((PEER_HEADS, PEER_NKEYS, n), F32)
    big_spec = pl.BlockSpec((PEER_HEADS, PEER_NKEYS, tn), lambda i: (0, 0, i))
    key_spec = pl.BlockSpec((PEER_HEADS, PEER_NKEYS, PEER_HALF), lambda i: (0, 0, 0))
    return pl.pallas_call(
        _peer_scores_kernel,
        out_shape=(big, big, big, big, jax.ShapeDtypeStruct((PEER_HEADS, n), F32)),
        grid=(n // tn,),
        in_specs=[pl.BlockSpec((tn, qp.shape[1]), lambda i: (i, 0)), key_spec, key_spec],
        out_specs=(big_spec, big_spec, big_spec, big_spec, pl.BlockSpec((PEER_HEADS, tn), lambda i: (0, i))),
        compiler_params=_params(("parallel",)),
        name="peer_scores",
    )(qp, keys1, keys2)


PEER_I1_PER_TILE = 8


def _peer_dense_kernel(x_ref, g2_ref, gf_ref, u_ref, v_ref, s1_ref, e1_ref, s2_ref, e2_ref, tau_ref, o_ref,
                       h_ref, acc_ref, pre_ref, gated_ref):
    e = pl.program_id(1)

    @pl.when(e == 0)
    def _():
        h_ref[...] = _rms(x_ref[...], g2_ref[...]).astype(BF16)
        acc_ref[...] = jnp.zeros_like(acc_ref)
        pre_ref[...] = jnp.zeros_like(pre_ref)
        gated_ref[...] = jnp.zeros_like(gated_ref)

    acc_ref[...] += _dot_tn(gated_ref[...], v_ref[...])
    n_tok = pre_ref.shape[1]
    tile = min(LANES, n_tok)
    for c in range(PEER_I1_PER_TILE):
        for t0 in range(0, n_tok, tile):
            tok = slice(t0, t0 + tile)
            w = jnp.zeros((PEER_NKEYS, tile), F32)
            for h in range(PEER_HEADS):
                keep = s1_ref[h, c:c + 1, tok] + s2_ref[h, :, tok] >= tau_ref[h:h + 1, tok]
                w = w + jnp.where(keep, e1_ref[h, c:c + 1, tok] * e2_ref[h, :, tok], 0.0)
            rows = slice(c * PEER_NKEYS, (c + 1) * PEER_NKEYS)
            gated_ref[rows, tok] = (w * _gelu_tanh(pre_ref[rows, tok])).astype(BF16)
    pre_ref[...] = _dot_nt(u_ref[...], h_ref[...])

    @pl.when(e == pl.num_programs(1) - 1)
    def _():
        o_ref[...] = _rms(x_ref[...] + acc_ref[...], gf_ref[...])


PEER_PIPE = 2


def peer_dense(x, g2, gf, u_bf, v_bf, s1, s2, e1, e2, tau, tn):
    n = x.shape[0]
    te = PEER_I1_PER_TILE * PEER_NKEYS
    n_tiles = u_bf.shape[0] // te
    last = n_tiles - 1
    sub = pl.BlockSpec((PEER_HEADS, PEER_I1_PER_TILE, tn), lambda i, e: (0, jnp.clip(e - 1, 0, last), i))
    full = pl.BlockSpec((PEER_HEADS, PEER_NKEYS, tn), lambda i, e: (0, 0, i))
    vec = pl.BlockSpec((1, D_MODEL), lambda i, e: (0, 0))
    return pl.pallas_call(
        _peer_dense_kernel,
        out_shape=jax.ShapeDtypeStruct((n, D_MODEL), F32),
        grid=(n // tn, n_tiles + PEER_PIPE),
        in_specs=[pl.BlockSpec((tn, D_MODEL), lambda i, e: (i, 0)), vec, vec,
                  pl.BlockSpec((te, D_MODEL), lambda i, e: (jnp.minimum(e, last), 0)),
                  pl.BlockSpec((te, D_MODEL), lambda i, e: (jnp.clip(e - 2, 0, last), 0)),
                  sub, sub, full, full,
                  pl.BlockSpec((PEER_HEADS, tn), lambda i, e: (0, i))],
        out_specs=pl.BlockSpec((tn, D_MODEL), lambda i, e: (i, 0)),
        scratch_shapes=[pltpu.VMEM((tn, D_MODEL), BF16), pltpu.VMEM((tn, D_MODEL), F32),
                        pltpu.VMEM((te, tn), F32), pltpu.VMEM((te, tn), BF16)],
        compiler_params=_params(("parallel", "arbitrary")),
        name="peer_dense",
    )(x, g2.reshape(1, D_MODEL), gf.reshape(1, D_MODEL), u_bf, v_bf, s1, e1, s2, e2, tau)


def _pack_w_in(w_in):
    offs = np.concatenate([[0], np.cumsum(IN_SPLITS)])
    seg = lambda i: w_in[:, offs[i]:offs[i + 1]]
    g_nsa, glr = seg(7), seg(12)
    per_group = 3 * NSA_HPG
    zeros = lambda n: jnp.zeros((w_in.shape[0], n), w_in.dtype)
    tail = []
    for g in range(NSA_GROUPS):
        tail.append(g_nsa[:, g * per_group:(g + 1) * per_group])
        if g == 0:
            tail += [zeros(GLR_LANE - per_group), glr, zeros(LANES - GLR_LANE - GLA_RANK)]
        else:
            tail.append(zeros(LANES - per_group))
    packed = jnp.concatenate([seg(i) for i in (0, 1, 2, 3, 4, 5, 6, 8, 9, 10, 11, 13, 14)] + tail, axis=1)
    assert packed.shape[1] == Z_COLS
    return packed.astype(BF16)


def _pack_cmp(pe, w1, w2):
    w1cat = jnp.concatenate([w1[:CMP_STRIDE].reshape(CMP_STRIDE * NSA_DK, CMP_HIDDEN),
                             w1[CMP_STRIDE:].reshape(CMP_STRIDE * NSA_DK, CMP_HIDDEN)], axis=1).astype(BF16)
    return (w1cat, pe.reshape(1, CMP_LEN * NSA_DK).astype(BF16),
            w1.reshape(CMP_LEN * NSA_DK, CMP_HIDDEN).astype(BF16), w2.astype(BF16))


def _row_tile(n, cap):
    t = min(n, cap)
    while n % t:
        t //= 2
    return t


def _channel_tail(x2d, z, o_nsa, o_gla, wts):
    n = x2d.shape[0]
    tm = _row_tile(n, 512)
    y = merge(o_nsa, o_gla, wts["w_nsa"], wts["w_gla"], z, tm, 512)
    x1 = out_proj(y, wts["w_out"], x2d, tm, 512)
    qp = norm_matmul(x1, wts["norm2_g"], wts["w_q"], tm, 512)
    tn = _row_tile(n, 256)
    s1, s2, e1, e2, tau = peer_scores(qp, wts["keys1"], wts["keys2"], tn)
    return peer_dense(x1, wts["norm2_g"], wts["norm_f_g"], wts["u"], wts["v"], s1, s2, e1, e2, tau, _row_tile(n, 512))


def kernel(x_prompt, x_sample, cache_k_cmp, cache_v_cmp, cache_k_slc, cache_v_slc, cache_k_win, cache_v_win, state_gla, page_table, norm1_g, w_in, cmp_pe_k, cmp_w1_k, cmp_w2_k, cmp_pe_v, cmp_w1_v, cmp_w2_v, gla_w_lr2, gla_b_lr, gla_norm_g, w_nsa_proj, w_gla_proj, w_out, norm2_g, peer_w_q, peer_keys1, peer_keys2, peer_u, peer_v, norm_f_g):
    b_sz, t_len, _ = x_prompt.shape
    bd, s_len, _ = x_sample.shape
    n_pool = cache_k_cmp.shape[0]
    wb = cache_k_win.shape[1]

    w_pack = _pack_w_in(w_in)
    cmp_k = _pack_cmp(cmp_pe_k, cmp_w1_k, cmp_w2_k)
    cmp_v = _pack_cmp(cmp_pe_v, cmp_w1_v, cmp_w2_v)
    cmp_kv = [jnp.stack([a, b]) for a, b in zip(cmp_k, cmp_v)]
    wlr = jnp.zeros((LANES, GLA_HEADS * GLA_DK), F32).at[GLR_LANE:GLR_LANE + GLA_RANK].set(gla_w_lr2)
    wlr = wlr.reshape(LANES, GLA_HEADS, GLA_DK).transpose(1, 0, 2).astype(BF16)
    blr = gla_b_lr.reshape(GLA_HEADS, 1, GLA_DK)
    wts = dict(w_nsa=w_nsa_proj.astype(BF16), w_gla=w_gla_proj.astype(BF16), w_out=w_out.astype(BF16),
               w_q=peer_w_q.astype(BF16), norm2_g=norm2_g, norm_f_g=norm_f_g,
               keys1=peer_keys1.astype(BF16), keys2=peer_keys2.astype(BF16),
               u=peer_u.astype(BF16), v=peer_v.astype(BF16))

    n_p = b_sz * t_len
    xp = x_prompt.reshape(n_p, D_MODEL)
    z_p, *heads_p = in_proj(xp, norm1_g, w_pack, _row_tile(n_p, 512))
    fs_p = half_proj_dense(z_p, cmp_kv[0], b_sz, t_len)
    kcvc_p = compress_finish(fs_p, cmp_kv[1], cmp_kv[2], cmp_kv[3], 1)
    o_nsa_p = nsa_prompt(z_p, kcvc_p, b_sz, t_len)
    s0 = jnp.zeros((b_sz, GLA_HEADS, GLA_DK, GLA_DV), F32)
    o_gla_p, gla_state_p = gla(z_p, wlr, blr, gla_norm_g, s0, b_sz, t_len, GLA_CHUNK, GLA_CHUNK)
    y_prompt = _channel_tail(xp, z_p, o_nsa_p, o_gla_p, wts).reshape(b_sz, t_len, D_MODEL)
    kcr_p, vcr_p, ksr_p, vsr_p, kwr_p, vwr_p = (a.reshape(b_sz, t_len, NSA_GROUPS, NSA_DK) for a in heads_p)
    wl = min(WINDOW, t_len)
    k_win_p = kwr_p[:, t_len - wl:]
    v_win_p = vwr_p[:, t_len - wl:]

    n_s = bd * s_len
    xs = x_sample.reshape(n_s, D_MODEL)
    z_s, *heads_s = in_proj(xs, norm1_g, w_pack, _row_tile(n_s, 512))
    kcr_s, vcr_s, ksr_s, vsr_s, kwr_s, vwr_s = (a.reshape(bd, s_len, NSA_GROUPS, NSA_DK) for a in heads_s)

    pool2d = lambda p: p.reshape(n_pool, PAGE_SIZE * NSA_GROUPS, NSA_DK)
    pages = 16

    def compress_pool(pool, prm):
        fs = half_proj_paged(pool2d(pool), page_table, prm[0], pages)
        return compress_finish(fs[None, :, None], prm[1][None], prm[2][None], prm[3][None], NSA_GROUPS)[0, :, 0]

    kc_s = compress_pool(cache_k_cmp, cmp_k)
    vc_s = compress_pool(cache_v_cmp, cmp_v)

    assert wb >= s_len
    k_win_s = jnp.concatenate([cache_k_win[:, s_len:], kwr_s], axis=1)
    v_win_s = jnp.concatenate([cache_v_win[:, s_len:], vwr_s], axis=1)
    win_2d = lambda a: a.reshape(bd, wb * NSA_GROUPS, NSA_DK)

    def new_rows(a):
        return jnp.pad(a.transpose(0, 2, 1, 3), ((0, 0), (0, 0), (0, S_PAD - s_len), (0, 0)))

    rows = NSA_HPG * s_len
    q_s = z_s[:, C_Q:C_Q + NSA_HEADS * NSA_DK].reshape(bd, s_len, NSA_GROUPS, NSA_HPG, NSA_DK)
    q_s = q_s.transpose(0, 2, 3, 1, 4).reshape(bd, NSA_GROUPS, rows, NSA_DK)
    gate_s = z_s[:, C_TAIL:C_TAIL + 512].reshape(bd, s_len, NSA_GROUPS, LANES)[..., :3 * NSA_HPG]
    gate_s = gate_s.reshape(bd, s_len, NSA_GROUPS, NSA_HPG, 3).transpose(0, 2, 3, 1, 4).reshape(bd, NSA_GROUPS, rows, 3)
    gate_s = jnp.pad(gate_s, ((0, 0), (0, 0), (0, 0), (0, LANES - 3)))
    o_s = nsa_sample(q_s, kc_s, vc_s, new_rows(ksr_s), new_rows(vsr_s), win_2d(cache_k_win), win_2d(cache_v_win),
                     new_rows(kwr_s), new_rows(vwr_s), gate_s, pool2d(cache_k_slc), pool2d(cache_v_slc), page_table,
                     s_len, pages)
    o_nsa_s = o_s.reshape(bd, NSA_GROUPS, NSA_HPG, s_len, NSA_DK).transpose(0, 3, 1, 2, 4)
    o_nsa_s = o_nsa_s.reshape(n_s, NSA_HEADS * NSA_DK).astype(BF16)

    z_s_pad = jnp.pad(z_s.reshape(bd, s_len, Z_COLS), ((0, 0), (0, S_PAD - s_len), (0, 0))).reshape(bd * S_PAD, Z_COLS)
    o_gla_s, gla_state_s = gla(z_s_pad, wlr, blr, gla_norm_g, state_gla, bd, S_PAD, S_PAD, s_len)
    o_gla_s = o_gla_s.reshape(bd, S_PAD, GLA_HEADS * GLA_DV)[:, :s_len].reshape(n_s, GLA_HEADS * GLA_DV)
    y_sample = _channel_tail(xs, z_s, o_nsa_s, o_gla_s, wts).reshape(bd, s_len, D_MODEL)

    return (y_prompt, y_sample, kcr_p, vcr_p, ksr_p, vsr_p, k_win_p, v_win_p, gla_state_p,
            kcr_s, vcr_s, ksr_s, vsr_s, k_win_s, v_win_s, gla_state_s)
```

```python
import functools

import numpy as np
import jax
import jax.numpy as jnp
from jax import lax
from jax.experimental import pallas as pl
from jax.experimental.pallas import tpu as pltpu

F32 = jnp.float32
BF16 = jnp.bfloat16

D_MODEL = 2048
PAGE_SIZE = 128
NSA_HEADS = 16
NSA_GROUPS = 4
NSA_HPG = NSA_HEADS // NSA_GROUPS
NSA_DK = 128
NSA_SCALE = NSA_DK ** -0.5
CMP_LEN = 32
CMP_STRIDE = 16
CMP_HIDDEN = 256
SEL_BLOCK = 64
N_SEL = 16
WINDOW = 512
QUERY_BLOCK = 256
GLA_HEADS = 4
GLA_DK = 256
GLA_DV = 512
GLA_RANK = 16
GLA_TAU = 16.0
GLA_CHUNK = 64
PEER_HEADS = 8
PEER_NKEYS = 128
PEER_HALF = 128
PEER_TOPK = 16
NORM_EPS = 1e-6
NEG = -1e30
BIG = 1e30
PAD_SCORE = -3e38
REMOVED = -float("inf")

IN_SPLITS = (NSA_HEADS * NSA_DK,) + (NSA_GROUPS * NSA_DK,) * 6 + (
    3 * NSA_HEADS, GLA_HEADS * GLA_DK, GLA_HEADS * GLA_DK, GLA_HEADS * GLA_DV,
    GLA_HEADS * GLA_DV, GLA_RANK, D_MODEL, D_MODEL)

GD = NSA_GROUPS * NSA_DK
C_Q = 0
C_KC = C_Q + NSA_HEADS * NSA_DK
C_VC = C_KC + GD
C_KS = C_VC + GD
C_VS = C_KS + GD
C_KW = C_VS + GD
C_VW = C_KW + GD
C_GQ = C_VW + GD
C_GK = C_GQ + GLA_HEADS * GLA_DK
C_GV = C_GK + GLA_HEADS * GLA_DK
C_GR = C_GV + GLA_HEADS * GLA_DV
C_MA = C_GR + GLA_HEADS * GLA_DV
C_MB = C_MA + D_MODEL
C_TAIL = C_MB + D_MODEL
Z_COLS = C_TAIL + 512
GLR_LANE = 16
LANES = 128

VMEM_LIMIT = 56 * 1024 * 1024


def _params(sem):
    return pltpu.CompilerParams(dimension_semantics=sem, vmem_limit_bytes=VMEM_LIMIT)


def _dot(a, b):
    return jnp.dot(a, b, preferred_element_type=F32)


def _dot_nt(a, b):
    return lax.dot_general(a, b, (((1,), (1,)), ((), ())), preferred_element_type=F32)


def _dot_tn(a, b):
    return lax.dot_general(a, b, (((0,), (0,)), ((), ())), preferred_element_type=F32)


def _rms(x, g):
    return x * lax.rsqrt(jnp.mean(x * x, axis=-1, keepdims=True) + NORM_EPS) * g


GELU_C0 = 2.0 * (2.0 / np.pi) ** 0.5
GELU_C1 = GELU_C0 * 0.044715


def _gelu_tanh(x):
    return x / (1.0 + jnp.exp(-(x * (GELU_C0 + GELU_C1 * (x * x)))))


def _bf16_pieces(x):
    hi = x.astype(BF16)
    rest = x - hi.astype(F32)
    mid = rest.astype(BF16)
    lo = (rest - mid.astype(F32)).astype(BF16)
    return hi, mid, lo


def _dot_nt_split(small_ints_bf16, x):
    return sum(_dot_nt(small_ints_bf16, piece) for piece in _bf16_pieces(x))


def _dot_split(small_ints_bf16, x):
    return sum(_dot(small_ints_bf16, piece) for piece in _bf16_pieces(x))


def _norm_matmul_kernel(x_ref, g_ref, w_ref, o_ref, h_ref):
    @pl.when(pl.program_id(1) == 0)
    def _():
        h_ref[...] = _rms(x_ref[...], g_ref[...]).astype(BF16)

    o_ref[...] = _dot(h_ref[...], w_ref[...])


def norm_matmul(x, g, w, tm, tn):
    n, d = x.shape
    cols = w.shape[1]
    return pl.pallas_call(
        _norm_matmul_kernel,
        out_shape=jax.ShapeDtypeStruct((n, cols), F32),
        grid=(n // tm, cols // tn),
        in_specs=[pl.BlockSpec((tm, d), lambda i, j: (i, 0)),
                  pl.BlockSpec((1, d), lambda i, j: (0, 0)),
                  pl.BlockSpec((d, tn), lambda i, j: (0, j))],
        out_specs=pl.BlockSpec((tm, tn), lambda i, j: (i, j)),
        scratch_shapes=[pltpu.VMEM((tm, d), BF16)],
        compiler_params=_params(("parallel", "arbitrary")),
        name="norm_matmul",
    )(x, g.reshape(1, d), w)


KV_SEGMENTS = 6


def _split_heads_kernel(z_ref, *head_refs):
    k = pl.program_id(1)
    for seg in range(KV_SEGMENTS):
        @pl.when(k == seg)
        def _(seg=seg):
            for g in range(NSA_GROUPS):
                head_refs[seg][:, g, :] = z_ref[:, g * NSA_DK:(g + 1) * NSA_DK]


def split_heads(z, tm):
    n = z.shape[0]
    heads = jax.ShapeDtypeStruct((n, NSA_GROUPS, NSA_DK), F32)
    return pl.pallas_call(
        _split_heads_kernel,
        out_shape=(heads,) * KV_SEGMENTS,
        grid=(n // tm, KV_SEGMENTS),
        in_specs=[pl.BlockSpec((tm, GD), lambda i, k: (i, C_KC // GD + k))],
        out_specs=(pl.BlockSpec((tm, NSA_GROUPS, NSA_DK), lambda i, k: (i, 0, 0)),) * KV_SEGMENTS,
        compiler_params=_params(("parallel", "arbitrary")),
        name="split_heads",
    )(z)


def _half_rows(ref, first_row, n_half, row_stride, lane0):
    parts = [ref[pl.ds(first_row + r * row_stride, n_half, stride=CMP_STRIDE * row_stride), pl.ds(lane0, NSA_DK)]
             for r in range(CMP_STRIDE)]
    return jnp.concatenate(parts, axis=1)


def _half_proj_dense_kernel(x_ref, w_ref, o_ref, *, n_half):
    o_ref[...] = _dot(_half_rows(x_ref, 0, n_half, 1, 0).astype(BF16), w_ref[...])


def half_proj_dense(z, w1cat, b_sz, t_len):
    n_half = t_len // CMP_STRIDE
    return pl.pallas_call(
        functools.partial(_half_proj_dense_kernel, n_half=n_half),
        out_shape=jax.ShapeDtypeStruct((2, b_sz, NSA_GROUPS, n_half, 2 * CMP_HIDDEN), F32),
        grid=(2, b_sz, NSA_GROUPS),
        in_specs=[pl.BlockSpec((t_len, NSA_DK), lambda kv, b, g: (b, C_KC // NSA_DK + kv * NSA_GROUPS + g)),
                  pl.BlockSpec((None, CMP_STRIDE * NSA_DK, 2 * CMP_HIDDEN), lambda kv, b, g: (kv, 0, 0))],
        out_specs=pl.BlockSpec((None, None, None, n_half, 2 * CMP_HIDDEN), lambda kv, b, g: (kv, b, g, 0, 0)),
        compiler_params=_params(("parallel", "parallel", "parallel")),
        name="half_proj_dense",
    )(z, w1cat)


HALVES_PER_PAGE = PAGE_SIZE // CMP_STRIDE


def _half_proj_paged_kernel(pt_ref, *refs, pages):
    page_refs, w_ref, o_ref = refs[:pages], refs[pages], refs[pages + 1]
    tile = 2 * NSA_GROUPS
    rows_per_half = CMP_STRIDE * NSA_GROUPS
    low = lax.broadcasted_iota(jnp.int32, (tile, NSA_DK), 0) < NSA_GROUPS
    acc = None
    for r2 in range(CMP_STRIDE // 2):
        even, odd = [], []
        for p in range(pages):
            for n in range(0, HALVES_PER_PAGE, 2):
                a = page_refs[p][pl.ds(n * rows_per_half + tile * r2, tile), :]
                b = page_refs[p][pl.ds((n + 1) * rows_per_half + tile * r2, tile), :]
                even.append(jnp.where(low, a, pltpu.roll(b, NSA_GROUPS, 0)))
                odd.append(jnp.where(low, pltpu.roll(a, NSA_GROUPS, 0), b))
        lhs = jnp.concatenate([jnp.concatenate(even, axis=0), jnp.concatenate(odd, axis=0)], axis=1).astype(BF16)
        part = _dot(lhs, w_ref[pl.ds(r2 * 2 * NSA_DK, 2 * NSA_DK), :])
        acc = part if acc is None else acc + part
    o_ref[...] = acc


def half_proj_paged(pool2d, page_table, w1cat_one, pages):
    bd, n_pages = page_table.shape
    n_half = n_pages * HALVES_PER_PAGE
    rows = pages * HALVES_PER_PAGE * NSA_GROUPS

    def page_spec(i):
        return pl.BlockSpec((None, PAGE_SIZE * NSA_GROUPS, NSA_DK),
                            lambda b, j, pt: (pt[b, j * pages + i], 0, 0))

    grid_spec = pltpu.PrefetchScalarGridSpec(
        num_scalar_prefetch=1,
        grid=(bd, n_pages // pages),
        in_specs=[page_spec(i) for i in range(pages)]
        + [pl.BlockSpec((CMP_STRIDE * NSA_DK, 2 * CMP_HIDDEN), lambda b, j, pt: (0, 0))],
        out_specs=pl.BlockSpec((None, rows, 2 * CMP_HIDDEN), lambda b, j, pt: (b, j, 0)),
    )
    return pl.pallas_call(
        functools.partial(_half_proj_paged_kernel, pages=pages),
        out_shape=jax.ShapeDtypeStruct((bd, n_half * NSA_GROUPS, 2 * CMP_HIDDEN), F32),
        grid_spec=grid_spec,
        compiler_params=_params(("parallel", "arbitrary")),
        name="half_proj_paged",
    )(page_table, *([pool2d] * pages), w1cat_one)


def _compress_finish_kernel(fs_ref, pe_ref, w1_ref, w2_ref, o_ref, *, n_rows, step):
    pe = jnp.broadcast_to(pe_ref[...], (8, CMP_LEN * NSA_DK))
    c = _dot(pe, w1_ref[...])[0:1]
    first = fs_ref[:, 0:CMP_HIDDEN]
    second_next = pltpu.roll(fs_ref[:, CMP_HIDDEN:2 * CMP_HIDDEN], n_rows - step, 0)
    hid = _gelu_tanh(first + second_next + c)
    out = _dot(hid.astype(BF16), w2_ref[...])
    row = lax.broadcasted_iota(jnp.int32, (n_rows, 1), 0)
    o_ref[...] = jnp.where(row < n_rows - step, out, 0.0)


def compress_finish(fs, pe, w1, w2, step):
    kv, b_sz, n_sets, n_rows, _ = fs.shape
    return pl.pallas_call(
        functools.partial(_compress_finish_kernel, n_rows=n_rows, step=step),
        out_shape=jax.ShapeDtypeStruct((kv, b_sz, n_sets, n_rows, NSA_DK), F32),
        grid=(kv, b_sz, n_sets),
        in_specs=[pl.BlockSpec((None, None, None, n_rows, 2 * CMP_HIDDEN), lambda k, b, g: (k, b, g, 0, 0)),
                  pl.BlockSpec((None, 1, CMP_LEN * NSA_DK), lambda k, b, g: (k, 0, 0)),
                  pl.BlockSpec((None, CMP_LEN * NSA_DK, CMP_HIDDEN), lambda k, b, g: (k, 0, 0)),
                  pl.BlockSpec((None, CMP_HIDDEN, NSA_DK), lambda k, b, g: (k, 0, 0))],
        out_specs=pl.BlockSpec((None, None, None, n_rows, NSA_DK), lambda k, b, g: (k, b, g, 0, 0)),
        compiler_params=_params(("parallel", "parallel", "parallel")),
        name="compress_finish",
    )(fs, pe, w1, w2)


def _cmp_probs(s, pos, n_cmp):
    c = lax.broadcasted_iota(jnp.int32, (1, s.shape[1]), 1)
    mask = (c * CMP_STRIDE + (CMP_LEN - 1) <= pos) & (c < n_cmp)
    s = jnp.where(mask, s, NEG)
    m = jnp.max(s, axis=-1, keepdims=True)
    e = jnp.where(mask, jnp.exp(s - m), 0.0)
    l = jnp.sum(e, axis=-1, keepdims=True)
    return e / jnp.where(l > 0.0, l, 1.0)


def _block_scores_t(pg, n_rows):
    ncp = pg.shape[1]
    blk = lax.broadcasted_iota(jnp.int32, (n_rows, ncp), 0)
    c = lax.broadcasted_iota(jnp.int32, (n_rows, ncp), 1)
    hpb = SEL_BLOCK // CMP_STRIDE
    spread = (c // hpb == blk).astype(F32) + ((c + 1) // hpb == blk).astype(F32)
    return _dot_nt_split(spread.astype(BF16), pg)


def _select_blocks_t(ps_t, pos_row, n_blk):
    blk = lax.broadcasted_iota(jnp.int32, ps_t.shape, 0)
    cur = pos_row // SEL_BLOCK
    forced = (blk == 0) | (blk == cur) | (blk == cur - 1)
    score = jnp.where(forced, BIG, jnp.where(blk <= cur, ps_t, NEG))
    score = jnp.where(blk < n_blk, score, PAD_SCORE)
    blk_f = blk.astype(F32)
    sel = jnp.zeros(ps_t.shape, F32)
    for _ in range(N_SEL):
        m = jnp.max(score, axis=0, keepdims=True)
        first = jnp.min(jnp.where(score == m, blk_f, float(ps_t.shape[0])), axis=0, keepdims=True)
        hit = blk_f == first
        sel = jnp.where(hit, 1.0, sel)
        score = jnp.where(hit, REMOVED, score)
    return sel


def _lane_groups(x, op):
    parts = [x[:, i * LANES:(i + 1) * LANES] for i in range(x.shape[1] // LANES)]
    while len(parts) > 1:
        parts = [op(parts[i], parts[i + 1]) for i in range(0, len(parts) - 1, 2)] + (
            [parts[-1]] if len(parts) % 2 else [])
    return parts[0]


def _block_expand(n_blocks, n_keys):
    blk = lax.broadcasted_iota(jnp.int32, (n_blocks, n_keys), 0)
    key = lax.broadcasted_iota(jnp.int32, (n_blocks, n_keys), 1)
    return (key // SEL_BLOCK == blk).astype(BF16)


def _online_update(state, s, mask, v_bf):
    m, l, acc = state
    s = jnp.where(mask, s, NEG)
    m_new = jnp.maximum(m, jnp.max(s, axis=-1, keepdims=True))
    alpha = jnp.exp(m - m_new)
    p = jnp.where(mask, jnp.exp(s - m_new), 0.0)
    l = alpha * l + jnp.sum(p, axis=-1, keepdims=True)
    acc = alpha * acc + _dot(p.astype(BF16), v_bf)
    return m_new, l, acc


def _masked_attend(q_bf, k_bf, v_bf, mask):
    s = jnp.where(mask, _dot_nt(q_bf, k_bf), NEG)
    m = jnp.max(s, axis=-1, keepdims=True)
    e = jnp.where(mask, jnp.exp(s - m), 0.0)
    l = jnp.sum(e, axis=-1, keepdims=True)
    return _dot((e / l).astype(BF16), v_bf)


SEL_KEYS = 512
BLK_LANES = 128


def _nsa_prompt_kernel(q_ref, kc_ref, vc_ref, ks_ref, vs_ref, kw_ref, vw_ref, gate_ref, o_ref, sel_ref, s_ref,
                       kv_ref, *, t_len):
    qb = QUERY_BLOCK
    start = pl.program_id(2) * qb

    @pl.when(pl.program_id(2) == 0)
    def _():
        for i, ref in enumerate((ks_ref, vs_ref, kw_ref, vw_ref)):
            kv_ref[i] = ref[...].astype(BF16)

    pos = start + lax.broadcasted_iota(jnp.int32, (qb, 1), 0)
    pos_row = start + lax.broadcasted_iota(jnp.int32, (1, qb), 1)
    n_cmp = t_len // CMP_STRIDE - 1
    n_blk = max(-(-t_len // SEL_BLOCK), N_SEL)
    q = q_ref[...] * NSA_SCALE
    qh = [q[:, h * NSA_DK:(h + 1) * NSA_DK].astype(BF16) for h in range(NSA_HPG)]

    kc = kc_ref[...].astype(BF16)
    vc = vc_ref[...].astype(BF16)
    o_cmp = []
    pg = jnp.zeros((qb, kc.shape[0]), F32)
    for h in range(NSA_HPG):
        p = _cmp_probs(_dot_nt(qh[h], kc), pos, n_cmp)
        o_cmp.append(_dot(p.astype(BF16), vc))
        pg = pg + p
    sel_ref[...] = _select_blocks_t(_block_scores_t(pg, BLK_LANES), pos_row, n_blk)

    w_rows = WINDOW + qb
    base = pl.multiple_of(jnp.maximum(start - WINDOW, 0), qb)
    kw = kv_ref[2, pl.ds(base, w_rows), :]
    vw = kv_ref[3, pl.ds(base, w_rows), :]
    dist = pos - (base + lax.broadcasted_iota(jnp.int32, (1, w_rows), 1))
    w_mask = (dist >= 0) & (dist <= WINDOW)
    o_win = [_masked_attend(qh[h], kw, vw, w_mask) for h in range(NSA_HPG)]

    n_steps = (start + qb + SEL_KEYS - 1) // SEL_KEYS
    blocks_per_step = SEL_KEYS // SEL_BLOCK
    expand = _block_expand(blocks_per_step, SEL_KEYS)

    def scores_step(j, m_run):
        k0 = pl.multiple_of(j * SEL_KEYS, SEL_KEYS)
        k_bf = kv_ref[0, pl.ds(k0, SEL_KEYS), :]
        key = k0 + lax.broadcasted_iota(jnp.int32, (1, SEL_KEYS), 1)
        chosen = sel_ref[pl.ds(pl.multiple_of(j * blocks_per_step, blocks_per_step), blocks_per_step), :]
        mask = (_dot_tn(chosen.astype(BF16), expand) > 0.5) & (key <= pos)
        out = []
        for h in range(NSA_HPG):
            s = jnp.where(mask, _dot_nt(qh[h], k_bf), NEG)
            s_ref[j, h] = s
            out.append(jnp.maximum(m_run[h], _lane_groups(s, jnp.maximum)))
        return tuple(out)

    m_run = lax.fori_loop(0, n_steps, scores_step, tuple(jnp.full((qb, LANES), NEG, F32) for _ in range(NSA_HPG)))
    m_fin = [jnp.max(m, axis=-1, keepdims=True) for m in m_run]

    def values_step(j, carry):
        k0 = pl.multiple_of(j * SEL_KEYS, SEL_KEYS)
        v_bf = kv_ref[1, pl.ds(k0, SEL_KEYS), :]
        out = []
        for h in range(NSA_HPG):
            l_run, acc = carry[h]
            p = jnp.exp(s_ref[j, h] - m_fin[h])
            out.append((l_run + _lane_groups(p, jnp.add), acc + _dot(p.astype(BF16), v_bf)))
        return tuple(out)

    sel_state = lax.fori_loop(0, n_steps, values_step,
                              tuple((jnp.zeros((qb, LANES), F32), jnp.zeros((qb, NSA_DK), F32))
                                    for _ in range(NSA_HPG)))

    gates = jax.nn.sigmoid(gate_ref[...])
    outs = []
    for h in range(NSA_HPG):
        l_run, acc = sel_state[h]
        o_sel = acc / jnp.sum(l_run, axis=-1, keepdims=True)
        outs.append(gates[:, 3 * h:3 * h + 1] * o_cmp[h] + gates[:, 3 * h + 1:3 * h + 2] * o_sel
                    + gates[:, 3 * h + 2:3 * h + 3] * o_win[h])
    o_ref[...] = jnp.concatenate(outs, axis=1).astype(o_ref.dtype)


def nsa_prompt(z, kcvc, b_sz, t_len):
    assert t_len % SEL_KEYS == 0 and t_len >= WINDOW + QUERY_BLOCK and t_len // SEL_BLOCK <= BLK_LANES
    nqb = t_len // QUERY_BLOCK
    n_half = t_len // CMP_STRIDE
    hd = NSA_HPG * NSA_DK

    def rows(col0):
        return pl.BlockSpec((t_len, NSA_DK), lambda b, g, i: (b, col0 // NSA_DK + g))

    def cmp_spec(kv):
        return pl.BlockSpec((None, None, None, n_half, NSA_DK), lambda b, g, i: (kv, b, g, 0, 0))

    return pl.pallas_call(
        functools.partial(_nsa_prompt_kernel, t_len=t_len),
        out_shape=jax.ShapeDtypeStruct((b_sz * t_len, NSA_HEADS * NSA_DK), BF16),
        grid=(b_sz, NSA_GROUPS, nqb),
        in_specs=[pl.BlockSpec((QUERY_BLOCK, hd), lambda b, g, i: (b * nqb + i, g)),
                  cmp_spec(0), cmp_spec(1),
                  rows(C_KS), rows(C_VS), rows(C_KW), rows(C_VW),
                  pl.BlockSpec((QUERY_BLOCK, LANES), lambda b, g, i: (b * nqb + i, C_TAIL // LANES + g))],
        out_specs=pl.BlockSpec((QUERY_BLOCK, hd), lambda b, g, i: (b * nqb + i, g)),
        scratch_shapes=[pltpu.VMEM((BLK_LANES, QUERY_BLOCK), F32),
                        pltpu.VMEM((t_len // SEL_KEYS, NSA_HPG, QUERY_BLOCK, SEL_KEYS), F32),
                        pltpu.VMEM((4, t_len, NSA_DK), BF16)],
        compiler_params=_params(("parallel", "parallel", "arbitrary")),
        name="nsa_prompt",
    )(z, kcvc, kcvc, z, z, z, z, z)


S_PAD = 8


def _nsa_sample_kernel(pt_ref, q_ref, kc_ref, vc_ref, kn_ref, vn_ref, kw_ref, vw_ref, kwn_ref, vwn_ref, gate_ref, *refs,
                       pages, past, s_len, n_cmp, ncp, n_blk, blk_rows, win_rows):
    kpages, vpages = refs[:pages], refs[pages:2 * pages]
    o_ref = refs[2 * pages]
    sel_ref, m_ref, l_ref, acc_ref, oc_ref = refs[2 * pages + 1:]
    j = pl.program_id(1)
    n_j = pl.num_programs(1)
    rows = NSA_HPG * s_len
    step = lax.broadcasted_iota(jnp.int32, (rows, 1), 0) % s_len
    pos = past + step
    qg = [(q_ref[g] * NSA_SCALE).astype(BF16) for g in range(NSA_GROUPS)]

    @pl.when(j == 0)
    def _():
        step_i = lax.broadcasted_iota(jnp.int32, (LANES, rows), 1) % s_len
        lane_i = lax.broadcasted_iota(jnp.int32, (LANES, rows), 0)
        step_o = lax.broadcasted_iota(jnp.int32, (rows, LANES), 0) % s_len
        lane_o = lax.broadcasted_iota(jnp.int32, (rows, LANES), 1)
        pos_row = past + lax.broadcasted_iota(jnp.int32, (1, LANES), 1) % S_PAD
        pg = jnp.zeros((LANES, ncp), F32)
        for g in range(NSA_GROUPS):
            kc = kc_ref[pl.ds(g, ncp, stride=NSA_GROUPS), :].astype(BF16)
            vc = vc_ref[pl.ds(g, ncp, stride=NSA_GROUPS), :].astype(BF16)
            p = _cmp_probs(_dot_nt(qg[g], kc), pos, n_cmp)
            oc_ref[g] = _dot(p.astype(BF16), vc)
            fold = (step_i + g * S_PAD == lane_i).astype(BF16)
            pg = pg + _dot_split(fold, p)
            m_ref[g] = jnp.full((rows, 1), NEG, F32)
            l_ref[g] = jnp.zeros((rows, 1), F32)
            acc_ref[g] = jnp.zeros((rows, NSA_DK), F32)
        sel_t = _select_blocks_t(_block_scores_t(pg, blk_rows), pos_row, n_blk).astype(BF16)
        for g in range(NSA_GROUPS):
            unfold = (step_o + g * S_PAD == lane_o).astype(BF16)
            sel_ref[g] = _dot_nt(sel_t, unfold)

    n_keys = pages * PAGE_SIZE
    blocks_per_step = n_keys // SEL_BLOCK
    expand = _block_expand(blocks_per_step, n_keys)
    key = j * n_keys + lax.broadcasted_iota(jnp.int32, (1, n_keys), 1)
    blk0 = pl.multiple_of(j * blocks_per_step, blocks_per_step)
    for g in range(NSA_GROUPS):
        k_bf = jnp.concatenate([kpages[p][pl.ds(g, PAGE_SIZE, stride=NSA_GROUPS), :] for p in range(pages)],
                               axis=0).astype(BF16)
        v_bf = jnp.concatenate([vpages[p][pl.ds(g, PAGE_SIZE, stride=NSA_GROUPS), :] for p in range(pages)],
                               axis=0).astype(BF16)
        chosen = _dot_tn(sel_ref[g, pl.ds(blk0, blocks_per_step), :].astype(BF16), expand) > 0.5
        m, l, acc = _online_update((m_ref[g], l_ref[g], acc_ref[g]), _dot_nt(qg[g], k_bf),
                                   chosen & (key <= pos), v_bf)
        m_ref[g] = m
        l_ref[g] = l
        acc_ref[g] = acc

    @pl.when(j == n_j - 1)
    def _():
        gates = jax.nn.sigmoid(gate_ref[...])
        new_i = lax.broadcasted_iota(jnp.int32, (1, S_PAD), 1)
        new_key = past + new_i
        new_blk = past // SEL_BLOCK
        first_row = (lax.broadcasted_iota(jnp.int32, (8, S_PAD), 0) == 0).astype(BF16)
        dist = pos - (past - win_rows + lax.broadcasted_iota(jnp.int32, (1, win_rows), 1))
        w_mask = (dist >= 0) & (dist <= WINDOW)
        n_mask = (new_key <= pos) & (pos - new_key <= WINDOW) & (new_i < s_len)
        for g in range(NSA_GROUPS):
            chosen_new = _dot_tn(sel_ref[g, new_blk:new_blk + 8, :].astype(BF16), first_row) > 0.5
            mask = chosen_new & (new_key <= pos) & (new_i < s_len)
            _, l, acc = _online_update((m_ref[g], l_ref[g], acc_ref[g]),
                                       _dot_nt(qg[g], kn_ref[g].astype(BF16)), mask, vn_ref[g].astype(BF16))
            o_sel = acc / l
            kw = kw_ref[pl.ds(g, win_rows, stride=NSA_GROUPS), :].astype(BF16)
            vw = vw_ref[pl.ds(g, win_rows, stride=NSA_GROUPS), :].astype(BF16)
            s_old = jnp.where(w_mask, _dot_nt(qg[g], kw), NEG)
            s_new = jnp.where(n_mask, _dot_nt(qg[g], kwn_ref[g].astype(BF16)), NEG)
            m_w = jnp.maximum(jnp.max(s_old, axis=-1, keepdims=True), jnp.max(s_new, axis=-1, keepdims=True))
            e_old = jnp.where(w_mask, jnp.exp(s_old - m_w), 0.0)
            e_new = jnp.where(n_mask, jnp.exp(s_new - m_w), 0.0)
            l_w = jnp.sum(e_old, axis=-1, keepdims=True) + jnp.sum(e_new, axis=-1, keepdims=True)
            o_win = (_dot(e_old.astype(BF16), vw) + _dot(e_new.astype(BF16), vwn_ref[g].astype(BF16))) / l_w
            gt = gates[g]
            o_ref[g] = gt[:, 0:1] * oc_ref[g] + gt[:, 1:2] * o_sel + gt[:, 2:3] * o_win


def nsa_sample(q, kc, vc, k_new, v_new, kw_old, vw_old, kw_new, vw_new, gate, pool_k, pool_v, page_table, s_len,
               pages):
    bd, n_pages = page_table.shape
    past = n_pages * PAGE_SIZE
    rows = NSA_HPG * s_len
    n_cmp = past // CMP_STRIDE - 1
    n_blk = max(-(-(past + s_len) // SEL_BLOCK), N_SEL)
    assert s_len <= S_PAD and s_len < CMP_STRIDE and n_pages % pages == 0
    assert (past // SEL_BLOCK) % 8 == 0 and (pages * PAGE_SIZE // SEL_BLOCK) % 8 == 0
    blk_rows = -(-(past // SEL_BLOCK + 8) // LANES) * LANES
    ncp = kc.shape[1] // NSA_GROUPS
    win_rows = kw_old.shape[1] // NSA_GROUPS
    new_spec = pl.BlockSpec((None, NSA_GROUPS, S_PAD, NSA_DK), lambda b, j, pt: (b, 0, 0, 0))

    def per_b(shape):
        return pl.BlockSpec((None,) + shape, lambda b, j, pt: (b,) + (0,) * len(shape))

    def page_spec(i):
        return pl.BlockSpec((None, PAGE_SIZE * NSA_GROUPS, NSA_DK), lambda b, j, pt: (pt[b, j * pages + i], 0, 0))

    grid_spec = pltpu.PrefetchScalarGridSpec(
        num_scalar_prefetch=1,
        grid=(bd, n_pages // pages),
        in_specs=[per_b((NSA_GROUPS, rows, NSA_DK)), per_b((ncp * NSA_GROUPS, NSA_DK)), per_b((ncp * NSA_GROUPS, NSA_DK)),
                  new_spec, new_spec,
                  per_b((win_rows * NSA_GROUPS, NSA_DK)), per_b((win_rows * NSA_GROUPS, NSA_DK)),
                  new_spec, new_spec,
                  per_b((NSA_GROUPS, rows, LANES))]
        + [page_spec(i) for i in range(pages)] * 2,
        out_specs=per_b((NSA_GROUPS, rows, NSA_DK)),
        scratch_shapes=[pltpu.VMEM((NSA_GROUPS, blk_rows, rows), F32),
                        pltpu.VMEM((NSA_GROUPS, rows, 1), F32),
                        pltpu.VMEM((NSA_GROUPS, rows, 1), F32),
                        pltpu.VMEM((NSA_GROUPS, rows, NSA_DK), F32),
                        pltpu.VMEM((NSA_GROUPS, rows, NSA_DK), F32)],
    )
    return pl.pallas_call(
        functools.partial(_nsa_sample_kernel, pages=pages, past=past, s_len=s_len, n_cmp=n_cmp, ncp=ncp, n_blk=n_blk,
                          blk_rows=blk_rows, win_rows=win_rows),
        out_shape=jax.ShapeDtypeStruct((bd, NSA_GROUPS, rows, NSA_DK), F32),
        grid_spec=grid_spec,
        compiler_params=_params(("parallel", "arbitrary")),
        name="nsa_sample",
    )(page_table, q, kc, vc, k_new, v_new, kw_old, vw_old, kw_new, vw_new, gate,
      *([pool_k] * pages), *([pool_v] * pages))


def _gla_kernel(q_ref, k_ref, v_ref, gr_ref, lr_ref, wlr_ref, blr_ref, ng_ref, s0_ref, o_ref, sT_out_ref,
                st_ref, *, chunk, t_valid):
    c = pl.program_id(2)

    @pl.when(c == 0)
    def _():
        st_ref[...] = s0_ref[...].T

    row = lax.broadcasted_iota(jnp.int32, (chunk, 1), 0)
    x = _dot(lr_ref[...].astype(BF16), wlr_ref[...]) + blr_ref[...]
    log_a = jnp.where(row < t_valid, jax.nn.log_sigmoid(x) / GLA_TAU, 0.0)
    tri = (lax.broadcasted_iota(jnp.int32, (chunk, chunk), 1) <= lax.broadcasted_iota(jnp.int32, (chunk, chunk), 0))
    b = _dot_split(tri.astype(BF16), log_a)
    qs = q_ref[...] * (GLA_DK ** -0.5)
    k = k_ref[...]
    a_t = jnp.zeros((chunk, LANES), F32)
    for t in range(min(chunk, t_valid)):
        n_s = -(-(t + 1) // 8) * 8
        seen = lax.broadcasted_iota(jnp.int32, (n_s, GLA_DK), 0) <= t
        w = jnp.where(seen, jnp.exp(jnp.where(seen, b[t:t + 1] - b[0:n_s], 0.0)), 0.0)
        col = jnp.sum(k[0:n_s] * w * qs[t:t + 1], axis=-1, keepdims=True)
        filled = jnp.where(lax.broadcasted_iota(jnp.int32, (n_s, LANES), 1) == t, col, a_t[0:n_s])
        a_t = filled if n_s == chunk else jnp.concatenate([filled, a_t[n_s:]], axis=0)
    v = v_ref[...]
    v_bf = v.astype(BF16)
    st = st_ref[...]
    o = _dot_tn(a_t[:, 0:chunk].astype(BF16), v_bf) + _dot_nt((qs * jnp.exp(b)).astype(BF16), st.astype(BF16))
    b_last = b[chunk - 1:chunk, :]
    k_dec = (k * jnp.exp(b_last - b)).astype(BF16)
    st_new = st * jnp.exp(b_last) + _dot_tn(v_bf, k_dec)
    st_ref[...] = st_new
    o = _rms(o, ng_ref[...])
    gr = gr_ref[...]
    o_ref[...] = (o * (gr * jax.nn.sigmoid(gr))).astype(o_ref.dtype)

    @pl.when(c == pl.num_programs(2) - 1)
    def _():
        sT_out_ref[...] = st_new.T


def gla(z, wlr, blr, norm_g, s0, b_sz, t_len, chunk, t_valid):
    assert chunk <= LANES and chunk % 8 == 0 and t_len % chunk == 0
    nck = t_len // chunk

    def seg(col0, width):
        return pl.BlockSpec((chunk, width), lambda b, h, c: (b * nck + c, col0 // width + h))

    return pl.pallas_call(
        functools.partial(_gla_kernel, chunk=chunk, t_valid=t_valid),
        out_shape=(jax.ShapeDtypeStruct((b_sz * t_len, GLA_HEADS * GLA_DV), BF16),
                   jax.ShapeDtypeStruct((b_sz, GLA_HEADS, GLA_DK, GLA_DV), F32)),
        grid=(b_sz, GLA_HEADS, nck),
        in_specs=[seg(C_GQ, GLA_DK), seg(C_GK, GLA_DK), seg(C_GV, GLA_DV), seg(C_GR, GLA_DV),
                  pl.BlockSpec((chunk, LANES), lambda b, h, c: (b * nck + c, C_TAIL // LANES)),
                  pl.BlockSpec((None, LANES, GLA_DK), lambda b, h, c: (h, 0, 0)),
                  pl.BlockSpec((None, 1, GLA_DK), lambda b, h, c: (h, 0, 0)),
                  pl.BlockSpec((1, GLA_DV), lambda b, h, c: (0, 0)),
                  pl.BlockSpec((None, None, GLA_DK, GLA_DV), lambda b, h, c: (b, h, 0, 0))],
        out_specs=(pl.BlockSpec((chunk, GLA_DV), lambda b, h, c: (b * nck + c, h)),
                   pl.BlockSpec((None, None, GLA_DK, GLA_DV), lambda b, h, c: (b, h, 0, 0))),
        scratch_shapes=[pltpu.VMEM((GLA_DV, GLA_DK), F32)],
        compiler_params=_params(("parallel", "parallel", "arbitrary")),
        name="gla",
    )(z, z, z, z, z, wlr, blr, norm_g.reshape(1, GLA_DV), s0)


def _merge_kernel(on_ref, og_ref, wn_ref, wg_ref, ma_ref, mb_ref, y_ref):
    y = (jax.nn.sigmoid(ma_ref[...]) * _dot(on_ref[...], wn_ref[...])
         + jax.nn.sigmoid(mb_ref[...]) * _dot(og_ref[...], wg_ref[...]))
    y_ref[...] = y.astype(y_ref.dtype)


def merge(o_nsa, o_gla, w_nsa, w_gla, z, tm, tn):
    n = o_nsa.shape[0]
    return pl.pallas_call(
        _merge_kernel,
        out_shape=jax.ShapeDtypeStruct((n, D_MODEL), BF16),
        grid=(n // tm, D_MODEL // tn),
        in_specs=[pl.BlockSpec((tm, o_nsa.shape[1]), lambda i, j: (i, 0)),
                  pl.BlockSpec((tm, o_gla.shape[1]), lambda i, j: (i, 0)),
                  pl.BlockSpec((w_nsa.shape[0], tn), lambda i, j: (0, j)),
                  pl.BlockSpec((w_gla.shape[0], tn), lambda i, j: (0, j)),
                  pl.BlockSpec((tm, tn), lambda i, j: (i, C_MA // tn + j)),
                  pl.BlockSpec((tm, tn), lambda i, j: (i, C_MB // tn + j))],
        out_specs=pl.BlockSpec((tm, tn), lambda i, j: (i, j)),
        compiler_params=_params(("parallel", "arbitrary")),
        name="merge",
    )(o_nsa, o_gla, w_nsa, w_gla, z, z)


def _out_proj_kernel(y_ref, w_ref, x_ref, o_ref):
    o_ref[...] = x_ref[...] + _dot(y_ref[...], w_ref[...])


def out_proj(y, w_out, x, tm, tn):
    n = y.shape[0]
    return pl.pallas_call(
        _out_proj_kernel,
        out_shape=jax.ShapeDtypeStruct((n, D_MODEL), F32),
        grid=(n // tm, D_MODEL // tn),
        in_specs=[pl.BlockSpec((tm, D_MODEL), lambda i, j: (i, 0)),
                  pl.BlockSpec((D_MODEL, tn), lambda i, j: (0, j)),
                  pl.BlockSpec((tm, tn), lambda i, j: (i, j))],
        out_specs=pl.BlockSpec((tm, tn), lambda i, j: (i, j)),
        compiler_params=_params(("parallel", "arbitrary")),
        name="out_proj",
    )(y, w_out, x)


def _top_values(s, count):
    vals = []
    for _ in range(count):
        m = jnp.max(s, axis=0, keepdims=True)
        s = jnp.where(s == m, REMOVED, s)
        vals.append(m)
    return vals


def _peer_scores_kernel(q_ref, k1_ref, k2_ref, s1_ref, s2_ref, e1_ref, e2_ref, tau_ref):
    taus = []
    for h in range(PEER_HEADS):
        q1 = q_ref[:, h * 2 * PEER_HALF:h * 2 * PEER_HALF + PEER_HALF].astype(BF16)
        q2 = q_ref[:, h * 2 * PEER_HALF + PEER_HALF:(h + 1) * 2 * PEER_HALF].astype(BF16)
        s1 = _dot_nt(k1_ref[h], q1)
        s2 = _dot_nt(k2_ref[h], q2)
        v1 = _top_values(s1, PEER_TOPK)
        v2 = _top_values(s2, PEER_TOPK)
        v2_all = jnp.concatenate(v2, axis=0)
        cand = jnp.concatenate([v1[i] + v2_all for i in range(PEER_TOPK)], axis=0)
        top = _top_values(cand, PEER_TOPK)
        z = sum(jnp.exp(t - top[0]) for t in top)
        s1_ref[h] = s1
        s2_ref[h] = s2
        e1_ref[h] = jnp.exp(s1 - v1[0])
        e2_ref[h] = jnp.exp(s2 - v2[0]) / z
        taus.append(top[PEER_TOPK - 1])
    tau_ref[...] = jnp.concatenate(taus, axis=0)


def peer_scores(qp, keys1, keys2, tn):
    n = qp.shape[0]
    big = jax.ShapeDtypeStruct((PEER_HEADS, PEER_NKEYS, n), F32)
    big_spec = pl.BlockSpec((PEER_HEADS, PEER_NKEYS, tn), lambda i: (0, 0, i))
    key_spec = pl.BlockSpec((PEER_HEADS, PEER_NKEYS, PEER_HALF), lambda i: (0, 0, 0))
    return pl.pallas_call(
        _peer_scores_kernel,
        out_shape=(big, big, big, big, jax.ShapeDtypeStruct((PEER_HEADS, n), F32)),
        grid=(n // tn,),
        in_specs=[pl.BlockSpec((tn, qp.shape[1]), lambda i: (i, 0)), key_spec, key_spec],
        out_specs=(big_spec, big_spec, big_spec, big_spec, pl.BlockSpec((PEER_HEADS, tn), lambda i: (0, i))),
        compiler_params=_params(("parallel",)),
        name="peer_scores",
    )(qp, keys1, keys2)


PEER_I1_PER_TILE = 8


def _peer_dense_kernel(x_ref, g2_ref, gf_ref, u_ref, v_ref, s1_ref, e1_ref, s2_ref, e2_ref, tau_ref, o_ref,
                       h_ref, acc_ref):
    e = pl.program_id(1)

    @pl.when(e == 0)
    def _():
        h_ref[...] = _rms(x_ref[...], g2_ref[...]).astype(BF16)
        acc_ref[...] = jnp.zeros_like(acc_ref)

    pre = _dot_nt(u_ref[...], h_ref[...])
    n_tok = pre.shape[1]
    tile = min(LANES, n_tok)
    parts = []
    for c in range(PEER_I1_PER_TILE):
        cols = []
        for t0 in range(0, n_tok, tile):
            tok = slice(t0, t0 + tile)
            w = jnp.zeros((PEER_NKEYS, tile), F32)
            for h in range(PEER_HEADS):
                keep = s1_ref[h, c:c + 1, tok] + s2_ref[h, :, tok] >= tau_ref[h:h + 1, tok]
                w = w + jnp.where(keep, e1_ref[h, c:c + 1, tok] * e2_ref[h, :, tok], 0.0)
            cols.append(w * _gelu_tanh(pre[c * PEER_NKEYS:(c + 1) * PEER_NKEYS, tok]))
        parts.append(jnp.concatenate(cols, axis=1).astype(BF16))
    acc_ref[...] += _dot_tn(jnp.concatenate(parts, axis=0), v_ref[...])

    @pl.when(e == pl.num_programs(1) - 1)
    def _():
        o_ref[...] = _rms(x_ref[...] + acc_ref[...], gf_ref[...])


def peer_dense(x, g2, gf, u_bf, v_bf, s1, s2, e1, e2, tau, tn):
    n = x.shape[0]
    te = PEER_I1_PER_TILE * PEER_NKEYS
    n_exp = u_bf.shape[0]
    sub = pl.BlockSpec((PEER_HEADS, PEER_I1_PER_TILE, tn), lambda i, e: (0, e, i))
    full = pl.BlockSpec((PEER_HEADS, PEER_NKEYS, tn), lambda i, e: (0, 0, i))
    vec = pl.BlockSpec((1, D_MODEL), lambda i, e: (0, 0))
    return pl.pallas_call(
        _peer_dense_kernel,
        out_shape=jax.ShapeDtypeStruct((n, D_MODEL), F32),
        grid=(n // tn, n_exp // te),
        in_specs=[pl.BlockSpec((tn, D_MODEL), lambda i, e: (i, 0)), vec, vec,
                  pl.BlockSpec((te, D_MODEL), lambda i, e: (e, 0)),
                  pl.BlockSpec((te, D_MODEL), lambda i, e: (e, 0)),
                  sub, sub, full, full,
                  pl.BlockSpec((PEER_HEADS, tn), lambda i, e: (0, i))],
        out_specs=pl.BlockSpec((tn, D_MODEL), lambda i, e: (i, 0)),
        scratch_shapes=[pltpu.VMEM((tn, D_MODEL), BF16), pltpu.VMEM((tn, D_MODEL), F32)],
        compiler_params=_params(("parallel", "arbitrary")),
        name="peer_dense",
    )(x, g2.reshape(1, D_MODEL), gf.reshape(1, D_MODEL), u_bf, v_bf, s1, e1, s2, e2, tau)


def _lanes_left(x, s):
    return x if s == 0 else jnp.concatenate([x[:, s:], x[:, :s]], axis=1)


def _pack_w_in_kernel(ia_ref, ib_ref, kind_ref, a_ref, b_ref, o_ref, *, shifts, gate_lane0, per_group, glr_lane0):
    kind = kind_ref[pl.program_id(0)]
    for k, s in enumerate(shifts):
        @pl.when(kind == k)
        def _(s=s):
            src = a_ref[...] if s == 0 else jnp.concatenate([a_ref[:, s:], b_ref[:, :s]], axis=1)
            o_ref[...] = src.astype(BF16)
    lane = lax.broadcasted_iota(jnp.int32, (1, LANES), 1)
    for g in range(NSA_GROUPS):
        @pl.when(kind == len(shifts) + g)
        def _(g=g):
            out = jnp.where(lane < per_group, _lanes_left(a_ref[...], gate_lane0 + g * per_group), 0.0)
            if g == 0:
                low_rank = _lanes_left(b_ref[...], glr_lane0 - GLR_LANE)
                out = out + jnp.where((lane >= GLR_LANE) & (lane < GLR_LANE + GLA_RANK), low_rank, 0.0)
            o_ref[...] = out.astype(BF16)


def _pack_w_in(w_in):
    offs = np.concatenate([[0], np.cumsum(IN_SPLITS)]).astype(np.int64)
    plain = (0, 1, 2, 3, 4, 5, 6, 8, 9, 10, 11, 13, 14)
    src_start = np.concatenate([np.arange(offs[i], offs[i + 1], LANES) for i in plain])
    assert all((offs[i + 1] - offs[i]) % LANES == 0 for i in plain) and len(src_start) * LANES == C_TAIL
    shifts = tuple(sorted(set(int(s) for s in src_start % LANES)))
    gate_col, glr_col = int(offs[7]), int(offs[12])
    per_group = 3 * NSA_HPG
    assert gate_col % LANES + NSA_GROUPS * per_group <= LANES and glr_col % LANES >= GLR_LANE
    n_tail = (Z_COLS - C_TAIL) // LANES
    ia = np.concatenate([src_start // LANES, np.full(n_tail, gate_col // LANES)])
    ib = np.concatenate([src_start // LANES + 1, np.full(n_tail, glr_col // LANES)])
    kind = np.concatenate([[shifts.index(int(s)) for s in src_start % LANES],
                           len(shifts) + np.minimum(np.arange(n_tail), NSA_GROUPS - 1)])
    d = w_in.shape[0]
    grid_spec = pltpu.PrefetchScalarGridSpec(
        num_scalar_prefetch=3,
        grid=(Z_COLS // LANES,),
        in_specs=[pl.BlockSpec((d, LANES), lambda c, ia, ib, kind: (0, ia[c])),
                  pl.BlockSpec((d, LANES), lambda c, ia, ib, kind: (0, ib[c]))],
        out_specs=pl.BlockSpec((d, LANES), lambda c, ia, ib, kind: (0, c)),
    )
    return pl.pallas_call(
        functools.partial(_pack_w_in_kernel, shifts=shifts, gate_lane0=gate_col % LANES, per_group=per_group,
                          glr_lane0=glr_col % LANES),
        out_shape=jax.ShapeDtypeStruct((d, Z_COLS), BF16),
        grid_spec=grid_spec,
        compiler_params=_params(("arbitrary",)),
        name="pack_w_in",
    )(jnp.asarray(ia, jnp.int32), jnp.asarray(ib, jnp.int32), jnp.asarray(kind, jnp.int32), w_in, w_in)


def _pack_cmp(pe, w1, w2):
    w1cat = jnp.concatenate([w1[:CMP_STRIDE].reshape(CMP_STRIDE * NSA_DK, CMP_HIDDEN),
                             w1[CMP_STRIDE:].reshape(CMP_STRIDE * NSA_DK, CMP_HIDDEN)], axis=1).astype(BF16)
    return (w1cat, pe.reshape(1, CMP_LEN * NSA_DK).astype(BF16),
            w1.reshape(CMP_LEN * NSA_DK, CMP_HIDDEN).astype(BF16), w2.astype(BF16))


def _row_tile(n, cap):
    t = min(n, cap)
    while n % t:
        t //= 2
    return t


def _channel_tail(x2d, z, o_nsa, o_gla, wts):
    n = x2d.shape[0]
    tm = _row_tile(n, 512)
    y = merge(o_nsa, o_gla, wts["w_nsa"], wts["w_gla"], z, tm, 512)
    x1 = out_proj(y, wts["w_out"], x2d, tm, 512)
    qp = norm_matmul(x1, wts["norm2_g"], wts["w_q"], tm, 512)
    tn = _row_tile(n, 256)
    s1, s2, e1, e2, tau = peer_scores(qp, wts["keys1"], wts["keys2"], tn)
    return peer_dense(x1, wts["norm2_g"], wts["norm_f_g"], wts["u"], wts["v"], s1, s2, e1, e2, tau, _row_tile(n, 512))


def kernel(x_prompt, x_sample, cache_k_cmp, cache_v_cmp, cache_k_slc, cache_v_slc, cache_k_win, cache_v_win, state_gla, page_table, norm1_g, w_in, cmp_pe_k, cmp_w1_k, cmp_w2_k, cmp_pe_v, cmp_w1_v, cmp_w2_v, gla_w_lr2, gla_b_lr, gla_norm_g, w_nsa_proj, w_gla_proj, w_out, norm2_g, peer_w_q, peer_keys1, peer_keys2, peer_u, peer_v, norm_f_g):
    b_sz, t_len, _ = x_prompt.shape
    bd, s_len, _ = x_sample.shape
    n_pool = cache_k_cmp.shape[0]
    wb = cache_k_win.shape[1]

    w_pack = _pack_w_in(w_in)
    cmp_k = _pack_cmp(cmp_pe_k, cmp_w1_k, cmp_w2_k)
    cmp_v = _pack_cmp(cmp_pe_v, cmp_w1_v, cmp_w2_v)
    cmp_kv = [jnp.stack([a, b]) for a, b in zip(cmp_k, cmp_v)]
    wlr = jnp.zeros((LANES, GLA_HEADS * GLA_DK), F32).at[GLR_LANE:GLR_LANE + GLA_RANK].set(gla_w_lr2)
    wlr = wlr.reshape(LANES, GLA_HEADS, GLA_DK).transpose(1, 0, 2).astype(BF16)
    blr = gla_b_lr.reshape(GLA_HEADS, 1, GLA_DK)
    wts = dict(w_nsa=w_nsa_proj.astype(BF16), w_gla=w_gla_proj.astype(BF16), w_out=w_out.astype(BF16),
               w_q=peer_w_q.astype(BF16), norm2_g=norm2_g, norm_f_g=norm_f_g,
               keys1=peer_keys1.astype(BF16), keys2=peer_keys2.astype(BF16),
               u=peer_u.astype(BF16), v=peer_v.astype(BF16))

    n_p = b_sz * t_len
    xp = x_prompt.reshape(n_p, D_MODEL)
    z_p = norm_matmul(xp, norm1_g, w_pack, _row_tile(n_p, 1024), GD)
    heads_p = split_heads(z_p, _row_tile(n_p, 1024))
    fs_p = half_proj_dense(z_p, cmp_kv[0], b_sz, t_len)
    kcvc_p = compress_finish(fs_p, cmp_kv[1], cmp_kv[2], cmp_kv[3], 1)
    o_nsa_p = nsa_prompt(z_p, kcvc_p, b_sz, t_len)
    s0 = jnp.zeros((b_sz, GLA_HEADS, GLA_DK, GLA_DV), F32)
    o_gla_p, gla_state_p = gla(z_p, wlr, blr, gla_norm_g, s0, b_sz, t_len, GLA_CHUNK, GLA_CHUNK)
    y_prompt = _channel_tail(xp, z_p, o_nsa_p, o_gla_p, wts).reshape(b_sz, t_len, D_MODEL)
    kcr_p, vcr_p, ksr_p, vsr_p, kwr_p, vwr_p = (a.reshape(b_sz, t_len, NSA_GROUPS, NSA_DK) for a in heads_p)
    wl = min(WINDOW, t_len)
    k_win_p = kwr_p[:, t_len - wl:]
    v_win_p = vwr_p[:, t_len - wl:]

    n_s = bd * s_len
    xs = x_sample.reshape(n_s, D_MODEL)
    z_s = norm_matmul(xs, norm1_g, w_pack, _row_tile(n_s, 1024), GD)
    heads_s = split_heads(z_s, _row_tile(n_s, 1024))
    kcr_s, vcr_s, ksr_s, vsr_s, kwr_s, vwr_s = (a.reshape(bd, s_len, NSA_GROUPS, NSA_DK) for a in heads_s)

    pool2d = lambda p: p.reshape(n_pool, PAGE_SIZE * NSA_GROUPS, NSA_DK)
    pages = 16

    def compress_pool(pool, prm):
        fs = half_proj_paged(pool2d(pool), page_table, prm[0], pages)
        return compress_finish(fs[None, :, None], prm[1][None], prm[2][None], prm[3][None], NSA_GROUPS)[0, :, 0]

    kc_s = compress_pool(cache_k_cmp, cmp_k)
    vc_s = compress_pool(cache_v_cmp, cmp_v)

    assert wb >= s_len
    k_win_s = jnp.concatenate([cache_k_win[:, s_len:], kwr_s], axis=1)
    v_win_s = jnp.concatenate([cache_v_win[:, s_len:], vwr_s], axis=1)
    win_2d = lambda a: a.reshape(bd, wb * NSA_GROUPS, NSA_DK)

    def new_rows(a):
        return jnp.pad(a.transpose(0, 2, 1, 3), ((0, 0), (0, 0), (0, S_PAD - s_len), (0, 0)))

    rows = NSA_HPG * s_len
    q_s = z_s[:, C_Q:C_Q + NSA_HEADS * NSA_DK].reshape(bd, s_len, NSA_GROUPS, NSA_HPG, NSA_DK)
    q_s = q_s.transpose(0, 2, 3, 1, 4).reshape(bd, NSA_GROUPS, rows, NSA_DK)
    gate_s = z_s[:, C_TAIL:C_TAIL + 512].reshape(bd, s_len, NSA_GROUPS, LANES)[..., :3 * NSA_HPG]
    gate_s = gate_s.reshape(bd, s_len, NSA_GROUPS, NSA_HPG, 3).transpose(0, 2, 3, 1, 4).reshape(bd, NSA_GROUPS, rows, 3)
    gate_s = jnp.pad(gate_s, ((0, 0), (0, 0), (0, 0), (0, LANES - 3)))
    o_s = nsa_sample(q_s, kc_s, vc_s, new_rows(ksr_s), new_rows(vsr_s), win_2d(cache_k_win), win_2d(cache_v_win),
                     new_rows(kwr_s), new_rows(vwr_s), gate_s, pool2d(cache_k_slc), pool2d(cache_v_slc), page_table,
                     s_len, pages)
    o_nsa_s = o_s.reshape(bd, NSA_GROUPS, NSA_HPG, s_len, NSA_DK).transpose(0, 3, 1, 2, 4)
    o_nsa_s = o_nsa_s.reshape(n_s, NSA_HEADS * NSA_DK).astype(BF16)

    z_s_pad = jnp.pad(z_s.reshape(bd, s_len, Z_COLS), ((0, 0), (0, S_PAD - s_len), (0, 0))).reshape(bd * S_PAD, Z_COLS)
    o_gla_s, gla_state_s = gla(z_s_pad, wlr, blr, gla_norm_g, state_gla, bd, S_PAD, S_PAD, s_len)
    o_gla_s = o_gla_s.reshape(bd, S_PAD, GLA_HEADS * GLA_DV)[:, :s_len].reshape(n_s, GLA_HEADS * GLA_DV)
    y_sample = _channel_tail(xs, z_s, o_nsa_s, o_gla_s, wts).reshape(bd, s_len, D_MODEL)

    return (y_prompt, y_sample, kcr_p, vcr_p, ksr_p, vsr_p, k_win_p, v_win_p, gla_state_p,
            kcr_s, vcr_s, ksr_s, vsr_s, k_win_s, v_win_s, gla_state_s)
```

```python
import functools

import numpy as np
import jax
import jax.numpy as jnp
from jax import lax
from jax.experimental import pallas as pl
from jax.experimental.pallas import tpu as pltpu

F32 = jnp.float32
BF16 = jnp.bfloat16

D_MODEL = 2048
PAGE_SIZE = 128
NSA_HEADS = 16
NSA_GROUPS = 4
NSA_HPG = NSA_HEADS // NSA_GROUPS
NSA_DK = 128
NSA_SCALE = NSA_DK ** -0.5
CMP_LEN = 32
CMP_STRIDE = 16
CMP_HIDDEN = 256
SEL_BLOCK = 64
N_SEL = 16
WINDOW = 512
QUERY_BLOCK = 256
GLA_HEADS = 4
GLA_DK = 256
GLA_DV = 512
GLA_RANK = 16
GLA_TAU = 16.0
GLA_CHUNK = 64
PEER_HEADS = 8
PEER_NKEYS = 128
PEER_HALF = 128
PEER_TOPK = 16
NORM_EPS = 1e-6
NEG = -1e30
BIG = 1e30
PAD_SCORE = -3e38
REMOVED = -float("inf")

IN_SPLITS = (NSA_HEADS * NSA_DK,) + (NSA_GROUPS * NSA_DK,) * 6 + (
    3 * NSA_HEADS, GLA_HEADS * GLA_DK, GLA_HEADS * GLA_DK, GLA_HEADS * GLA_DV,
    GLA_HEADS * GLA_DV, GLA_RANK, D_MODEL, D_MODEL)

GD = NSA_GROUPS * NSA_DK
C_Q = 0
C_KC = C_Q + NSA_HEADS * NSA_DK
C_VC = C_KC + GD
C_KS = C_VC + GD
C_VS = C_KS + GD
C_KW = C_VS + GD
C_VW = C_KW + GD
C_GQ = C_VW + GD
C_GK = C_GQ + GLA_HEADS * GLA_DK
C_GV = C_GK + GLA_HEADS * GLA_DK
C_GR = C_GV + GLA_HEADS * GLA_DV
C_MA = C_GR + GLA_HEADS * GLA_DV
C_MB = C_MA + D_MODEL
C_TAIL = C_MB + D_MODEL
Z_COLS = C_TAIL + 512
GLR_LANE = 16
LANES = 128

VMEM_LIMIT = 56 * 1024 * 1024


def _params(sem):
    return pltpu.CompilerParams(dimension_semantics=sem, vmem_limit_bytes=VMEM_LIMIT)


def _dot(a, b):
    return jnp.dot(a, b, preferred_element_type=F32)


def _dot_nt(a, b):
    return lax.dot_general(a, b, (((1,), (1,)), ((), ())), preferred_element_type=F32)


def _dot_tn(a, b):
    return lax.dot_general(a, b, (((0,), (0,)), ((), ())), preferred_element_type=F32)


def _rms(x, g):
    return x * lax.rsqrt(jnp.mean(x * x, axis=-1, keepdims=True) + NORM_EPS) * g


GELU_C0 = 2.0 * (2.0 / np.pi) ** 0.5
GELU_C1 = GELU_C0 * 0.044715


def _gelu_tanh(x):
    return x / (1.0 + jnp.exp(-(x * (GELU_C0 + GELU_C1 * (x * x)))))


def _bf16_pieces(x):
    hi = x.astype(BF16)
    rest = x - hi.astype(F32)
    mid = rest.astype(BF16)
    lo = (rest - mid.astype(F32)).astype(BF16)
    return hi, mid, lo


def _dot_nt_split(small_ints_bf16, x):
    return sum(_dot_nt(small_ints_bf16, piece) for piece in _bf16_pieces(x))


def _dot_split(small_ints_bf16, x):
    return sum(_dot(small_ints_bf16, piece) for piece in _bf16_pieces(x))


def _norm_matmul_kernel(x_ref, g_ref, w_ref, o_ref, h_ref, *, w_rows_are_outputs):
    @pl.when(pl.program_id(1) == 0)
    def _():
        h_ref[...] = _rms(x_ref[...], g_ref[...]).astype(BF16)

    o_ref[...] = (_dot_nt if w_rows_are_outputs else _dot)(h_ref[...], w_ref[...])


def norm_matmul(x, g, w, tm, tn, w_rows_are_outputs=False):
    n, d = x.shape
    cols = w.shape[0] if w_rows_are_outputs else w.shape[1]
    w_spec = (pl.BlockSpec((tn, d), lambda i, j: (j, 0)) if w_rows_are_outputs
              else pl.BlockSpec((d, tn), lambda i, j: (0, j)))
    return pl.pallas_call(
        functools.partial(_norm_matmul_kernel, w_rows_are_outputs=w_rows_are_outputs),
        out_shape=jax.ShapeDtypeStruct((n, cols), F32),
        grid=(n // tm, cols // tn),
        in_specs=[pl.BlockSpec((tm, d), lambda i, j: (i, 0)),
                  pl.BlockSpec((1, d), lambda i, j: (0, 0)),
                  w_spec],
        out_specs=pl.BlockSpec((tm, tn), lambda i, j: (i, j)),
        scratch_shapes=[pltpu.VMEM((tm, d), BF16)],
        compiler_params=_params(("parallel", "arbitrary")),
        name="norm_matmul",
    )(x, g.reshape(1, d), w)


KV_SEGMENTS = 6


def _split_heads_kernel(z_ref, *head_refs):
    k = pl.program_id(1)
    for seg in range(KV_SEGMENTS):
        @pl.when(k == seg)
        def _(seg=seg):
            for g in range(NSA_GROUPS):
                head_refs[seg][:, g, :] = z_ref[:, g * NSA_DK:(g + 1) * NSA_DK]


def split_heads(z, tm):
    n = z.shape[0]
    heads = jax.ShapeDtypeStruct((n, NSA_GROUPS, NSA_DK), F32)
    return pl.pallas_call(
        _split_heads_kernel,
        out_shape=(heads,) * KV_SEGMENTS,
        grid=(n // tm, KV_SEGMENTS),
        in_specs=[pl.BlockSpec((tm, GD), lambda i, k: (i, C_KC // GD + k))],
        out_specs=(pl.BlockSpec((tm, NSA_GROUPS, NSA_DK), lambda i, k: (i, 0, 0)),) * KV_SEGMENTS,
        compiler_params=_params(("parallel", "arbitrary")),
        name="split_heads",
    )(z)


def _half_rows(ref, first_row, n_half, row_stride, lane0):
    parts = [ref[pl.ds(first_row + r * row_stride, n_half, stride=CMP_STRIDE * row_stride), pl.ds(lane0, NSA_DK)]
             for r in range(CMP_STRIDE)]
    return jnp.concatenate(parts, axis=1)


def _half_proj_dense_kernel(x_ref, w_ref, o_ref, *, n_half):
    o_ref[...] = _dot(_half_rows(x_ref, 0, n_half, 1, 0).astype(BF16), w_ref[...])


def half_proj_dense(z, w1cat, b_sz, t_len):
    n_half = t_len // CMP_STRIDE
    return pl.pallas_call(
        functools.partial(_half_proj_dense_kernel, n_half=n_half),
        out_shape=jax.ShapeDtypeStruct((2, b_sz, NSA_GROUPS, n_half, 2 * CMP_HIDDEN), F32),
        grid=(2, b_sz, NSA_GROUPS),
        in_specs=[pl.BlockSpec((t_len, NSA_DK), lambda kv, b, g: (b, C_KC // NSA_DK + kv * NSA_GROUPS + g)),
                  pl.BlockSpec((None, CMP_STRIDE * NSA_DK, 2 * CMP_HIDDEN), lambda kv, b, g: (kv, 0, 0))],
        out_specs=pl.BlockSpec((None, None, None, n_half, 2 * CMP_HIDDEN), lambda kv, b, g: (kv, b, g, 0, 0)),
        compiler_params=_params(("parallel", "parallel", "parallel")),
        name="half_proj_dense",
    )(z, w1cat)


HALVES_PER_PAGE = PAGE_SIZE // CMP_STRIDE


def _half_proj_paged_kernel(pt_ref, *refs, pages):
    page_refs, w_ref, o_ref = refs[:pages], refs[pages], refs[pages + 1]
    tile = 2 * NSA_GROUPS
    rows_per_half = CMP_STRIDE * NSA_GROUPS
    low = lax.broadcasted_iota(jnp.int32, (tile, NSA_DK), 0) < NSA_GROUPS
    acc = None
    for r2 in range(CMP_STRIDE // 2):
        even, odd = [], []
        for p in range(pages):
            for n in range(0, HALVES_PER_PAGE, 2):
                a = page_refs[p][pl.ds(n * rows_per_half + tile * r2, tile), :]
                b = page_refs[p][pl.ds((n + 1) * rows_per_half + tile * r2, tile), :]
                even.append(jnp.where(low, a, pltpu.roll(b, NSA_GROUPS, 0)))
                odd.append(jnp.where(low, pltpu.roll(a, NSA_GROUPS, 0), b))
        lhs = jnp.concatenate([jnp.concatenate(even, axis=0), jnp.concatenate(odd, axis=0)], axis=1).astype(BF16)
        part = _dot(lhs, w_ref[pl.ds(r2 * 2 * NSA_DK, 2 * NSA_DK), :])
        acc = part if acc is None else acc + part
    o_ref[...] = acc


def half_proj_paged(pool2d, page_table, w1cat_one, pages):
    bd, n_pages = page_table.shape
    n_half = n_pages * HALVES_PER_PAGE
    rows = pages * HALVES_PER_PAGE * NSA_GROUPS

    def page_spec(i):
        return pl.BlockSpec((None, PAGE_SIZE * NSA_GROUPS, NSA_DK),
                            lambda b, j, pt: (pt[b, j * pages + i], 0, 0))

    grid_spec = pltpu.PrefetchScalarGridSpec(
        num_scalar_prefetch=1,
        grid=(bd, n_pages // pages),
        in_specs=[page_spec(i) for i in range(pages)]
        + [pl.BlockSpec((CMP_STRIDE * NSA_DK, 2 * CMP_HIDDEN), lambda b, j, pt: (0, 0))],
        out_specs=pl.BlockSpec((None, rows, 2 * CMP_HIDDEN), lambda b, j, pt: (b, j, 0)),
    )
    return pl.pallas_call(
        functools.partial(_half_proj_paged_kernel, pages=pages),
        out_shape=jax.ShapeDtypeStruct((bd, n_half * NSA_GROUPS, 2 * CMP_HIDDEN), F32),
        grid_spec=grid_spec,
        compiler_params=_params(("parallel", "arbitrary")),
        name="half_proj_paged",
    )(page_table, *([pool2d] * pages), w1cat_one)


def _compress_finish_kernel(fs_ref, pe_ref, w1_ref, w2_ref, o_ref, *, n_rows, step):
    pe = jnp.broadcast_to(pe_ref[...], (8, CMP_LEN * NSA_DK))
    c = _dot(pe, w1_ref[...])[0:1]
    first = fs_ref[:, 0:CMP_HIDDEN]
    second_next = pltpu.roll(fs_ref[:, CMP_HIDDEN:2 * CMP_HIDDEN], n_rows - step, 0)
    hid = _gelu_tanh(first + second_next + c)
    out = _dot(hid.astype(BF16), w2_ref[...])
    row = lax.broadcasted_iota(jnp.int32, (n_rows, 1), 0)
    o_ref[...] = jnp.where(row < n_rows - step, out, 0.0)


def compress_finish(fs, pe, w1, w2, step):
    kv, b_sz, n_sets, n_rows, _ = fs.shape
    return pl.pallas_call(
        functools.partial(_compress_finish_kernel, n_rows=n_rows, step=step),
        out_shape=jax.ShapeDtypeStruct((kv, b_sz, n_sets, n_rows, NSA_DK), F32),
        grid=(kv, b_sz, n_sets),
        in_specs=[pl.BlockSpec((None, None, None, n_rows, 2 * CMP_HIDDEN), lambda k, b, g: (k, b, g, 0, 0)),
                  pl.BlockSpec((None, 1, CMP_LEN * NSA_DK), lambda k, b, g: (k, 0, 0)),
                  pl.BlockSpec((None, CMP_LEN * NSA_DK, CMP_HIDDEN), lambda k, b, g: (k, 0, 0)),
                  pl.BlockSpec((None, CMP_HIDDEN, NSA_DK), lambda k, b, g: (k, 0, 0))],
        out_specs=pl.BlockSpec((None, None, None, n_rows, NSA_DK), lambda k, b, g: (k, b, g, 0, 0)),
        compiler_params=_params(("parallel", "parallel", "parallel")),
        name="compress_finish",
    )(fs, pe, w1, w2)


def _cmp_probs(s, pos, n_cmp):
    c = lax.broadcasted_iota(jnp.int32, (1, s.shape[1]), 1)
    mask = (c * CMP_STRIDE + (CMP_LEN - 1) <= pos) & (c < n_cmp)
    s = jnp.where(mask, s, NEG)
    m = jnp.max(s, axis=-1, keepdims=True)
    e = jnp.where(mask, jnp.exp(s - m), 0.0)
    l = jnp.sum(e, axis=-1, keepdims=True)
    return e / jnp.where(l > 0.0, l, 1.0)


def _block_scores_t(pg, n_rows):
    ncp = pg.shape[1]
    blk = lax.broadcasted_iota(jnp.int32, (n_rows, ncp), 0)
    c = lax.broadcasted_iota(jnp.int32, (n_rows, ncp), 1)
    hpb = SEL_BLOCK // CMP_STRIDE
    spread = (c // hpb == blk).astype(F32) + ((c + 1) // hpb == blk).astype(F32)
    return _dot_nt_split(spread.astype(BF16), pg)


def _select_blocks_t(ps_t, pos_row, n_blk):
    blk = lax.broadcasted_iota(jnp.int32, ps_t.shape, 0)
    cur = pos_row // SEL_BLOCK
    forced = (blk == 0) | (blk == cur) | (blk == cur - 1)
    score = jnp.where(forced, BIG, jnp.where(blk <= cur, ps_t, NEG))
    score = jnp.where(blk < n_blk, score, PAD_SCORE)
    blk_f = blk.astype(F32)
    sel = jnp.zeros(ps_t.shape, F32)
    for _ in range(N_SEL):
        m = jnp.max(score, axis=0, keepdims=True)
        first = jnp.min(jnp.where(score == m, blk_f, float(ps_t.shape[0])), axis=0, keepdims=True)
        hit = blk_f == first
        sel = jnp.where(hit, 1.0, sel)
        score = jnp.where(hit, REMOVED, score)
    return sel


def _lane_groups(x, op):
    parts = [x[:, i * LANES:(i + 1) * LANES] for i in range(x.shape[1] // LANES)]
    while len(parts) > 1:
        parts = [op(parts[i], parts[i + 1]) for i in range(0, len(parts) - 1, 2)] + (
            [parts[-1]] if len(parts) % 2 else [])
    return parts[0]


def _block_expand(n_blocks, n_keys):
    blk = lax.broadcasted_iota(jnp.int32, (n_blocks, n_keys), 0)
    key = lax.broadcasted_iota(jnp.int32, (n_blocks, n_keys), 1)
    return (key // SEL_BLOCK == blk).astype(BF16)


def _online_update(state, s, mask, v_bf):
    m, l, acc = state
    s = jnp.where(mask, s, NEG)
    m_new = jnp.maximum(m, jnp.max(s, axis=-1, keepdims=True))
    alpha = jnp.exp(m - m_new)
    p = jnp.where(mask, jnp.exp(s - m_new), 0.0)
    l = alpha * l + jnp.sum(p, axis=-1, keepdims=True)
    acc = alpha * acc + _dot(p.astype(BF16), v_bf)
    return m_new, l, acc


def _masked_attend(q_bf, k_bf, v_bf, mask):
    s = jnp.where(mask, _dot_nt(q_bf, k_bf), NEG)
    m = jnp.max(s, axis=-1, keepdims=True)
    e = jnp.where(mask, jnp.exp(s - m), 0.0)
    l = jnp.sum(e, axis=-1, keepdims=True)
    return _dot((e / l).astype(BF16), v_bf)


SEL_KEYS = 512


def _prompt_block_rows(t_len):
    return max(-(-t_len // SEL_BLOCK), N_SEL)


def _nsa_prompt_kernel(q_ref, kc_ref, vc_ref, ks_ref, vs_ref, kw_ref, vw_ref, gate_ref, o_ref, sel_ref, s_ref,
                       kv_ref, *, t_len):
    qb = QUERY_BLOCK
    start = pl.program_id(2) * qb

    @pl.when(pl.program_id(2) == 0)
    def _():
        for i, ref in enumerate((ks_ref, vs_ref, kw_ref, vw_ref)):
            kv_ref[i] = ref[...].astype(BF16)

    pos = start + lax.broadcasted_iota(jnp.int32, (qb, 1), 0)
    pos_row = start + lax.broadcasted_iota(jnp.int32, (1, qb), 1)
    n_cmp = t_len // CMP_STRIDE - 1
    n_blk = max(-(-t_len // SEL_BLOCK), N_SEL)
    q = q_ref[...] * NSA_SCALE
    qh = [q[:, h * NSA_DK:(h + 1) * NSA_DK].astype(BF16) for h in range(NSA_HPG)]

    kc = kc_ref[...].astype(BF16)
    vc = vc_ref[...].astype(BF16)
    o_cmp = []
    pg = jnp.zeros((qb, kc.shape[0]), F32)
    for h in range(NSA_HPG):
        p = _cmp_probs(_dot_nt(qh[h], kc), pos, n_cmp)
        o_cmp.append(_dot(p.astype(BF16), vc))
        pg = pg + p
    sel_ref[...] = _select_blocks_t(_block_scores_t(pg, _prompt_block_rows(t_len)), pos_row, n_blk)

    w_rows = WINDOW + qb
    base = pl.multiple_of(jnp.maximum(start - WINDOW, 0), qb)
    kw = kv_ref[2, pl.ds(base, w_rows), :]
    vw = kv_ref[3, pl.ds(base, w_rows), :]
    dist = pos - (base + lax.broadcasted_iota(jnp.int32, (1, w_rows), 1))
    w_mask = (dist >= 0) & (dist <= WINDOW)
    o_win = [_masked_attend(qh[h], kw, vw, w_mask) for h in range(NSA_HPG)]

    n_steps = (start + qb + SEL_KEYS - 1) // SEL_KEYS
    blocks_per_step = SEL_KEYS // SEL_BLOCK
    expand = _block_expand(blocks_per_step, SEL_KEYS)

    def scores_step(j, m_run):
        k0 = pl.multiple_of(j * SEL_KEYS, SEL_KEYS)
        k_bf = kv_ref[0, pl.ds(k0, SEL_KEYS), :]
        key = k0 + lax.broadcasted_iota(jnp.int32, (1, SEL_KEYS), 1)
        chosen = sel_ref[pl.ds(pl.multiple_of(j * blocks_per_step, blocks_per_step), blocks_per_step), :]
        mask = (_dot_tn(chosen.astype(BF16), expand) > 0.5) & (key <= pos)
        out = []
        for h in range(NSA_HPG):
            s = jnp.where(mask, _dot_nt(qh[h], k_bf), NEG)
            s_ref[j, h] = s
            out.append(jnp.maximum(m_run[h], _lane_groups(s, jnp.maximum)))
        return tuple(out)

    m_run = lax.fori_loop(0, n_steps, scores_step, tuple(jnp.full((qb, LANES), NEG, F32) for _ in range(NSA_HPG)))
    m_fin = [jnp.max(m, axis=-1, keepdims=True) for m in m_run]

    def values_step(j, carry):
        k0 = pl.multiple_of(j * SEL_KEYS, SEL_KEYS)
        v_bf = kv_ref[1, pl.ds(k0, SEL_KEYS), :]
        out = []
        for h in range(NSA_HPG):
            l_run, acc = carry[h]
            p = jnp.exp(s_ref[j, h] - m_fin[h])
            out.append((l_run + _lane_groups(p, jnp.add), acc + _dot(p.astype(BF16), v_bf)))
        return tuple(out)

    sel_state = lax.fori_loop(0, n_steps, values_step,
                              tuple((jnp.zeros((qb, LANES), F32), jnp.zeros((qb, NSA_DK), F32))
                                    for _ in range(NSA_HPG)))

    gates = jax.nn.sigmoid(gate_ref[...])
    outs = []
    for h in range(NSA_HPG):
        l_run, acc = sel_state[h]
        o_sel = acc / jnp.sum(l_run, axis=-1, keepdims=True)
        outs.append(gates[:, 3 * h:3 * h + 1] * o_cmp[h] + gates[:, 3 * h + 1:3 * h + 2] * o_sel
                    + gates[:, 3 * h + 2:3 * h + 3] * o_win[h])
    o_ref[...] = jnp.concatenate(outs, axis=1).astype(o_ref.dtype)


def nsa_prompt(z, kcvc, b_sz, t_len):
    assert t_len % SEL_KEYS == 0 and t_len >= WINDOW + QUERY_BLOCK
    nqb = t_len // QUERY_BLOCK
    n_half = t_len // CMP_STRIDE
    hd = NSA_HPG * NSA_DK

    def rows(col0):
        return pl.BlockSpec((t_len, NSA_DK), lambda b, g, i: (b, col0 // NSA_DK + g))

    def cmp_spec(kv):
        return pl.BlockSpec((None, None, None, n_half, NSA_DK), lambda b, g, i: (kv, b, g, 0, 0))

    return pl.pallas_call(
        functools.partial(_nsa_prompt_kernel, t_len=t_len),
        out_shape=jax.ShapeDtypeStruct((b_sz * t_len, NSA_HEADS * NSA_DK), BF16),
        grid=(b_sz, NSA_GROUPS, nqb),
        in_specs=[pl.BlockSpec((QUERY_BLOCK, hd), lambda b, g, i: (b * nqb + i, g)),
                  cmp_spec(0), cmp_spec(1),
                  rows(C_KS), rows(C_VS), rows(C_KW), rows(C_VW),
                  pl.BlockSpec((QUERY_BLOCK, LANES), lambda b, g, i: (b * nqb + i, C_TAIL // LANES + g))],
        out_specs=pl.BlockSpec((QUERY_BLOCK, hd), lambda b, g, i: (b * nqb + i, g)),
        scratch_shapes=[pltpu.VMEM((_prompt_block_rows(t_len), QUERY_BLOCK), F32),
                        pltpu.VMEM((t_len // SEL_KEYS, NSA_HPG, QUERY_BLOCK, SEL_KEYS), F32),
                        pltpu.VMEM((4, t_len, NSA_DK), BF16)],
        compiler_params=_params(("parallel", "parallel", "arbitrary")),
        name="nsa_prompt",
    )(z, kcvc, kcvc, z, z, z, z, z)


S_PAD = 8


def _nsa_sample_kernel(pt_ref, q_ref, kc_ref, vc_ref, kn_ref, vn_ref, kw_ref, vw_ref, kwn_ref, vwn_ref, gate_ref, *refs,
                       pages, past, s_len, n_cmp, ncp, n_blk, blk_rows, win_rows):
    kpages, vpages = refs[:pages], refs[pages:2 * pages]
    o_ref = refs[2 * pages]
    sel_ref, m_ref, l_ref, acc_ref, oc_ref = refs[2 * pages + 1:]
    j = pl.program_id(1)
    n_j = pl.num_programs(1)
    rows = NSA_HPG * s_len
    step = lax.broadcasted_iota(jnp.int32, (rows, 1), 0) % s_len
    pos = past + step
    qg = [(q_ref[g] * NSA_SCALE).astype(BF16) for g in range(NSA_GROUPS)]

    @pl.when(j == 0)
    def _():
        step_i = lax.broadcasted_iota(jnp.int32, (LANES, rows), 1) % s_len
        lane_i = lax.broadcasted_iota(jnp.int32, (LANES, rows), 0)
        step_o = lax.broadcasted_iota(jnp.int32, (rows, LANES), 0) % s_len
        lane_o = lax.broadcasted_iota(jnp.int32, (rows, LANES), 1)
        pos_row = past + lax.broadcasted_iota(jnp.int32, (1, LANES), 1) % S_PAD
        pg = jnp.zeros((LANES, ncp), F32)
        for g in range(NSA_GROUPS):
            kc = kc_ref[pl.ds(g, ncp, stride=NSA_GROUPS), :].astype(BF16)
            vc = vc_ref[pl.ds(g, ncp, stride=NSA_GROUPS), :].astype(BF16)
            p = _cmp_probs(_dot_nt(qg[g], kc), pos, n_cmp)
            oc_ref[g] = _dot(p.astype(BF16), vc)
            fold = (step_i + g * S_PAD == lane_i).astype(BF16)
            pg = pg + _dot_split(fold, p)
            m_ref[g] = jnp.full((rows, 1), NEG, F32)
            l_ref[g] = jnp.zeros((rows, 1), F32)
            acc_ref[g] = jnp.zeros((rows, NSA_DK), F32)
        sel_t = _select_blocks_t(_block_scores_t(pg, blk_rows), pos_row, n_blk).astype(BF16)
        for g in range(NSA_GROUPS):
            unfold = (step_o + g * S_PAD == lane_o).astype(BF16)
            sel_ref[g] = _dot_nt(sel_t, unfold)

    n_keys = pages * PAGE_SIZE
    blocks_per_step = n_keys // SEL_BLOCK
    expand = _block_expand(blocks_per_step, n_keys)
    key = j * n_keys + lax.broadcasted_iota(jnp.int32, (1, n_keys), 1)
    blk0 = pl.multiple_of(j * blocks_per_step, blocks_per_step)
    for g in range(NSA_GROUPS):
        k_bf = jnp.concatenate([kpages[p][pl.ds(g, PAGE_SIZE, stride=NSA_GROUPS), :] for p in range(pages)],
                               axis=0).astype(BF16)
        v_bf = jnp.concatenate([vpages[p][pl.ds(g, PAGE_SIZE, stride=NSA_GROUPS), :] for p in range(pages)],
                               axis=0).astype(BF16)
        chosen = _dot_tn(sel_ref[g, pl.ds(blk0, blocks_per_step), :].astype(BF16), expand) > 0.5
        m, l, acc = _online_update((m_ref[g], l_ref[g], acc_ref[g]), _dot_nt(qg[g], k_bf),
                                   chosen & (key <= pos), v_bf)
        m_ref[g] = m
        l_ref[g] = l
        acc_ref[g] = acc

    @pl.when(j == n_j - 1)
    def _():
        gates = jax.nn.sigmoid(gate_ref[...])
        new_i = lax.broadcasted_iota(jnp.int32, (1, S_PAD), 1)
        new_key = past + new_i
        new_blk = past // SEL_BLOCK
        first_row = (lax.broadcasted_iota(jnp.int32, (8, S_PAD), 0) == 0).astype(BF16)
        dist = pos - (past - win_rows + lax.broadcasted_iota(jnp.int32, (1, win_rows), 1))
        w_mask = (dist >= 0) & (dist <= WINDOW)
        n_mask = (new_key <= pos) & (pos - new_key <= WINDOW) & (new_i < s_len)
        for g in range(NSA_GROUPS):
            chosen_new = _dot_tn(sel_ref[g, new_blk:new_blk + 8, :].astype(BF16), first_row) > 0.5
            mask = chosen_new & (new_key <= pos) & (new_i < s_len)
            _, l, acc = _online_update((m_ref[g], l_ref[g], acc_ref[g]),
                                       _dot_nt(qg[g], kn_ref[g].astype(BF16)), mask, vn_ref[g].astype(BF16))
            o_sel = acc / l
            kw = kw_ref[pl.ds(g, win_rows, stride=NSA_GROUPS), :].astype(BF16)
            vw = vw_ref[pl.ds(g, win_rows, stride=NSA_GROUPS), :].astype(BF16)
            s_old = jnp.where(w_mask, _dot_nt(qg[g], kw), NEG)
            s_new = jnp.where(n_mask, _dot_nt(qg[g], kwn_ref[g].astype(BF16)), NEG)
            m_w = jnp.maximum(jnp.max(s_old, axis=-1, keepdims=True), jnp.max(s_new, axis=-1, keepdims=True))
            e_old = jnp.where(w_mask, jnp.exp(s_old - m_w), 0.0)
            e_new = jnp.where(n_mask, jnp.exp(s_new - m_w), 0.0)
            l_w = jnp.sum(e_old, axis=-1, keepdims=True) + jnp.sum(e_new, axis=-1, keepdims=True)
            o_win = (_dot(e_old.astype(BF16), vw) + _dot(e_new.astype(BF16), vwn_ref[g].astype(BF16))) / l_w
            gt = gates[g]
            o_ref[g] = gt[:, 0:1] * oc_ref[g] + gt[:, 1:2] * o_sel + gt[:, 2:3] * o_win


def nsa_sample(q, kc, vc, k_new, v_new, kw_old, vw_old, kw_new, vw_new, gate, pool_k, pool_v, page_table, s_len,
               pages):
    bd, n_pages = page_table.shape
    past = n_pages * PAGE_SIZE
    rows = NSA_HPG * s_len
    n_cmp = past // CMP_STRIDE - 1
    n_blk = max(-(-(past + s_len) // SEL_BLOCK), N_SEL)
    assert s_len <= S_PAD and s_len < CMP_STRIDE and n_pages % pages == 0
    assert (past // SEL_BLOCK) % 8 == 0 and (pages * PAGE_SIZE // SEL_BLOCK) % 8 == 0
    blk_rows = -(-(past // SEL_BLOCK + 8) // LANES) * LANES
    ncp = kc.shape[1] // NSA_GROUPS
    win_rows = kw_old.shape[1] // NSA_GROUPS
    new_spec = pl.BlockSpec((None, NSA_GROUPS, S_PAD, NSA_DK), lambda b, j, pt: (b, 0, 0, 0))

    def per_b(shape):
        return pl.BlockSpec((None,) + shape, lambda b, j, pt: (b,) + (0,) * len(shape))

    def page_spec(i):
        return pl.BlockSpec((None, PAGE_SIZE * NSA_GROUPS, NSA_DK), lambda b, j, pt: (pt[b, j * pages + i], 0, 0))

    grid_spec = pltpu.PrefetchScalarGridSpec(
        num_scalar_prefetch=1,
        grid=(bd, n_pages // pages),
        in_specs=[per_b((NSA_GROUPS, rows, NSA_DK)), per_b((ncp * NSA_GROUPS, NSA_DK)), per_b((ncp * NSA_GROUPS, NSA_DK)),
                  new_spec, new_spec,
                  per_b((win_rows * NSA_GROUPS, NSA_DK)), per_b((win_rows * NSA_GROUPS, NSA_DK)),
                  new_spec, new_spec,
                  per_b((NSA_GROUPS, rows, LANES))]
        + [page_spec(i) for i in range(pages)] * 2,
        out_specs=per_b((NSA_GROUPS, rows, NSA_DK)),
        scratch_shapes=[pltpu.VMEM((NSA_GROUPS, blk_rows, rows), F32),
                        pltpu.VMEM((NSA_GROUPS, rows, 1), F32),
                        pltpu.VMEM((NSA_GROUPS, rows, 1), F32),
                        pltpu.VMEM((NSA_GROUPS, rows, NSA_DK), F32),
                        pltpu.VMEM((NSA_GROUPS, rows, NSA_DK), F32)],
    )
    return pl.pallas_call(
        functools.partial(_nsa_sample_kernel, pages=pages, past=past, s_len=s_len, n_cmp=n_cmp, ncp=ncp, n_blk=n_blk,
                          blk_rows=blk_rows, win_rows=win_rows),
        out_shape=jax.ShapeDtypeStruct((bd, NSA_GROUPS, rows, NSA_DK), F32),
        grid_spec=grid_spec,
        compiler_params=_params(("parallel", "arbitrary")),
        name="nsa_sample",
    )(page_table, q, kc, vc, k_new, v_new, kw_old, vw_old, kw_new, vw_new, gate,
      *([pool_k] * pages), *([pool_v] * pages))


GLA_HEADS_PER_STEP = 2


def _gla_kernel(q_ref, k_ref, v_ref, gr_ref, lr_ref, wlr_ref, blr_ref, ng_ref, s0_ref, o_ref, sT_out_ref,
                st_ref, *, chunk, t_valid):
    c = pl.program_id(2)
    row = lax.broadcasted_iota(jnp.int32, (chunk, 1), 0)
    tri = (lax.broadcasted_iota(jnp.int32, (chunk, chunk), 1)
           <= lax.broadcasted_iota(jnp.int32, (chunk, chunk), 0)).astype(BF16)
    lr = lr_ref[...].astype(BF16)

    @pl.when(c == 0)
    def _():
        for hh in range(GLA_HEADS_PER_STEP):
            st_ref[hh] = s0_ref[hh].T

    states = []
    for hh in range(GLA_HEADS_PER_STEP):
        dk = slice(hh * GLA_DK, (hh + 1) * GLA_DK)
        dv = slice(hh * GLA_DV, (hh + 1) * GLA_DV)
        x = _dot(lr, wlr_ref[hh]) + blr_ref[hh]
        log_a = jnp.where(row < t_valid, jax.nn.log_sigmoid(x) / GLA_TAU, 0.0)
        b = _dot_split(tri, log_a)
        qs = q_ref[:, dk] * (GLA_DK ** -0.5)
        k = k_ref[:, dk]
        a_t = jnp.zeros((chunk, LANES), F32)
        for t in range(min(chunk, t_valid)):
            n_s = -(-(t + 1) // 8) * 8
            seen = lax.broadcasted_iota(jnp.int32, (n_s, GLA_DK), 0) <= t
            w = jnp.where(seen, jnp.exp(jnp.where(seen, b[t:t + 1] - b[0:n_s], 0.0)), 0.0)
            col = jnp.sum(k[0:n_s] * w * qs[t:t + 1], axis=-1, keepdims=True)
            filled = jnp.where(lax.broadcasted_iota(jnp.int32, (n_s, LANES), 1) == t, col, a_t[0:n_s])
            a_t = filled if n_s == chunk else jnp.concatenate([filled, a_t[n_s:]], axis=0)
        v_bf = v_ref[:, dv].astype(BF16)
        st = st_ref[hh]
        o = (_dot_tn(a_t[:, 0:chunk].astype(BF16), v_bf)
             + _dot_nt((qs * jnp.exp(b)).astype(BF16), st.astype(BF16)))
        b_last = b[chunk - 1:chunk, :]
        k_dec = (k * jnp.exp(b_last - b)).astype(BF16)
        st_new = st * jnp.exp(b_last) + _dot_tn(v_bf, k_dec)
        st_ref[hh] = st_new
        states.append(st_new)
        gr = gr_ref[:, dv]
        o_ref[:, dv] = (_rms(o, ng_ref[...]) * (gr * jax.nn.sigmoid(gr))).astype(o_ref.dtype)

    @pl.when(c == pl.num_programs(2) - 1)
    def _():
        for hh in range(GLA_HEADS_PER_STEP):
            sT_out_ref[hh] = states[hh].T


def gla(z, wlr, blr, norm_g, s0, b_sz, t_len, chunk, t_valid):
    assert chunk <= LANES and chunk % 8 == 0 and t_len % chunk == 0 and GLA_HEADS % GLA_HEADS_PER_STEP == 0
    nck = t_len // chunk
    hp = GLA_HEADS_PER_STEP

    def seg(col0, width):
        return pl.BlockSpec((chunk, hp * width), lambda b, h, c: (b * nck + c, col0 // (hp * width) + h))

    return pl.pallas_call(
        functools.partial(_gla_kernel, chunk=chunk, t_valid=t_valid),
        out_shape=(jax.ShapeDtypeStruct((b_sz * t_len, GLA_HEADS * GLA_DV), BF16),
                   jax.ShapeDtypeStruct((b_sz, GLA_HEADS, GLA_DK, GLA_DV), F32)),
        grid=(b_sz, GLA_HEADS // hp, nck),
        in_specs=[seg(C_GQ, GLA_DK), seg(C_GK, GLA_DK), seg(C_GV, GLA_DV), seg(C_GR, GLA_DV),
                  pl.BlockSpec((chunk, LANES), lambda b, h, c: (b * nck + c, C_TAIL // LANES)),
                  pl.BlockSpec((hp, LANES, GLA_DK), lambda b, h, c: (h, 0, 0)),
                  pl.BlockSpec((hp, 1, GLA_DK), lambda b, h, c: (h, 0, 0)),
                  pl.BlockSpec((1, GLA_DV), lambda b, h, c: (0, 0)),
                  pl.BlockSpec((None, hp, GLA_DK, GLA_DV), lambda b, h, c: (b, h, 0, 0))],
        out_specs=(pl.BlockSpec((chunk, hp * GLA_DV), lambda b, h, c: (b * nck + c, h)),
                   pl.BlockSpec((None, hp, GLA_DK, GLA_DV), lambda b, h, c: (b, h, 0, 0))),
        scratch_shapes=[pltpu.VMEM((hp, GLA_DV, GLA_DK), F32)],
        compiler_params=_params(("parallel", "parallel", "arbitrary")),
        name="gla",
    )(z, z, z, z, z, wlr, blr, norm_g.reshape(1, GLA_DV), s0)


def _merge_kernel(on_ref, og_ref, wn_ref, wg_ref, ma_ref, mb_ref, y_ref):
    y = (jax.nn.sigmoid(ma_ref[...]) * _dot(on_ref[...], wn_ref[...])
         + jax.nn.sigmoid(mb_ref[...]) * _dot(og_ref[...], wg_ref[...]))
    y_ref[...] = y.astype(y_ref.dtype)


def merge(o_nsa, o_gla, w_nsa, w_gla, z, tm, tn):
    n = o_nsa.shape[0]
    return pl.pallas_call(
        _merge_kernel,
        out_shape=jax.ShapeDtypeStruct((n, D_MODEL), BF16),
        grid=(n // tm, D_MODEL // tn),
        in_specs=[pl.BlockSpec((tm, o_nsa.shape[1]), lambda i, j: (i, 0)),
                  pl.BlockSpec((tm, o_gla.shape[1]), lambda i, j: (i, 0)),
                  pl.BlockSpec((w_nsa.shape[0], tn), lambda i, j: (0, j)),
                  pl.BlockSpec((w_gla.shape[0], tn), lambda i, j: (0, j)),
                  pl.BlockSpec((tm, tn), lambda i, j: (i, C_MA // tn + j)),
                  pl.BlockSpec((tm, tn), lambda i, j: (i, C_MB // tn + j))],
        out_specs=pl.BlockSpec((tm, tn), lambda i, j: (i, j)),
        compiler_params=_params(("parallel", "arbitrary")),
        name="merge",
    )(o_nsa, o_gla, w_nsa, w_gla, z, z)


def _out_proj_kernel(y_ref, w_ref, x_ref, o_ref):
    o_ref[...] = x_ref[...] + _dot(y_ref[...], w_ref[...])


def out_proj(y, w_out, x, tm, tn):
    n = y.shape[0]
    return pl.pallas_call(
        _out_proj_kernel,
        out_shape=jax.ShapeDtypeStruct((n, D_MODEL), F32),
        grid=(n // tm, D_MODEL // tn),
        in_specs=[pl.BlockSpec((tm, D_MODEL), lambda i, j: (i, 0)),
                  pl.BlockSpec((D_MODEL, tn), lambda i, j: (0, j)),
                  pl.BlockSpec((tm, tn), lambda i, j: (i, j))],
        out_specs=pl.BlockSpec((tm, tn), lambda i, j: (i, j)),
        compiler_params=_params(("parallel", "arbitrary")),
        name="out_proj",
    )(y, w_out, x)


def _top_values(s, count):
    vals = []
    for _ in range(count):
        m = jnp.max(s, axis=0, keepdims=True)
        s = jnp.where(s == m, REMOVED, s)
        vals.append(m)
    return vals


def _peer_scores_kernel(q_ref, k1_ref, k2_ref, s1_ref, s2_ref, e1_ref, e2_ref, tau_ref):
    taus = []
    for h in range(PEER_HEADS):
        q1 = q_ref[:, h * 2 * PEER_HALF:h * 2 * PEER_HALF + PEER_HALF].astype(BF16)
        q2 = q_ref[:, h * 2 * PEER_HALF + PEER_HALF:(h + 1) * 2 * PEER_HALF].astype(BF16)
        s1 = _dot_nt(k1_ref[h], q1)
        s2 = _dot_nt(k2_ref[h], q2)
        v1 = _top_values(s1, PEER_TOPK)
        v2 = _top_values(s2, PEER_TOPK)
        v2_all = jnp.concatenate(v2, axis=0)
        cand = jnp.concatenate([v1[i] + v2_all for i in range(PEER_TOPK)], axis=0)
        top = _top_values(cand, PEER_TOPK)
        z = sum(jnp.exp(t - top[0]) for t in top)
        s1_ref[h] = s1
        s2_ref[h] = s2
        e1_ref[h] = jnp.exp(s1 - v1[0])
        e2_ref[h] = jnp.exp(s2 - v2[0]) / z
        taus.append(top[PEER_TOPK - 1])
    tau_ref[...] = jnp.concatenate(taus, axis=0)


def peer_scores(qp, keys1, keys2, tn):
    n = qp.shape[0]
    big = jax.ShapeDtypeStruct((PEER_HEADS, PEER_NKEYS, n), F32)
    big_spec = pl.BlockSpec((PEER_HEADS, PEER_NKEYS, tn), lambda i: (0, 0, i))
    key_spec = pl.BlockSpec((PEER_HEADS, PEER_NKEYS, PEER_HALF), lambda i: (0, 0, 0))
    return pl.pallas_call(
        _peer_scores_kernel,
        out_shape=(big, big, big, big, jax.ShapeDtypeStruct((PEER_HEADS, n), F32)),
        grid=(n // tn,),
        in_specs=[pl.BlockSpec((tn, qp.shape[1]), lambda i: (i, 0)), key_spec, key_spec],
        out_specs=(big_spec, big_spec, big_spec, big_spec, pl.BlockSpec((PEER_HEADS, tn), lambda i: (0, i))),
        compiler_params=_params(("parallel",)),
        name="peer_scores",
    )(qp, keys1, keys2)


PEER_I1_PER_TILE = 8


def _peer_dense_kernel(x_ref, g2_ref, gf_ref, u_ref, v_ref, s1_ref, e1_ref, s2_ref, e2_ref, tau_ref, o_ref,
                       h_ref, acc_ref):
    e = pl.program_id(1)

    @pl.when(e == 0)
    def _():
        h_ref[...] = _rms(x_ref[...], g2_ref[...]).astype(BF16)
        acc_ref[...] = jnp.zeros_like(acc_ref)

    pre = _dot_nt(u_ref[...], h_ref[...])
    n_tok = pre.shape[1]
    tile = min(LANES, n_tok)
    parts = []
    for c in range(PEER_I1_PER_TILE):
        cols = []
        for t0 in range(0, n_tok, tile):
            tok = slice(t0, t0 + tile)
            w = jnp.zeros((PEER_NKEYS, tile), F32)
            for h in range(PEER_HEADS):
                keep = s1_ref[h, c:c + 1, tok] + s2_ref[h, :, tok] >= tau_ref[h:h + 1, tok]
                w = w + jnp.where(keep, e1_ref[h, c:c + 1, tok] * e2_ref[h, :, tok], 0.0)
            cols.append(w * _gelu_tanh(pre[c * PEER_NKEYS:(c + 1) * PEER_NKEYS, tok]))
        parts.append(jnp.concatenate(cols, axis=1).astype(BF16))
    acc_ref[...] += _dot_tn(jnp.concatenate(parts, axis=0), v_ref[...])

    @pl.when(e == pl.num_programs(1) - 1)
    def _():
        o_ref[...] = _rms(x_ref[...] + acc_ref[...], gf_ref[...])


def peer_dense(x, g2, gf, u_bf, v_bf, s1, s2, e1, e2, tau, tn):
    n = x.shape[0]
    te = PEER_I1_PER_TILE * PEER_NKEYS
    n_exp = u_bf.shape[0]
    sub = pl.BlockSpec((PEER_HEADS, PEER_I1_PER_TILE, tn), lambda i, e: (0, e, i))
    full = pl.BlockSpec((PEER_HEADS, PEER_NKEYS, tn), lambda i, e: (0, 0, i))
    vec = pl.BlockSpec((1, D_MODEL), lambda i, e: (0, 0))
    return pl.pallas_call(
        _peer_dense_kernel,
        out_shape=jax.ShapeDtypeStruct((n, D_MODEL), F32),
        grid=(n // tn, n_exp // te),
        in_specs=[pl.BlockSpec((tn, D_MODEL), lambda i, e: (i, 0)), vec, vec,
                  pl.BlockSpec((te, D_MODEL), lambda i, e: (e, 0)),
                  pl.BlockSpec((te, D_MODEL), lambda i, e: (e, 0)),
                  sub, sub, full, full,
                  pl.BlockSpec((PEER_HEADS, tn), lambda i, e: (0, i))],
        out_specs=pl.BlockSpec((tn, D_MODEL), lambda i, e: (i, 0)),
        scratch_shapes=[pltpu.VMEM((tn, D_MODEL), BF16), pltpu.VMEM((tn, D_MODEL), F32)],
        compiler_params=_params(("parallel", "arbitrary")),
        name="peer_dense",
    )(x, g2.reshape(1, D_MODEL), gf.reshape(1, D_MODEL), u_bf, v_bf, s1, e1, s2, e2, tau)


def _pack_w_in_kernel(ia_ref, ib_ref, kind_ref, a_ref, b_ref, o_ref, *, shifts, gate_row0, per_group, glr_row0):
    kind = kind_ref[pl.program_id(0)]
    for k, s in enumerate(shifts):
        @pl.when(kind == k)
        def _(s=s):
            src = a_ref[...] if s == 0 else jnp.concatenate([a_ref[s:, :], b_ref[:s, :]], axis=0)
            o_ref[...] = src.astype(BF16)
    out_row = lax.broadcasted_iota(jnp.int32, (LANES, LANES), 0)
    src_row = lax.broadcasted_iota(jnp.int32, (LANES, LANES), 1)
    for g in range(NSA_GROUPS):
        @pl.when(kind == len(shifts) + g)
        def _(g=g):
            pick = (src_row == out_row + (gate_row0 + g * per_group)) & (out_row < per_group)
            out = _dot(pick.astype(BF16), a_ref[...].astype(BF16))
            if g == 0:
                pick = ((src_row == out_row + (glr_row0 - GLR_LANE)) & (out_row >= GLR_LANE)
                        & (out_row < GLR_LANE + GLA_RANK))
                out = out + _dot(pick.astype(BF16), b_ref[...].astype(BF16))
            o_ref[...] = out.astype(BF16)


def _pack_w_in(w_in):
    offs = np.concatenate([[0], np.cumsum(IN_SPLITS)]).astype(np.int64)
    plain = (0, 1, 2, 3, 4, 5, 6, 8, 9, 10, 11, 13, 14)
    src_start = np.concatenate([np.arange(offs[i], offs[i + 1], LANES) for i in plain])
    assert all((offs[i + 1] - offs[i]) % LANES == 0 for i in plain) and len(src_start) * LANES == C_TAIL
    shifts = tuple(sorted(set(int(s) for s in src_start % LANES)))
    gate_col, glr_col = int(offs[7]), int(offs[12])
    per_group = 3 * NSA_HPG
    assert gate_col % LANES + NSA_GROUPS * per_group <= LANES and glr_col % LANES >= GLR_LANE
    n_tail = (Z_COLS - C_TAIL) // LANES
    ia = np.concatenate([src_start // LANES, np.full(n_tail, gate_col // LANES)])
    ib = np.concatenate([src_start // LANES + 1, np.full(n_tail, glr_col // LANES)])
    kind = np.concatenate([[shifts.index(int(s)) for s in src_start % LANES],
                           len(shifts) + np.minimum(np.arange(n_tail), NSA_GROUPS - 1)])
    assert all(s % 8 == 0 for s in shifts)
    d = w_in.shape[0]
    w_t = w_in.T
    grid_spec = pltpu.PrefetchScalarGridSpec(
        num_scalar_prefetch=3,
        grid=(Z_COLS // LANES,),
        in_specs=[pl.BlockSpec((LANES, d), lambda c, ia, ib, kind: (ia[c], 0)),
                  pl.BlockSpec((LANES, d), lambda c, ia, ib, kind: (ib[c], 0))],
        out_specs=pl.BlockSpec((LANES, d), lambda c, ia, ib, kind: (c, 0)),
    )
    return pl.pallas_call(
        functools.partial(_pack_w_in_kernel, shifts=shifts, gate_row0=gate_col % LANES, per_group=per_group,
                          glr_row0=glr_col % LANES),
        out_shape=jax.ShapeDtypeStruct((Z_COLS, d), BF16),
        grid_spec=grid_spec,
        compiler_params=_params(("arbitrary",)),
        name="pack_w_in",
    )(jnp.asarray(ia, jnp.int32), jnp.asarray(ib, jnp.int32), jnp.asarray(kind, jnp.int32), w_t, w_t)


def _pack_cmp(pe, w1, w2):
    w1cat = jnp.concatenate([w1[:CMP_STRIDE].reshape(CMP_STRIDE * NSA_DK, CMP_HIDDEN),
                             w1[CMP_STRIDE:].reshape(CMP_STRIDE * NSA_DK, CMP_HIDDEN)], axis=1).astype(BF16)
    return (w1cat, pe.reshape(1, CMP_LEN * NSA_DK).astype(BF16),
            w1.reshape(CMP_LEN * NSA_DK, CMP_HIDDEN).astype(BF16), w2.astype(BF16))


def _row_tile(n, cap):
    t = min(n, cap)
    while n % t:
        t //= 2
    return t


def _channel_tail(x2d, z, o_nsa, o_gla, wts):
    n = x2d.shape[0]
    tm = _row_tile(n, 512)
    y = merge(o_nsa, o_gla, wts["w_nsa"], wts["w_gla"], z, tm, 512)
    x1 = out_proj(y, wts["w_out"], x2d, tm, 512)
    qp = norm_matmul(x1, wts["norm2_g"], wts["w_q"], tm, 512)
    tn = _row_tile(n, 256)
    s1, s2, e1, e2, tau = peer_scores(qp, wts["keys1"], wts["keys2"], tn)
    return peer_dense(x1, wts["norm2_g"], wts["norm_f_g"], wts["u"], wts["v"], s1, s2, e1, e2, tau, _row_tile(n, 512))


def kernel(x_prompt, x_sample, cache_k_cmp, cache_v_cmp, cache_k_slc, cache_v_slc, cache_k_win, cache_v_win, state_gla, page_table, norm1_g, w_in, cmp_pe_k, cmp_w1_k, cmp_w2_k, cmp_pe_v, cmp_w1_v, cmp_w2_v, gla_w_lr2, gla_b_lr, gla_norm_g, w_nsa_proj, w_gla_proj, w_out, norm2_g, peer_w_q, peer_keys1, peer_keys2, peer_u, peer_v, norm_f_g):
    b_sz, t_len, _ = x_prompt.shape
    bd, s_len, _ = x_sample.shape
    n_pool = cache_k_cmp.shape[0]
    wb = cache_k_win.shape[1]

    w_pack = _pack_w_in(w_in)
    cmp_k = _pack_cmp(cmp_pe_k, cmp_w1_k, cmp_w2_k)
    cmp_v = _pack_cmp(cmp_pe_v, cmp_w1_v, cmp_w2_v)
    cmp_kv = [jnp.stack([a, b]) for a, b in zip(cmp_k, cmp_v)]
    wlr = jnp.zeros((LANES, GLA_HEADS * GLA_DK), F32).at[GLR_LANE:GLR_LANE + GLA_RANK].set(gla_w_lr2)
    wlr = wlr.reshape(LANES, GLA_HEADS, GLA_DK).transpose(1, 0, 2).astype(BF16)
    blr = gla_b_lr.reshape(GLA_HEADS, 1, GLA_DK)
    wts = dict(w_nsa=w_nsa_proj.astype(BF16), w_gla=w_gla_proj.astype(BF16), w_out=w_out.astype(BF16),
               w_q=peer_w_q.astype(BF16), norm2_g=norm2_g, norm_f_g=norm_f_g,
               keys1=peer_keys1.astype(BF16), keys2=peer_keys2.astype(BF16),
               u=peer_u.astype(BF16), v=peer_v.astype(BF16))

    n_p = b_sz * t_len
    xp = x_prompt.reshape(n_p, D_MODEL)
    z_p = norm_matmul(xp, norm1_g, w_pack, _row_tile(n_p, 1024), GD, w_rows_are_outputs=True)
    heads_p = split_heads(z_p, _row_tile(n_p, 1024))
    fs_p = half_proj_dense(z_p, cmp_kv[0], b_sz, t_len)
    kcvc_p = compress_finish(fs_p, cmp_kv[1], cmp_kv[2], cmp_kv[3], 1)
    o_nsa_p = nsa_prompt(z_p, kcvc_p, b_sz, t_len)
    s0 = jnp.zeros((b_sz, GLA_HEADS, GLA_DK, GLA_DV), F32)
    o_gla_p, gla_state_p = gla(z_p, wlr, blr, gla_norm_g, s0, b_sz, t_len, GLA_CHUNK, GLA_CHUNK)
    y_prompt = _channel_tail(xp, z_p, o_nsa_p, o_gla_p, wts).reshape(b_sz, t_len, D_MODEL)
    kcr_p, vcr_p, ksr_p, vsr_p, kwr_p, vwr_p = (a.reshape(b_sz, t_len, NSA_GROUPS, NSA_DK) for a in heads_p)
    wl = min(WINDOW, t_len)
    k_win_p = kwr_p[:, t_len - wl:]
    v_win_p = vwr_p[:, t_len - wl:]

    n_s = bd * s_len
    xs = x_sample.reshape(n_s, D_MODEL)
    z_s = norm_matmul(xs, norm1_g, w_pack, _row_tile(n_s, 1024), GD, w_rows_are_outputs=True)
    heads_s = split_heads(z_s, _row_tile(n_s, 1024))
    kcr_s, vcr_s, ksr_s, vsr_s, kwr_s, vwr_s = (a.reshape(bd, s_len, NSA_GROUPS, NSA_DK) for a in heads_s)

    pool2d = lambda p: p.reshape(n_pool, PAGE_SIZE * NSA_GROUPS, NSA_DK)
    pages = 16

    def compress_pool(pool, prm):
        fs = half_proj_paged(pool2d(pool), page_table, prm[0], min(32, page_table.shape[1]))
        return compress_finish(fs[None, :, None], prm[1][None], prm[2][None], prm[3][None], NSA_GROUPS)[0, :, 0]

    kc_s = compress_pool(cache_k_cmp, cmp_k)
    vc_s = compress_pool(cache_v_cmp, cmp_v)

    assert wb >= s_len
    k_win_s = jnp.concatenate([cache_k_win[:, s_len:], kwr_s], axis=1)
    v_win_s = jnp.concatenate([cache_v_win[:, s_len:], vwr_s], axis=1)
    win_2d = lambda a: a.reshape(bd, wb * NSA_GROUPS, NSA_DK)

    def new_rows(a):
        return jnp.pad(a.transpose(0, 2, 1, 3), ((0, 0), (0, 0), (0, S_PAD - s_len), (0, 0)))

    rows = NSA_HPG * s_len
    q_s = z_s[:, C_Q:C_Q + NSA_HEADS * NSA_DK].reshape(bd, s_len, NSA_GROUPS, NSA_HPG, NSA_DK)
    q_s = q_s.transpose(0, 2, 3, 1, 4).reshape(bd, NSA_GROUPS, rows, NSA_DK)
    gate_s = z_s[:, C_TAIL:C_TAIL + 512].reshape(bd, s_len, NSA_GROUPS, LANES)[..., :3 * NSA_HPG]
    gate_s = gate_s.reshape(bd, s_len, NSA_GROUPS, NSA_HPG, 3).transpose(0, 2, 3, 1, 4).reshape(bd, NSA_GROUPS, rows, 3)
    gate_s = jnp.pad(gate_s, ((0, 0), (0, 0), (0, 0), (0, LANES - 3)))
    o_s = nsa_sample(q_s, kc_s, vc_s, new_rows(ksr_s), new_rows(vsr_s), win_2d(cache_k_win), win_2d(cache_v_win),
                     new_rows(kwr_s), new_rows(vwr_s), gate_s, pool2d(cache_k_slc), pool2d(cache_v_slc), page_table,
                     s_len, pages)
    o_nsa_s = o_s.reshape(bd, NSA_GROUPS, NSA_HPG, s_len, NSA_DK).transpose(0, 3, 1, 2, 4)
    o_nsa_s = o_nsa_s.reshape(n_s, NSA_HEADS * NSA_DK).astype(BF16)

    z_s_pad = jnp.pad(z_s.reshape(bd, s_len, Z_COLS), ((0, 0), (0, S_PAD - s_len), (0, 0))).reshape(bd * S_PAD, Z_COLS)
    o_gla_s, gla_state_s = gla(z_s_pad, wlr, blr, gla_norm_g, state_gla, bd, S_PAD, S_PAD, s_len)
    o_gla_s = o_gla_s.reshape(bd, S_PAD, GLA_HEADS * GLA_DV)[:, :s_len].reshape(n_s, GLA_HEADS * GLA_DV)
    y_sample = _channel_tail(xs, z_s, o_nsa_s, o_gla_s, wts).reshape(bd, s_len, D_MODEL)

    return (y_prompt, y_sample, kcr_p, vcr_p, ksr_p, vsr_p, k_win_p, v_win_p, gla_state_p,
            kcr_s, vcr_s, ksr_s, vsr_s, k_win_s, v_win_s, gla_state_s)
```

```python
import functools

import numpy as np
import jax
import jax.numpy as jnp
from jax import lax
from jax.experimental import pallas as pl
from jax.experimental.pallas import tpu as pltpu

F32 = jnp.float32
BF16 = jnp.bfloat16

D_MODEL = 2048
PAGE_SIZE = 128
NSA_HEADS = 16
NSA_GROUPS = 4
NSA_HPG = NSA_HEADS // NSA_GROUPS
NSA_DK = 128
NSA_SCALE = NSA_DK ** -0.5
CMP_LEN = 32
CMP_STRIDE = 16
CMP_HIDDEN = 256
SEL_BLOCK = 64
N_SEL = 16
WINDOW = 512
QUERY_BLOCK = 256
GLA_HEADS = 4
GLA_DK = 256
GLA_DV = 512
GLA_RANK = 16
GLA_TAU = 16.0
GLA_CHUNK = 64
PEER_HEADS = 8
PEER_NKEYS = 128
PEER_HALF = 128
PEER_TOPK = 16
NORM_EPS = 1e-6
NEG = -1e30
BIG = 1e30
PAD_SCORE = -3e38
REMOVED = -float("inf")

IN_SPLITS = (NSA_HEADS * NSA_DK,) + (NSA_GROUPS * NSA_DK,) * 6 + (
    3 * NSA_HEADS, GLA_HEADS * GLA_DK, GLA_HEADS * GLA_DK, GLA_HEADS * GLA_DV,
    GLA_HEADS * GLA_DV, GLA_RANK, D_MODEL, D_MODEL)

GD = NSA_GROUPS * NSA_DK
C_Q = 0
C_KC = C_Q + NSA_HEADS * NSA_DK
C_VC = C_KC + GD
C_KS = C_VC + GD
C_VS = C_KS + GD
C_KW = C_VS + GD
C_VW = C_KW + GD
C_GQ = C_VW + GD
C_GK = C_GQ + GLA_HEADS * GLA_DK
C_GV = C_GK + GLA_HEADS * GLA_DK
C_GR = C_GV + GLA_HEADS * GLA_DV
C_MA = C_GR + GLA_HEADS * GLA_DV
C_MB = C_MA + D_MODEL
C_TAIL = C_MB + D_MODEL
Z_COLS = C_TAIL + 512
GLR_LANE = 16
LANES = 128

VMEM_LIMIT = 56 * 1024 * 1024


def _params(sem):
    return pltpu.CompilerParams(dimension_semantics=sem, vmem_limit_bytes=VMEM_LIMIT)


def _dot(a, b):
    return jnp.dot(a, b, preferred_element_type=F32)


def _dot_nt(a, b):
    return lax.dot_general(a, b, (((1,), (1,)), ((), ())), preferred_element_type=F32)


def _dot_tn(a, b):
    return lax.dot_general(a, b, (((0,), (0,)), ((), ())), preferred_element_type=F32)


def _rms(x, g):
    return x * lax.rsqrt(jnp.mean(x * x, axis=-1, keepdims=True) + NORM_EPS) * g


GELU_C0 = 2.0 * (2.0 / np.pi) ** 0.5
GELU_C1 = GELU_C0 * 0.044715


def _gelu_tanh(x):
    return x / (1.0 + jnp.exp(-(x * (GELU_C0 + GELU_C1 * (x * x)))))


def _bf16_pieces(x):
    hi = x.astype(BF16)
    rest = x - hi.astype(F32)
    mid = rest.astype(BF16)
    lo = (rest - mid.astype(F32)).astype(BF16)
    return hi, mid, lo


def _dot_nt_split(small_ints_bf16, x):
    return sum(_dot_nt(small_ints_bf16, piece) for piece in _bf16_pieces(x))


def _dot_split(small_ints_bf16, x):
    return sum(_dot(small_ints_bf16, piece) for piece in _bf16_pieces(x))


def _norm_matmul_kernel(x_ref, g_ref, w_ref, o_ref, h_ref, *, w_rows_are_outputs):
    @pl.when(pl.program_id(1) == 0)
    def _():
        h_ref[...] = _rms(x_ref[...], g_ref[...]).astype(BF16)

    o_ref[...] = (_dot_nt if w_rows_are_outputs else _dot)(h_ref[...], w_ref[...])


def norm_matmul(x, g, w, tm, tn, w_rows_are_outputs=False):
    n, d = x.shape
    cols = w.shape[0] if w_rows_are_outputs else w.shape[1]
    w_spec = (pl.BlockSpec((tn, d), lambda i, j: (j, 0)) if w_rows_are_outputs
              else pl.BlockSpec((d, tn), lambda i, j: (0, j)))
    return pl.pallas_call(
        functools.partial(_norm_matmul_kernel, w_rows_are_outputs=w_rows_are_outputs),
        out_shape=jax.ShapeDtypeStruct((n, cols), F32),
        grid=(n // tm, cols // tn),
        in_specs=[pl.BlockSpec((tm, d), lambda i, j: (i, 0)),
                  pl.BlockSpec((1, d), lambda i, j: (0, 0)),
                  w_spec],
        out_specs=pl.BlockSpec((tm, tn), lambda i, j: (i, j)),
        scratch_shapes=[pltpu.VMEM((tm, d), BF16)],
        compiler_params=_params(("parallel", "arbitrary")),
        name="norm_matmul",
    )(x, g.reshape(1, d), w)


KV_SEGMENTS = 6


def _split_heads_kernel(z_ref, *head_refs):
    k = pl.program_id(1)
    for seg in range(KV_SEGMENTS):
        @pl.when(k == seg)
        def _(seg=seg):
            for g in range(NSA_GROUPS):
                head_refs[seg][:, g, :] = z_ref[:, g * NSA_DK:(g + 1) * NSA_DK]


def split_heads(z, tm):
    n = z.shape[0]
    heads = jax.ShapeDtypeStruct((n, NSA_GROUPS, NSA_DK), F32)
    return pl.pallas_call(
        _split_heads_kernel,
        out_shape=(heads,) * KV_SEGMENTS,
        grid=(n // tm, KV_SEGMENTS),
        in_specs=[pl.BlockSpec((tm, GD), lambda i, k: (i, C_KC // GD + k))],
        out_specs=(pl.BlockSpec((tm, NSA_GROUPS, NSA_DK), lambda i, k: (i, 0, 0)),) * KV_SEGMENTS,
        compiler_params=_params(("parallel", "arbitrary")),
        name="split_heads",
    )(z)


def _half_rows(ref, first_row, n_half, row_stride, lane0):
    parts = [ref[pl.ds(first_row + r * row_stride, n_half, stride=CMP_STRIDE * row_stride), pl.ds(lane0, NSA_DK)]
             for r in range(CMP_STRIDE)]
    return jnp.concatenate(parts, axis=1)


def _half_proj_dense_kernel(x_ref, w_ref, o_ref, *, n_half):
    o_ref[...] = _dot(_half_rows(x_ref, 0, n_half, 1, 0).astype(BF16), w_ref[...])


def half_proj_dense(z, w1cat, b_sz, t_len):
    n_half = t_len // CMP_STRIDE
    return pl.pallas_call(
        functools.partial(_half_proj_dense_kernel, n_half=n_half),
        out_shape=jax.ShapeDtypeStruct((2, b_sz, NSA_GROUPS, n_half, 2 * CMP_HIDDEN), F32),
        grid=(2, b_sz, NSA_GROUPS),
        in_specs=[pl.BlockSpec((t_len, NSA_DK), lambda kv, b, g: (b, C_KC // NSA_DK + kv * NSA_GROUPS + g)),
                  pl.BlockSpec((None, CMP_STRIDE * NSA_DK, 2 * CMP_HIDDEN), lambda kv, b, g: (kv, 0, 0))],
        out_specs=pl.BlockSpec((None, None, None, n_half, 2 * CMP_HIDDEN), lambda kv, b, g: (kv, b, g, 0, 0)),
        compiler_params=_params(("parallel", "parallel", "parallel")),
        name="half_proj_dense",
    )(z, w1cat)


HALVES_PER_PAGE = PAGE_SIZE // CMP_STRIDE


def _half_proj_paged_kernel(pt_ref, *refs, pages):
    page_refs, w_ref, o_ref = refs[:pages], refs[pages], refs[pages + 1]
    tile = 2 * NSA_GROUPS
    rows_per_half = CMP_STRIDE * NSA_GROUPS
    low = lax.broadcasted_iota(jnp.int32, (tile, NSA_DK), 0) < NSA_GROUPS
    acc = None
    for r2 in range(CMP_STRIDE // 2):
        even, odd = [], []
        for p in range(pages):
            for n in range(0, HALVES_PER_PAGE, 2):
                a = page_refs[p][pl.ds(n * rows_per_half + tile * r2, tile), :]
                b = page_refs[p][pl.ds((n + 1) * rows_per_half + tile * r2, tile), :]
                even.append(jnp.where(low, a, pltpu.roll(b, NSA_GROUPS, 0)))
                odd.append(jnp.where(low, pltpu.roll(a, NSA_GROUPS, 0), b))
        lhs = jnp.concatenate([jnp.concatenate(even, axis=0), jnp.concatenate(odd, axis=0)], axis=1).astype(BF16)
        part = _dot(lhs, w_ref[pl.ds(r2 * 2 * NSA_DK, 2 * NSA_DK), :])
        acc = part if acc is None else acc + part
    o_ref[...] = acc


def half_proj_paged(pool2d, page_table, w1cat_one, pages):
    bd, n_pages = page_table.shape
    n_half = n_pages * HALVES_PER_PAGE
    rows = pages * HALVES_PER_PAGE * NSA_GROUPS

    def page_spec(i):
        return pl.BlockSpec((None, PAGE_SIZE * NSA_GROUPS, NSA_DK),
                            lambda b, j, pt: (pt[b, j * pages + i], 0, 0))

    grid_spec = pltpu.PrefetchScalarGridSpec(
        num_scalar_prefetch=1,
        grid=(bd, n_pages // pages),
        in_specs=[page_spec(i) for i in range(pages)]
        + [pl.BlockSpec((CMP_STRIDE * NSA_DK, 2 * CMP_HIDDEN), lambda b, j, pt: (0, 0))],
        out_specs=pl.BlockSpec((None, rows, 2 * CMP_HIDDEN), lambda b, j, pt: (b, j, 0)),
    )
    return pl.pallas_call(
        functools.partial(_half_proj_paged_kernel, pages=pages),
        out_shape=jax.ShapeDtypeStruct((bd, n_half * NSA_GROUPS, 2 * CMP_HIDDEN), F32),
        grid_spec=grid_spec,
        compiler_params=_params(("parallel", "arbitrary")),
        name="half_proj_paged",
    )(page_table, *([pool2d] * pages), w1cat_one)


def _compress_finish_kernel(fs_ref, pe_ref, w1_ref, w2_ref, o_ref, *, n_rows, step):
    pe = jnp.broadcast_to(pe_ref[...], (8, CMP_LEN * NSA_DK))
    c = _dot(pe, w1_ref[...])[0:1]
    first = fs_ref[:, 0:CMP_HIDDEN]
    second_next = pltpu.roll(fs_ref[:, CMP_HIDDEN:2 * CMP_HIDDEN], n_rows - step, 0)
    hid = _gelu_tanh(first + second_next + c)
    out = _dot(hid.astype(BF16), w2_ref[...])
    row = lax.broadcasted_iota(jnp.int32, (n_rows, 1), 0)
    o_ref[...] = jnp.where(row < n_rows - step, out, 0.0)


def compress_finish(fs, pe, w1, w2, step):
    kv, b_sz, n_sets, n_rows, _ = fs.shape
    return pl.pallas_call(
        functools.partial(_compress_finish_kernel, n_rows=n_rows, step=step),
        out_shape=jax.ShapeDtypeStruct((kv, b_sz, n_sets, n_rows, NSA_DK), F32),
        grid=(kv, b_sz, n_sets),
        in_specs=[pl.BlockSpec((None, None, None, n_rows, 2 * CMP_HIDDEN), lambda k, b, g: (k, b, g, 0, 0)),
                  pl.BlockSpec((None, 1, CMP_LEN * NSA_DK), lambda k, b, g: (k, 0, 0)),
                  pl.BlockSpec((None, CMP_LEN * NSA_DK, CMP_HIDDEN), lambda k, b, g: (k, 0, 0)),
                  pl.BlockSpec((None, CMP_HIDDEN, NSA_DK), lambda k, b, g: (k, 0, 0))],
        out_specs=pl.BlockSpec((None, None, None, n_rows, NSA_DK), lambda k, b, g: (k, b, g, 0, 0)),
        compiler_params=_params(("parallel", "parallel", "parallel")),
        name="compress_finish",
    )(fs, pe, w1, w2)


def _cmp_probs(s, pos, n_cmp):
    c = lax.broadcasted_iota(jnp.int32, (1, s.shape[1]), 1)
    mask = (c * CMP_STRIDE + (CMP_LEN - 1) <= pos) & (c < n_cmp)
    s = jnp.where(mask, s, NEG)
    m = jnp.max(s, axis=-1, keepdims=True)
    e = jnp.where(mask, jnp.exp(s - m), 0.0)
    l = jnp.sum(e, axis=-1, keepdims=True)
    return e / jnp.where(l > 0.0, l, 1.0)


def _block_scores_t(pg, n_rows):
    ncp = pg.shape[1]
    blk = lax.broadcasted_iota(jnp.int32, (n_rows, ncp), 0)
    c = lax.broadcasted_iota(jnp.int32, (n_rows, ncp), 1)
    hpb = SEL_BLOCK // CMP_STRIDE
    spread = (c // hpb == blk).astype(F32) + ((c + 1) // hpb == blk).astype(F32)
    return _dot_nt_split(spread.astype(BF16), pg)


def _select_blocks_t(ps_t, pos_row, n_blk):
    blk = lax.broadcasted_iota(jnp.int32, ps_t.shape, 0)
    cur = pos_row // SEL_BLOCK
    forced = (blk == 0) | (blk == cur) | (blk == cur - 1)
    score = jnp.where(forced, BIG, jnp.where(blk <= cur, ps_t, NEG))
    score = jnp.where(blk < n_blk, score, PAD_SCORE)
    blk_f = blk.astype(F32)
    sel = jnp.zeros(ps_t.shape, F32)
    for _ in range(N_SEL):
        m = jnp.max(score, axis=0, keepdims=True)
        first = jnp.min(jnp.where(score == m, blk_f, float(ps_t.shape[0])), axis=0, keepdims=True)
        hit = blk_f == first
        sel = jnp.where(hit, 1.0, sel)
        score = jnp.where(hit, REMOVED, score)
    return sel


def _lane_groups(x, op):
    parts = [x[:, i * LANES:(i + 1) * LANES] for i in range(x.shape[1] // LANES)]
    while len(parts) > 1:
        parts = [op(parts[i], parts[i + 1]) for i in range(0, len(parts) - 1, 2)] + (
            [parts[-1]] if len(parts) % 2 else [])
    return parts[0]


def _block_expand(n_blocks, n_keys):
    blk = lax.broadcasted_iota(jnp.int32, (n_blocks, n_keys), 0)
    key = lax.broadcasted_iota(jnp.int32, (n_blocks, n_keys), 1)
    return (key // SEL_BLOCK == blk).astype(BF16)


def _online_update(state, s, mask, v_bf):
    m, l, acc = state
    s = jnp.where(mask, s, NEG)
    m_new = jnp.maximum(m, jnp.max(s, axis=-1, keepdims=True))
    alpha = jnp.exp(m - m_new)
    p = jnp.where(mask, jnp.exp(s - m_new), 0.0)
    l = alpha * l + jnp.sum(p, axis=-1, keepdims=True)
    acc = alpha * acc + _dot(p.astype(BF16), v_bf)
    return m_new, l, acc


def _masked_attend(q_bf, k_bf, v_bf, mask):
    s = jnp.where(mask, _dot_nt(q_bf, k_bf), NEG)
    m = jnp.max(s, axis=-1, keepdims=True)
    e = jnp.where(mask, jnp.exp(s - m), 0.0)
    l = jnp.sum(e, axis=-1, keepdims=True)
    return _dot((e / l).astype(BF16), v_bf)


SEL_KEYS = 512


def _prompt_block_rows(t_len):
    return max(-(-t_len // SEL_BLOCK), N_SEL)


def _nsa_prompt_kernel(q_ref, kc_ref, vc_ref, ks_ref, vs_ref, kw_ref, vw_ref, gate_ref, o_ref, sel_ref, s_ref,
                       kv_ref, *, t_len):
    qb = QUERY_BLOCK
    start = pl.program_id(2) * qb

    @pl.when(pl.program_id(2) == 0)
    def _():
        for i, ref in enumerate((ks_ref, vs_ref, kw_ref, vw_ref)):
            kv_ref[i] = ref[...].astype(BF16)

    pos = start + lax.broadcasted_iota(jnp.int32, (qb, 1), 0)
    pos_row = start + lax.broadcasted_iota(jnp.int32, (1, qb), 1)
    n_cmp = t_len // CMP_STRIDE - 1
    n_blk = max(-(-t_len // SEL_BLOCK), N_SEL)
    q = q_ref[...] * NSA_SCALE
    qh = [q[:, h * NSA_DK:(h + 1) * NSA_DK].astype(BF16) for h in range(NSA_HPG)]

    kc = kc_ref[...].astype(BF16)
    vc = vc_ref[...].astype(BF16)
    o_cmp = []
    pg = jnp.zeros((qb, kc.shape[0]), F32)
    for h in range(NSA_HPG):
        p = _cmp_probs(_dot_nt(qh[h], kc), pos, n_cmp)
        o_cmp.append(_dot(p.astype(BF16), vc))
        pg = pg + p
    sel_ref[...] = _select_blocks_t(_block_scores_t(pg, _prompt_block_rows(t_len)), pos_row, n_blk)

    w_rows = WINDOW + qb
    base = pl.multiple_of(jnp.maximum(start - WINDOW, 0), qb)
    kw = kv_ref[2, pl.ds(base, w_rows), :]
    vw = kv_ref[3, pl.ds(base, w_rows), :]
    dist = pos - (base + lax.broadcasted_iota(jnp.int32, (1, w_rows), 1))
    w_mask = (dist >= 0) & (dist <= WINDOW)
    o_win = [_masked_attend(qh[h], kw, vw, w_mask) for h in range(NSA_HPG)]

    n_steps = (start + qb + SEL_KEYS - 1) // SEL_KEYS
    blocks_per_step = SEL_KEYS // SEL_BLOCK
    expand = _block_expand(blocks_per_step, SEL_KEYS)

    def scores_step(j, m_run):
        k0 = pl.multiple_of(j * SEL_KEYS, SEL_KEYS)
        k_bf = kv_ref[0, pl.ds(k0, SEL_KEYS), :]
        key = k0 + lax.broadcasted_iota(jnp.int32, (1, SEL_KEYS), 1)
        chosen = sel_ref[pl.ds(pl.multiple_of(j * blocks_per_step, blocks_per_step), blocks_per_step), :]
        mask = (_dot_tn(chosen.astype(BF16), expand) > 0.5) & (key <= pos)
        out = []
        for h in range(NSA_HPG):
            s = jnp.where(mask, _dot_nt(qh[h], k_bf), NEG)
            s_ref[j, h] = s
            out.append(jnp.maximum(m_run[h], _lane_groups(s, jnp.maximum)))
        return tuple(out)

    m_run = lax.fori_loop(0, n_steps, scores_step, tuple(jnp.full((qb, LANES), NEG, F32) for _ in range(NSA_HPG)))
    m_fin = [jnp.max(m, axis=-1, keepdims=True) for m in m_run]

    def values_step(j, carry):
        k0 = pl.multiple_of(j * SEL_KEYS, SEL_KEYS)
        v_bf = kv_ref[1, pl.ds(k0, SEL_KEYS), :]
        out = []
        for h in range(NSA_HPG):
            l_run, acc = carry[h]
            p = jnp.exp(s_ref[j, h] - m_fin[h])
            out.append((l_run + _lane_groups(p, jnp.add), acc + _dot(p.astype(BF16), v_bf)))
        return tuple(out)

    sel_state = lax.fori_loop(0, n_steps, values_step,
                              tuple((jnp.zeros((qb, LANES), F32), jnp.zeros((qb, NSA_DK), F32))
                                    for _ in range(NSA_HPG)))

    gates = jax.nn.sigmoid(gate_ref[...])
    outs = []
    for h in range(NSA_HPG):
        l_run, acc = sel_state[h]
        o_sel = acc / jnp.sum(l_run, axis=-1, keepdims=True)
        outs.append(gates[:, 3 * h:3 * h + 1] * o_cmp[h] + gates[:, 3 * h + 1:3 * h + 2] * o_sel
                    + gates[:, 3 * h + 2:3 * h + 3] * o_win[h])
    o_ref[...] = jnp.concatenate(outs, axis=1).astype(o_ref.dtype)


def nsa_prompt(z, kcvc, b_sz, t_len):
    assert t_len % SEL_KEYS == 0 and t_len >= WINDOW + QUERY_BLOCK
    nqb = t_len // QUERY_BLOCK
    n_half = t_len // CMP_STRIDE
    hd = NSA_HPG * NSA_DK

    def rows(col0):
        return pl.BlockSpec((t_len, NSA_DK), lambda b, g, i: (b, col0 // NSA_DK + g))

    def cmp_spec(kv):
        return pl.BlockSpec((None, None, None, n_half, NSA_DK), lambda b, g, i: (kv, b, g, 0, 0))

    return pl.pallas_call(
        functools.partial(_nsa_prompt_kernel, t_len=t_len),
        out_shape=jax.ShapeDtypeStruct((b_sz * t_len, NSA_HEADS * NSA_DK), BF16),
        grid=(b_sz, NSA_GROUPS, nqb),
        in_specs=[pl.BlockSpec((QUERY_BLOCK, hd), lambda b, g, i: (b * nqb + i, g)),
                  cmp_spec(0), cmp_spec(1),
                  rows(C_KS), rows(C_VS), rows(C_KW), rows(C_VW),
                  pl.BlockSpec((QUERY_BLOCK, LANES), lambda b, g, i: (b * nqb + i, C_TAIL // LANES + g))],
        out_specs=pl.BlockSpec((QUERY_BLOCK, hd), lambda b, g, i: (b * nqb + i, g)),
        scratch_shapes=[pltpu.VMEM((_prompt_block_rows(t_len), QUERY_BLOCK), F32),
                        pltpu.VMEM((t_len // SEL_KEYS, NSA_HPG, QUERY_BLOCK, SEL_KEYS), F32),
                        pltpu.VMEM((4, t_len, NSA_DK), BF16)],
        compiler_params=_params(("parallel", "parallel", "arbitrary")),
        name="nsa_prompt",
    )(z, kcvc, kcvc, z, z, z, z, z)


S_PAD = 8


def _nsa_sample_kernel(pt_ref, q_ref, kc_ref, vc_ref, kn_ref, vn_ref, kw_ref, vw_ref, kwn_ref, vwn_ref, gate_ref, *refs,
                       pages, past, s_len, n_cmp, ncp, n_blk, blk_rows, win_rows):
    kpages, vpages = refs[:pages], refs[pages:2 * pages]
    o_ref = refs[2 * pages]
    sel_ref, m_ref, l_ref, acc_ref, oc_ref = refs[2 * pages + 1:]
    j = pl.program_id(1)
    n_j = pl.num_programs(1)
    rows = NSA_HPG * s_len
    step = lax.broadcasted_iota(jnp.int32, (rows, 1), 0) % s_len
    pos = past + step
    qg = [(q_ref[g] * NSA_SCALE).astype(BF16) for g in range(NSA_GROUPS)]

    @pl.when(j == 0)
    def _():
        step_i = lax.broadcasted_iota(jnp.int32, (LANES, rows), 1) % s_len
        lane_i = lax.broadcasted_iota(jnp.int32, (LANES, rows), 0)
        step_o = lax.broadcasted_iota(jnp.int32, (rows, LANES), 0) % s_len
        lane_o = lax.broadcasted_iota(jnp.int32, (rows, LANES), 1)
        pos_row = past + lax.broadcasted_iota(jnp.int32, (1, LANES), 1) % S_PAD
        pg = jnp.zeros((LANES, ncp), F32)
        for g in range(NSA_GROUPS):
            kc = kc_ref[pl.ds(g, ncp, stride=NSA_GROUPS), :].astype(BF16)
            vc = vc_ref[pl.ds(g, ncp, stride=NSA_GROUPS), :].astype(BF16)
            p = _cmp_probs(_dot_nt(qg[g], kc), pos, n_cmp)
            oc_ref[g] = _dot(p.astype(BF16), vc)
            fold = (step_i + g * S_PAD == lane_i).astype(BF16)
            pg = pg + _dot_split(fold, p)
            m_ref[g] = jnp.full((rows, 1), NEG, F32)
            l_ref[g] = jnp.zeros((rows, 1), F32)
            acc_ref[g] = jnp.zeros((rows, NSA_DK), F32)
        sel_t = _select_blocks_t(_block_scores_t(pg, blk_rows), pos_row, n_blk).astype(BF16)
        for g in range(NSA_GROUPS):
            unfold = (step_o + g * S_PAD == lane_o).astype(BF16)
            sel_ref[g] = _dot_nt(sel_t, unfold)

    n_keys = pages * PAGE_SIZE
    blocks_per_step = n_keys // SEL_BLOCK
    expand = _block_expand(blocks_per_step, n_keys)
    key = j * n_keys + lax.broadcasted_iota(jnp.int32, (1, n_keys), 1)
    blk0 = pl.multiple_of(j * blocks_per_step, blocks_per_step)
    for g in range(NSA_GROUPS):
        k_bf = jnp.concatenate([kpages[p][pl.ds(g, PAGE_SIZE, stride=NSA_GROUPS), :] for p in range(pages)],
                               axis=0).astype(BF16)
        v_bf = jnp.concatenate([vpages[p][pl.ds(g, PAGE_SIZE, stride=NSA_GROUPS), :] for p in range(pages)],
                               axis=0).astype(BF16)
        chosen = _dot_tn(sel_ref[g, pl.ds(blk0, blocks_per_step), :].astype(BF16), expand) > 0.5
        m, l, acc = _online_update((m_ref[g], l_ref[g], acc_ref[g]), _dot_nt(qg[g], k_bf),
                                   chosen & (key <= pos), v_bf)
        m_ref[g] = m
        l_ref[g] = l
        acc_ref[g] = acc

    @pl.when(j == n_j - 1)
    def _():
        gates = jax.nn.sigmoid(gate_ref[...])
        new_i = lax.broadcasted_iota(jnp.int32, (1, S_PAD), 1)
        new_key = past + new_i
        new_blk = past // SEL_BLOCK
        first_row = (lax.broadcasted_iota(jnp.int32, (8, S_PAD), 0) == 0).astype(BF16)
        dist = pos - (past - win_rows + lax.broadcasted_iota(jnp.int32, (1, win_rows), 1))
        w_mask = (dist >= 0) & (dist <= WINDOW)
        n_mask = (new_key <= pos) & (pos - new_key <= WINDOW) & (new_i < s_len)
        for g in range(NSA_GROUPS):
            chosen_new = _dot_tn(sel_ref[g, new_blk:new_blk + 8, :].astype(BF16), first_row) > 0.5
            mask = chosen_new & (new_key <= pos) & (new_i < s_len)
            _, l, acc = _online_update((m_ref[g], l_ref[g], acc_ref[g]),
                                       _dot_nt(qg[g], kn_ref[g].astype(BF16)), mask, vn_ref[g].astype(BF16))
            o_sel = acc / l
            kw = kw_ref[pl.ds(g, win_rows, stride=NSA_GROUPS), :].astype(BF16)
            vw = vw_ref[pl.ds(g, win_rows, stride=NSA_GROUPS), :].astype(BF16)
            s_old = jnp.where(w_mask, _dot_nt(qg[g], kw), NEG)
            s_new = jnp.where(n_mask, _dot_nt(qg[g], kwn_ref[g].astype(BF16)), NEG)
            m_w = jnp.maximum(jnp.max(s_old, axis=-1, keepdims=True), jnp.max(s_new, axis=-1, keepdims=True))
            e_old = jnp.where(w_mask, jnp.exp(s_old - m_w), 0.0)
            e_new = jnp.where(n_mask, jnp.exp(s_new - m_w), 0.0)
            l_w = jnp.sum(e_old, axis=-1, keepdims=True) + jnp.sum(e_new, axis=-1, keepdims=True)
            o_win = (_dot(e_old.astype(BF16), vw) + _dot(e_new.astype(BF16), vwn_ref[g].astype(BF16))) / l_w
            gt = gates[g]
            o_ref[g] = gt[:, 0:1] * oc_ref[g] + gt[:, 1:2] * o_sel + gt[:, 2:3] * o_win


def nsa_sample(q, kc, vc, k_new, v_new, kw_old, vw_old, kw_new, vw_new, gate, pool_k, pool_v, page_table, s_len,
               pages):
    bd, n_pages = page_table.shape
    past = n_pages * PAGE_SIZE
    rows = NSA_HPG * s_len
    n_cmp = past // CMP_STRIDE - 1
    n_blk = max(-(-(past + s_len) // SEL_BLOCK), N_SEL)
    assert s_len <= S_PAD and s_len < CMP_STRIDE and n_pages % pages == 0
    assert (past // SEL_BLOCK) % 8 == 0 and (pages * PAGE_SIZE // SEL_BLOCK) % 8 == 0
    blk_rows = -(-(past // SEL_BLOCK + 8) // LANES) * LANES
    ncp = kc.shape[1] // NSA_GROUPS
    win_rows = kw_old.shape[1] // NSA_GROUPS
    new_spec = pl.BlockSpec((None, NSA_GROUPS, S_PAD, NSA_DK), lambda b, j, pt: (b, 0, 0, 0))

    def per_b(shape):
        return pl.BlockSpec((None,) + shape, lambda b, j, pt: (b,) + (0,) * len(shape))

    def page_spec(i):
        return pl.BlockSpec((None, PAGE_SIZE * NSA_GROUPS, NSA_DK), lambda b, j, pt: (pt[b, j * pages + i], 0, 0))

    grid_spec = pltpu.PrefetchScalarGridSpec(
        num_scalar_prefetch=1,
        grid=(bd, n_pages // pages),
        in_specs=[per_b((NSA_GROUPS, rows, NSA_DK)), per_b((ncp * NSA_GROUPS, NSA_DK)), per_b((ncp * NSA_GROUPS, NSA_DK)),
                  new_spec, new_spec,
                  per_b((win_rows * NSA_GROUPS, NSA_DK)), per_b((win_rows * NSA_GROUPS, NSA_DK)),
                  new_spec, new_spec,
                  per_b((NSA_GROUPS, rows, LANES))]
        + [page_spec(i) for i in range(pages)] * 2,
        out_specs=per_b((NSA_GROUPS, rows, NSA_DK)),
        scratch_shapes=[pltpu.VMEM((NSA_GROUPS, blk_rows, rows), F32),
                        pltpu.VMEM((NSA_GROUPS, rows, 1), F32),
                        pltpu.VMEM((NSA_GROUPS, rows, 1), F32),
                        pltpu.VMEM((NSA_GROUPS, rows, NSA_DK), F32),
                        pltpu.VMEM((NSA_GROUPS, rows, NSA_DK), F32)],
    )
    return pl.pallas_call(
        functools.partial(_nsa_sample_kernel, pages=pages, past=past, s_len=s_len, n_cmp=n_cmp, ncp=ncp, n_blk=n_blk,
                          blk_rows=blk_rows, win_rows=win_rows),
        out_shape=jax.ShapeDtypeStruct((bd, NSA_GROUPS, rows, NSA_DK), F32),
        grid_spec=grid_spec,
        compiler_params=_params(("parallel", "arbitrary")),
        name="nsa_sample",
    )(page_table, q, kc, vc, k_new, v_new, kw_old, vw_old, kw_new, vw_new, gate,
      *([pool_k] * pages), *([pool_v] * pages))


GLA_HEADS_PER_STEP = 2


def _gla_kernel(q_ref, k_ref, v_ref, gr_ref, lr_ref, wlr_ref, blr_ref, ng_ref, s0_ref, o_ref, sT_out_ref,
                st_ref, *, chunk, t_valid):
    c = pl.program_id(2)
    row = lax.broadcasted_iota(jnp.int32, (chunk, 1), 0)
    tri = (lax.broadcasted_iota(jnp.int32, (chunk, chunk), 1)
           <= lax.broadcasted_iota(jnp.int32, (chunk, chunk), 0)).astype(BF16)
    lr = lr_ref[...].astype(BF16)

    @pl.when(c == 0)
    def _():
        for hh in range(GLA_HEADS_PER_STEP):
            st_ref[hh] = s0_ref[hh].T

    states = []
    for hh in range(GLA_HEADS_PER_STEP):
        dk = slice(hh * GLA_DK, (hh + 1) * GLA_DK)
        dv = slice(hh * GLA_DV, (hh + 1) * GLA_DV)
        x = _dot(lr, wlr_ref[hh]) + blr_ref[hh]
        log_a = jnp.where(row < t_valid, jax.nn.log_sigmoid(x) / GLA_TAU, 0.0)
        b = _dot_split(tri, log_a)
        qs = q_ref[:, dk] * (GLA_DK ** -0.5)
        k = k_ref[:, dk]
        a_t = jnp.zeros((chunk, LANES), F32)
        for t in range(min(chunk, t_valid)):
            n_s = -(-(t + 1) // 8) * 8
            seen = lax.broadcasted_iota(jnp.int32, (n_s, GLA_DK), 0) <= t
            w = jnp.where(seen, jnp.exp(jnp.where(seen, b[t:t + 1] - b[0:n_s], 0.0)), 0.0)
            col = jnp.sum(k[0:n_s] * w * qs[t:t + 1], axis=-1, keepdims=True)
            filled = jnp.where(lax.broadcasted_iota(jnp.int32, (n_s, LANES), 1) == t, col, a_t[0:n_s])
            a_t = filled if n_s == chunk else jnp.concatenate([filled, a_t[n_s:]], axis=0)
        v_bf = v_ref[:, dv].astype(BF16)
        st = st_ref[hh]
        o = (_dot_tn(a_t[:, 0:chunk].astype(BF16), v_bf)
             + _dot_nt((qs * jnp.exp(b)).astype(BF16), st.astype(BF16)))
        b_last = b[chunk - 1:chunk, :]
        k_dec = (k * jnp.exp(b_last - b)).astype(BF16)
        st_new = st * jnp.exp(b_last) + _dot_tn(v_bf, k_dec)
        st_ref[hh] = st_new
        states.append(st_new)
        gr = gr_ref[:, dv]
        o_ref[:, dv] = (_rms(o, ng_ref[...]) * (gr * jax.nn.sigmoid(gr))).astype(o_ref.dtype)

    @pl.when(c == pl.num_programs(2) - 1)
    def _():
        for hh in range(GLA_HEADS_PER_STEP):
            sT_out_ref[hh] = states[hh].T


def gla(z, wlr, blr, norm_g, s0, b_sz, t_len, chunk, t_valid):
    assert chunk <= LANES and chunk % 8 == 0 and t_len % chunk == 0 and GLA_HEADS % GLA_HEADS_PER_STEP == 0
    nck = t_len // chunk
    hp = GLA_HEADS_PER_STEP

    def seg(col0, width):
        assert col0 % (hp * width) == 0
        return pl.BlockSpec((chunk, hp * width), lambda b, h, c: (b * nck + c, col0 // (hp * width) + h))

    return pl.pallas_call(
        functools.partial(_gla_kernel, chunk=chunk, t_valid=t_valid),
        out_shape=(jax.ShapeDtypeStruct((b_sz * t_len, GLA_HEADS * GLA_DV), BF16),
                   jax.ShapeDtypeStruct((b_sz, GLA_HEADS, GLA_DK, GLA_DV), F32)),
        grid=(b_sz, GLA_HEADS // hp, nck),
        in_specs=[seg(C_GQ, GLA_DK), seg(C_GK, GLA_DK), seg(C_GV, GLA_DV), seg(C_GR, GLA_DV),
                  pl.BlockSpec((chunk, LANES), lambda b, h, c: (b * nck + c, C_TAIL // LANES)),
                  pl.BlockSpec((hp, LANES, GLA_DK), lambda b, h, c: (h, 0, 0)),
                  pl.BlockSpec((hp, 1, GLA_DK), lambda b, h, c: (h, 0, 0)),
                  pl.BlockSpec((1, GLA_DV), lambda b, h, c: (0, 0)),
                  pl.BlockSpec((None, hp, GLA_DK, GLA_DV), lambda b, h, c: (b, h, 0, 0))],
        out_specs=(pl.BlockSpec((chunk, hp * GLA_DV), lambda b, h, c: (b * nck + c, h)),
                   pl.BlockSpec((None, hp, GLA_DK, GLA_DV), lambda b, h, c: (b, h, 0, 0))),
        scratch_shapes=[pltpu.VMEM((hp, GLA_DV, GLA_DK), F32)],
        compiler_params=_params(("parallel", "parallel", "arbitrary")),
        name="gla",
    )(z, z, z, z, z, wlr, blr, norm_g.reshape(1, GLA_DV), s0)


def _merge_kernel(on_ref, og_ref, wn_ref, wg_ref, ma_ref, mb_ref, y_ref):
    y = (jax.nn.sigmoid(ma_ref[...]) * _dot(on_ref[...], wn_ref[...])
         + jax.nn.sigmoid(mb_ref[...]) * _dot(og_ref[...], wg_ref[...]))
    y_ref[...] = y.astype(y_ref.dtype)


def merge(o_nsa, o_gla, w_nsa, w_gla, z, tm, tn):
    n = o_nsa.shape[0]
    return pl.pallas_call(
        _merge_kernel,
        out_shape=jax.ShapeDtypeStruct((n, D_MODEL), BF16),
        grid=(n // tm, D_MODEL // tn),
        in_specs=[pl.BlockSpec((tm, o_nsa.shape[1]), lambda i, j: (i, 0)),
                  pl.BlockSpec((tm, o_gla.shape[1]), lambda i, j: (i, 0)),
                  pl.BlockSpec((w_nsa.shape[0], tn), lambda i, j: (0, j)),
                  pl.BlockSpec((w_gla.shape[0], tn), lambda i, j: (0, j)),
                  pl.BlockSpec((tm, tn), lambda i, j: (i, C_MA // tn + j)),
                  pl.BlockSpec((tm, tn), lambda i, j: (i, C_MB // tn + j))],
        out_specs=pl.BlockSpec((tm, tn), lambda i, j: (i, j)),
        compiler_params=_params(("parallel", "arbitrary")),
        name="merge",
    )(o_nsa, o_gla, w_nsa, w_gla, z, z)


def _out_proj_kernel(y_ref, w_ref, x_ref, o_ref):
    o_ref[...] = x_ref[...] + _dot(y_ref[...], w_ref[...])


def out_proj(y, w_out, x, tm, tn):
    n = y.shape[0]
    return pl.pallas_call(
        _out_proj_kernel,
        out_shape=jax.ShapeDtypeStruct((n, D_MODEL), F32),
        grid=(n // tm, D_MODEL // tn),
        in_specs=[pl.BlockSpec((tm, D_MODEL), lambda i, j: (i, 0)),
                  pl.BlockSpec((D_MODEL, tn), lambda i, j: (0, j)),
                  pl.BlockSpec((tm, tn), lambda i, j: (i, j))],
        out_specs=pl.BlockSpec((tm, tn), lambda i, j: (i, j)),
        compiler_params=_params(("parallel", "arbitrary")),
        name="out_proj",
    )(y, w_out, x)


def _top_values(s, count):
    vals = []
    for _ in range(count):
        m = jnp.max(s, axis=0, keepdims=True)
        s = jnp.where(s == m, REMOVED, s)
        vals.append(m)
    return vals


def _peer_scores_kernel(q_ref, k1_ref, k2_ref, s1_ref, s2_ref, e1_ref, e2_ref, tau_ref):
    taus = []
    for h in range(PEER_HEADS):
        q1 = q_ref[:, h * 2 * PEER_HALF:h * 2 * PEER_HALF + PEER_HALF].astype(BF16)
        q2 = q_ref[:, h * 2 * PEER_HALF + PEER_HALF:(h + 1) * 2 * PEER_HALF].astype(BF16)
        s1 = _dot_nt(k1_ref[h], q1)
        s2 = _dot_nt(k2_ref[h], q2)
        v1 = _top_values(s1, PEER_TOPK)
        v2 = _top_values(s2, PEER_TOPK)
        v2_all = jnp.concatenate(v2, axis=0)
        cand = jnp.concatenate([v1[i] + v2_all for i in range(PEER_TOPK)], axis=0)
        top = _top_values(cand, PEER_TOPK)
        z = sum(jnp.exp(t - top[0]) for t in top)
        s1_ref[h] = s1
        s2_ref[h] = s2
        e1_ref[h] = jnp.exp(s1 - v1[0])
        e2_ref[h] = jnp.exp(s2 - v2[0]) / z
        taus.append(top[PEER_TOPK - 1])
    tau_ref[...] = jnp.concatenate(taus, axis=0)


def peer_scores(qp, keys1, keys2, tn):
    n = qp.shape[0]
    big = jax.ShapeDtypeStruct((PEER_HEADS, PEER_NKEYS, n), F32)
    big_spec = pl.BlockSpec((PEER_HEADS, PEER_NKEYS, tn), lambda i: (0, 0, i))
    key_spec = pl.BlockSpec((PEER_HEADS, PEER_NKEYS, PEER_HALF), lambda i: (0, 0, 0))
    return pl.pallas_call(
        _peer_scores_kernel,
        out_shape=(big, big, big, big, jax.ShapeDtypeStruct((PEER_HEADS, n), F32)),
        grid=(n // tn,),
        in_specs=[pl.BlockSpec((tn, qp.shape[1]), lambda i: (i, 0)), key_spec, key_spec],
        out_specs=(big_spec, big_spec, big_spec, big_spec, pl.BlockSpec((PEER_HEADS, tn), lambda i: (0, i))),
        compiler_params=_params(("parallel",)),
        name="peer_scores",
    )(qp, keys1, keys2)


PEER_I1_PER_TILE = 8


def _peer_dense_kernel(x_ref, g2_ref, gf_ref, u_ref, v_ref, s1_ref, e1_ref, s2_ref, e2_ref, tau_ref, o_ref,
                       h_ref, acc_ref):
    e = pl.program_id(1)

    @pl.when(e == 0)
    def _():
        h_ref[...] = _rms(x_ref[...], g2_ref[...]).astype(BF16)
        acc_ref[...] = jnp.zeros_like(acc_ref)

    pre = _dot_nt(u_ref[...], h_ref[...])
    n_tok = pre.shape[1]
    tile = min(LANES, n_tok)
    parts = []
    for c in range(PEER_I1_PER_TILE):
        cols = []
        for t0 in range(0, n_tok, tile):
            tok = slice(t0, t0 + tile)
            w = jnp.zeros((PEER_NKEYS, tile), F32)
            for h in range(PEER_HEADS):
                keep = s1_ref[h, c:c + 1, tok] + s2_ref[h, :, tok] >= tau_ref[h:h + 1, tok]
                w = w + jnp.where(keep, e1_ref[h, c:c + 1, tok] * e2_ref[h, :, tok], 0.0)
            cols.append(w * _gelu_tanh(pre[c * PEER_NKEYS:(c + 1) * PEER_NKEYS, tok]))
        parts.append(jnp.concatenate(cols, axis=1).astype(BF16))
    acc_ref[...] += _dot_tn(jnp.concatenate(parts, axis=0), v_ref[...])

    @pl.when(e == pl.num_programs(1) - 1)
    def _():
        o_ref[...] = _rms(x_ref[...] + acc_ref[...], gf_ref[...])


def peer_dense(x, g2, gf, u_bf, v_bf, s1, s2, e1, e2, tau, tn):
    n = x.shape[0]
    te = PEER_I1_PER_TILE * PEER_NKEYS
    n_exp = u_bf.shape[0]
    sub = pl.BlockSpec((PEER_HEADS, PEER_I1_PER_TILE, tn), lambda i, e: (0, e, i))
    full = pl.BlockSpec((PEER_HEADS, PEER_NKEYS, tn), lambda i, e: (0, 0, i))
    vec = pl.BlockSpec((1, D_MODEL), lambda i, e: (0, 0))
    return pl.pallas_call(
        _peer_dense_kernel,
        out_shape=jax.ShapeDtypeStruct((n, D_MODEL), F32),
        grid=(n // tn, n_exp // te),
        in_specs=[pl.BlockSpec((tn, D_MODEL), lambda i, e: (i, 0)), vec, vec,
                  pl.BlockSpec((te, D_MODEL), lambda i, e: (e, 0)),
                  pl.BlockSpec((te, D_MODEL), lambda i, e: (e, 0)),
                  sub, sub, full, full,
                  pl.BlockSpec((PEER_HEADS, tn), lambda i, e: (0, i))],
        out_specs=pl.BlockSpec((tn, D_MODEL), lambda i, e: (i, 0)),
        scratch_shapes=[pltpu.VMEM((tn, D_MODEL), BF16), pltpu.VMEM((tn, D_MODEL), F32)],
        compiler_params=_params(("parallel", "arbitrary")),
        name="peer_dense",
    )(x, g2.reshape(1, D_MODEL), gf.reshape(1, D_MODEL), u_bf, v_bf, s1, e1, s2, e2, tau)


PACK_ROWS = Z_COLS - C_TAIL


def _pack_w_in_kernel(ia_ref, ib_ref, kind_ref, a_ref, b_ref, o_ref, *, shifts, gate_row0, per_group, glr_row0):
    kind = kind_ref[pl.program_id(0)]
    for k, s in enumerate(shifts):
        @pl.when(kind == k)
        def _(s=s):
            src = a_ref[...] if s == 0 else jnp.concatenate([a_ref[s:, :], b_ref[:s, :]], axis=0)
            o_ref[...] = src.astype(BF16)

    @pl.when(kind == len(shifts))
    def _():
        out_row = lax.broadcasted_iota(jnp.int32, (PACK_ROWS, PACK_ROWS), 0)
        src_row = lax.broadcasted_iota(jnp.int32, (PACK_ROWS, PACK_ROWS), 1)
        group, r = out_row // LANES, out_row % LANES
        pick = (src_row == gate_row0 + group * per_group + r) & (r < per_group)
        out = _dot(pick.astype(BF16), a_ref[...].astype(BF16))
        pick = (src_row == glr_row0 + out_row - GLR_LANE) & (out_row >= GLR_LANE) & (out_row < GLR_LANE + GLA_RANK)
        out = out + _dot(pick.astype(BF16), b_ref[...].astype(BF16))
        o_ref[...] = out.astype(BF16)


def _pack_w_in(w_in):
    offs = np.concatenate([[0], np.cumsum(IN_SPLITS)]).astype(np.int64)
    plain = (0, 1, 2, 3, 4, 5, 6, 8, 9, 10, 11, 13, 14)
    src_start = np.concatenate([np.arange(offs[i], offs[i + 1], PACK_ROWS) for i in plain])
    assert all((offs[i + 1] - offs[i]) % PACK_ROWS == 0 for i in plain) and len(src_start) * PACK_ROWS == C_TAIL
    shifts = tuple(sorted(set(int(s) for s in src_start % PACK_ROWS)))
    gate_col, glr_col = int(offs[7]), int(offs[12])
    per_group = 3 * NSA_HPG
    assert gate_col % PACK_ROWS + NSA_GROUPS * per_group <= PACK_ROWS and PACK_ROWS == NSA_GROUPS * LANES
    ia = np.concatenate([src_start // PACK_ROWS, [gate_col // PACK_ROWS]])
    ib = np.concatenate([src_start // PACK_ROWS + 1, [glr_col // PACK_ROWS]])
    kind = np.concatenate([[shifts.index(int(s)) for s in src_start % PACK_ROWS], [len(shifts)]])
    assert all(s % 8 == 0 for s in shifts)
    d = w_in.shape[0]
    w_t = w_in.T
    grid_spec = pltpu.PrefetchScalarGridSpec(
        num_scalar_prefetch=3,
        grid=(Z_COLS // PACK_ROWS,),
        in_specs=[pl.BlockSpec((PACK_ROWS, d), lambda c, ia, ib, kind: (ia[c], 0)),
                  pl.BlockSpec((PACK_ROWS, d), lambda c, ia, ib, kind: (ib[c], 0))],
        out_specs=pl.BlockSpec((PACK_ROWS, d), lambda c, ia, ib, kind: (c, 0)),
    )
    return pl.pallas_call(
        functools.partial(_pack_w_in_kernel, shifts=shifts, gate_row0=gate_col % PACK_ROWS, per_group=per_group,
                          glr_row0=glr_col % PACK_ROWS),
        out_shape=jax.ShapeDtypeStruct((Z_COLS, d), BF16),
        grid_spec=grid_spec,
        compiler_params=_params(("arbitrary",)),
        name="pack_w_in",
    )(jnp.asarray(ia, jnp.int32), jnp.asarray(ib, jnp.int32), jnp.asarray(kind, jnp.int32), w_t, w_t)


def _pack_cmp(pe, w1, w2):
    w1cat = jnp.concatenate([w1[:CMP_STRIDE].reshape(CMP_STRIDE * NSA_DK, CMP_HIDDEN),
                             w1[CMP_STRIDE:].reshape(CMP_STRIDE * NSA_DK, CMP_HIDDEN)], axis=1).astype(BF16)
    return (w1cat, pe.reshape(1, CMP_LEN * NSA_DK).astype(BF16),
            w1.reshape(CMP_LEN * NSA_DK, CMP_HIDDEN).astype(BF16), w2.astype(BF16))


def _row_tile(n, cap):
    t = min(n, cap)
    while n % t:
        t //= 2
    return t


def _channel_tail(x2d, z, o_nsa, o_gla, wts):
    n = x2d.shape[0]
    tm = _row_tile(n, 1024)
    y = merge(o_nsa, o_gla, wts["w_nsa"], wts["w_gla"], z, tm, 512)
    x1 = out_proj(y, wts["w_out"], x2d, tm, 512)
    qp = norm_matmul(x1, wts["norm2_g"], wts["w_q"], tm, 512)
    tn = _row_tile(n, 256)
    s1, s2, e1, e2, tau = peer_scores(qp, wts["keys1"], wts["keys2"], tn)
    return peer_dense(x1, wts["norm2_g"], wts["norm_f_g"], wts["u"], wts["v"], s1, s2, e1, e2, tau, _row_tile(n, 512))


def kernel(x_prompt, x_sample, cache_k_cmp, cache_v_cmp, cache_k_slc, cache_v_slc, cache_k_win, cache_v_win, state_gla, page_table, norm1_g, w_in, cmp_pe_k, cmp_w1_k, cmp_w2_k, cmp_pe_v, cmp_w1_v, cmp_w2_v, gla_w_lr2, gla_b_lr, gla_norm_g, w_nsa_proj, w_gla_proj, w_out, norm2_g, peer_w_q, peer_keys1, peer_keys2, peer_u, peer_v, norm_f_g):
    b_sz, t_len, _ = x_prompt.shape
    bd, s_len, _ = x_sample.shape
    n_pool = cache_k_cmp.shape[0]
    wb = cache_k_win.shape[1]

    w_pack = _pack_w_in(w_in)
    cmp_k = _pack_cmp(cmp_pe_k, cmp_w1_k, cmp_w2_k)
    cmp_v = _pack_cmp(cmp_pe_v, cmp_w1_v, cmp_w2_v)
    cmp_kv = [jnp.stack([a, b]) for a, b in zip(cmp_k, cmp_v)]
    wlr = jnp.zeros((LANES, GLA_HEADS * GLA_DK), F32).at[GLR_LANE:GLR_LANE + GLA_RANK].set(gla_w_lr2)
    wlr = wlr.reshape(LANES, GLA_HEADS, GLA_DK).transpose(1, 0, 2).astype(BF16)
    blr = gla_b_lr.reshape(GLA_HEADS, 1, GLA_DK)
    wts = dict(w_nsa=w_nsa_proj.astype(BF16), w_gla=w_gla_proj.astype(BF16), w_out=w_out.astype(BF16),
               w_q=peer_w_q.astype(BF16), norm2_g=norm2_g, norm_f_g=norm_f_g,
               keys1=peer_keys1.astype(BF16), keys2=peer_keys2.astype(BF16),
               u=peer_u.astype(BF16), v=peer_v.astype(BF16))

    n_p = b_sz * t_len
    xp = x_prompt.reshape(n_p, D_MODEL)
    z_p = norm_matmul(xp, norm1_g, w_pack, _row_tile(n_p, 1024), GD, w_rows_are_outputs=True)
    heads_p = split_heads(z_p, _row_tile(n_p, 1024))
    fs_p = half_proj_dense(z_p, cmp_kv[0], b_sz, t_len)
    kcvc_p = compress_finish(fs_p, cmp_kv[1], cmp_kv[2], cmp_kv[3], 1)
    o_nsa_p = nsa_prompt(z_p, kcvc_p, b_sz, t_len)
    s0 = jnp.zeros((b_sz, GLA_HEADS, GLA_DK, GLA_DV), F32)
    o_gla_p, gla_state_p = gla(z_p, wlr, blr, gla_norm_g, s0, b_sz, t_len, GLA_CHUNK, GLA_CHUNK)
    y_prompt = _channel_tail(xp, z_p, o_nsa_p, o_gla_p, wts).reshape(b_sz, t_len, D_MODEL)
    kcr_p, vcr_p, ksr_p, vsr_p, kwr_p, vwr_p = (a.reshape(b_sz, t_len, NSA_GROUPS, NSA_DK) for a in heads_p)
    wl = min(WINDOW, t_len)
    k_win_p = kwr_p[:, t_len - wl:]
    v_win_p = vwr_p[:, t_len - wl:]

    n_s = bd * s_len
    xs = x_sample.reshape(n_s, D_MODEL)
    z_s = norm_matmul(xs, norm1_g, w_pack, _row_tile(n_s, 1024), GD, w_rows_are_outputs=True)
    heads_s = split_heads(z_s, _row_tile(n_s, 1024))
    kcr_s, vcr_s, ksr_s, vsr_s, kwr_s, vwr_s = (a.reshape(bd, s_len, NSA_GROUPS, NSA_DK) for a in heads_s)

    pool2d = lambda p: p.reshape(n_pool, PAGE_SIZE * NSA_GROUPS, NSA_DK)
    def compress_pool(pool, prm):
        fs = half_proj_paged(pool2d(pool), page_table, prm[0], min(64, page_table.shape[1]))
        return compress_finish(fs[None, :, None], prm[1][None], prm[2][None], prm[3][None], NSA_GROUPS)[0, :, 0]

    kc_s = compress_pool(cache_k_cmp, cmp_k)
    vc_s = compress_pool(cache_v_cmp, cmp_v)

    assert wb >= s_len
    k_win_s = jnp.concatenate([cache_k_win[:, s_len:], kwr_s], axis=1)
    v_win_s = jnp.concatenate([cache_v_win[:, s_len:], vwr_s], axis=1)
    win_2d = lambda a: a.reshape(bd, wb * NSA_GROUPS, NSA_DK)

    def new_rows(a):
        return jnp.pad(a.transpose(0, 2, 1, 3), ((0, 0), (0, 0), (0, S_PAD - s_len), (0, 0)))

    rows = NSA_HPG * s_len
    q_s = z_s[:, C_Q:C_Q + NSA_HEADS * NSA_DK].reshape(bd, s_len, NSA_GROUPS, NSA_HPG, NSA_DK)
    q_s = q_s.transpose(0, 2, 3, 1, 4).reshape(bd, NSA_GROUPS, rows, NSA_DK)
    gate_s = z_s[:, C_TAIL:C_TAIL + 512].reshape(bd, s_len, NSA_GROUPS, LANES)[..., :3 * NSA_HPG]
    gate_s = gate_s.reshape(bd, s_len, NSA_GROUPS, NSA_HPG, 3).transpose(0, 2, 3, 1, 4).reshape(bd, NSA_GROUPS, rows, 3)
    gate_s = jnp.pad(gate_s, ((0, 0), (0, 0), (0, 0), (0, LANES - 3)))
    o_s = nsa_sample(q_s, kc_s, vc_s, new_rows(ksr_s), new_rows(vsr_s), win_2d(cache_k_win), win_2d(cache_v_win),
                     new_rows(kwr_s), new_rows(vwr_s), gate_s, pool2d(cache_k_slc), pool2d(cache_v_slc), page_table,
                     s_len, min(32, page_table.shape[1]))
    o_nsa_s = o_s.reshape(bd, NSA_GROUPS, NSA_HPG, s_len, NSA_DK).transpose(0, 3, 1, 2, 4)
    o_nsa_s = o_nsa_s.reshape(n_s, NSA_HEADS * NSA_DK).astype(BF16)

    z_s_pad = jnp.pad(z_s.reshape(bd, s_len, Z_COLS), ((0, 0), (0, S_PAD - s_len), (0, 0))).reshape(bd * S_PAD, Z_COLS)
    o_gla_s, gla_state_s = gla(z_s_pad, wlr, blr, gla_norm_g, state_gla, bd, S_PAD, S_PAD, s_len)
    o_gla_s = o_gla_s.reshape(bd, S_PAD, GLA_HEADS * GLA_DV)[:, :s_len].reshape(n_s, GLA_HEADS * GLA_DV)
    y_sample = _channel_tail(xs, z_s, o_nsa_s, o_gla_s, wts).reshape(bd, s_len, D_MODEL)

    return (y_prompt, y_sample, kcr_p, vcr_p, ksr_p, vsr_p, k_win_p, v_win_p, gla_state_p,
            kcr_s, vcr_s, ksr_s, vsr_s, k_win_s, v_win_s, gla_state_s)
```

```python
import functools

import numpy as np
import jax
import jax.numpy as jnp
from jax import lax
from jax.experimental import pallas as pl
from jax.experimental.pallas import tpu as pltpu

F32 = jnp.float32
BF16 = jnp.bfloat16

D_MODEL = 2048
PAGE_SIZE = 128
NSA_HEADS = 16
NSA_GROUPS = 4
NSA_HPG = NSA_HEADS // NSA_GROUPS
NSA_DK = 128
NSA_SCALE = NSA_DK ** -0.5
CMP_LEN = 32
CMP_STRIDE = 16
CMP_HIDDEN = 256
SEL_BLOCK = 64
N_SEL = 16
WINDOW = 512
QUERY_BLOCK = 256
GLA_HEADS = 4
GLA_DK = 256
GLA_DV = 512
GLA_RANK = 16
GLA_TAU = 16.0
GLA_CHUNK = 64
PEER_HEADS = 8
PEER_NKEYS = 128
PEER_HALF = 128
PEER_TOPK = 16
NORM_EPS = 1e-6
NEG = -1e30
BIG = 1e30
PAD_SCORE = -3e38
REMOVED = -float("inf")

IN_SPLITS = (NSA_HEADS * NSA_DK,) + (NSA_GROUPS * NSA_DK,) * 6 + (
    3 * NSA_HEADS, GLA_HEADS * GLA_DK, GLA_HEADS * GLA_DK, GLA_HEADS * GLA_DV,
    GLA_HEADS * GLA_DV, GLA_RANK, D_MODEL, D_MODEL)

GD = NSA_GROUPS * NSA_DK
C_Q = 0
C_KC = C_Q + NSA_HEADS * NSA_DK
C_VC = C_KC + GD
C_KS = C_VC + GD
C_VS = C_KS + GD
C_KW = C_VS + GD
C_VW = C_KW + GD
C_GQ = C_VW + GD
C_GK = C_GQ + GLA_HEADS * GLA_DK
C_GV = C_GK + GLA_HEADS * GLA_DK
C_GR = C_GV + GLA_HEADS * GLA_DV
C_MA = C_GR + GLA_HEADS * GLA_DV
C_MB = C_MA + D_MODEL
C_TAIL = C_MB + D_MODEL
Z_COLS = C_TAIL + 512
GLR_LANE = 16
LANES = 128

VMEM_LIMIT = 56 * 1024 * 1024


def _params(sem):
    return pltpu.CompilerParams(dimension_semantics=sem, vmem_limit_bytes=VMEM_LIMIT)


def _dot(a, b):
    return jnp.dot(a, b, preferred_element_type=F32)


def _dot_nt(a, b):
    return lax.dot_general(a, b, (((1,), (1,)), ((), ())), preferred_element_type=F32)


def _dot_tn(a, b):
    return lax.dot_general(a, b, (((0,), (0,)), ((), ())), preferred_element_type=F32)


def _rms(x, g):
    return x * lax.rsqrt(jnp.mean(x * x, axis=-1, keepdims=True) + NORM_EPS) * g


GELU_C0 = 2.0 * (2.0 / np.pi) ** 0.5
GELU_C1 = GELU_C0 * 0.044715


def _gelu_tanh(x):
    return x / (1.0 + jnp.exp(-(x * (GELU_C0 + GELU_C1 * (x * x)))))


def _bf16_pieces(x):
    hi = x.astype(BF16)
    rest = x - hi.astype(F32)
    mid = rest.astype(BF16)
    lo = (rest - mid.astype(F32)).astype(BF16)
    return hi, mid, lo


def _dot_nt_split(small_ints_bf16, x):
    return sum(_dot_nt(small_ints_bf16, piece) for piece in _bf16_pieces(x))


def _dot_split(small_ints_bf16, x):
    return sum(_dot(small_ints_bf16, piece) for piece in _bf16_pieces(x))


def _norm_matmul_kernel(x_ref, g_ref, w_ref, o_ref, h_ref, *, w_rows_are_outputs):
    @pl.when(pl.program_id(1) == 0)
    def _():
        h_ref[...] = _rms(x_ref[...], g_ref[...]).astype(BF16)

    o_ref[...] = (_dot_nt if w_rows_are_outputs else _dot)(h_ref[...], w_ref[...])


def norm_matmul(x, g, w, tm, tn, w_rows_are_outputs=False):
    n, d = x.shape
    cols = w.shape[0] if w_rows_are_outputs else w.shape[1]
    w_spec = (pl.BlockSpec((tn, d), lambda i, j: (j, 0)) if w_rows_are_outputs
              else pl.BlockSpec((d, tn), lambda i, j: (0, j)))
    return pl.pallas_call(
        functools.partial(_norm_matmul_kernel, w_rows_are_outputs=w_rows_are_outputs),
        out_shape=jax.ShapeDtypeStruct((n, cols), F32),
        grid=(n // tm, cols // tn),
        in_specs=[pl.BlockSpec((tm, d), lambda i, j: (i, 0)),
                  pl.BlockSpec((1, d), lambda i, j: (0, 0)),
                  w_spec],
        out_specs=pl.BlockSpec((tm, tn), lambda i, j: (i, j)),
        scratch_shapes=[pltpu.VMEM((tm, d), BF16)],
        compiler_params=_params(("parallel", "arbitrary")),
        name="norm_matmul",
    )(x, g.reshape(1, d), w)


KV_SEGMENTS = 6


def _split_heads_kernel(z_ref, *head_refs):
    k = pl.program_id(1)
    for seg in range(KV_SEGMENTS):
        @pl.when(k == seg)
        def _(seg=seg):
            for g in range(NSA_GROUPS):
                head_refs[seg][:, g, :] = z_ref[:, g * NSA_DK:(g + 1) * NSA_DK]


def split_heads(z, tm):
    n = z.shape[0]
    heads = jax.ShapeDtypeStruct((n, NSA_GROUPS, NSA_DK), F32)
    return pl.pallas_call(
        _split_heads_kernel,
        out_shape=(heads,) * KV_SEGMENTS,
        grid=(n // tm, KV_SEGMENTS),
        in_specs=[pl.BlockSpec((tm, GD), lambda i, k: (i, C_KC // GD + k))],
        out_specs=(pl.BlockSpec((tm, NSA_GROUPS, NSA_DK), lambda i, k: (i, 0, 0)),) * KV_SEGMENTS,
        compiler_params=_params(("parallel", "arbitrary")),
        name="split_heads",
    )(z)


def _half_rows(ref, first_row, n_half, row_stride, lane0):
    parts = [ref[pl.ds(first_row + r * row_stride, n_half, stride=CMP_STRIDE * row_stride), pl.ds(lane0, NSA_DK)]
             for r in range(CMP_STRIDE)]
    return jnp.concatenate(parts, axis=1)


def _half_proj_dense_kernel(x_ref, w_ref, o_ref, *, n_half):
    o_ref[...] = _dot(_half_rows(x_ref, 0, n_half, 1, 0).astype(BF16), w_ref[...])


def half_proj_dense(z, w1cat, b_sz, t_len):
    n_half = t_len // CMP_STRIDE
    return pl.pallas_call(
        functools.partial(_half_proj_dense_kernel, n_half=n_half),
        out_shape=jax.ShapeDtypeStruct((2, b_sz, NSA_GROUPS, n_half, 2 * CMP_HIDDEN), F32),
        grid=(2, b_sz, NSA_GROUPS),
        in_specs=[pl.BlockSpec((t_len, NSA_DK), lambda kv, b, g: (b, C_KC // NSA_DK + kv * NSA_GROUPS + g)),
                  pl.BlockSpec((None, CMP_STRIDE * NSA_DK, 2 * CMP_HIDDEN), lambda kv, b, g: (kv, 0, 0))],
        out_specs=pl.BlockSpec((None, None, None, n_half, 2 * CMP_HIDDEN), lambda kv, b, g: (kv, b, g, 0, 0)),
        compiler_params=_params(("parallel", "parallel", "parallel")),
        name="half_proj_dense",
    )(z, w1cat)


HALVES_PER_PAGE = PAGE_SIZE // CMP_STRIDE


def _half_proj_paged_kernel(pt_ref, *refs, pages):
    page_refs, w_ref, o_ref = refs[:pages], refs[pages], refs[pages + 1]
    tile = 2 * NSA_GROUPS
    rows_per_half = CMP_STRIDE * NSA_GROUPS
    low = lax.broadcasted_iota(jnp.int32, (tile, NSA_DK), 0) < NSA_GROUPS
    acc = None
    for r2 in range(CMP_STRIDE // 2):
        even, odd = [], []
        for p in range(pages):
            for n in range(0, HALVES_PER_PAGE, 2):
                a = page_refs[p][pl.ds(n * rows_per_half + tile * r2, tile), :]
                b = page_refs[p][pl.ds((n + 1) * rows_per_half + tile * r2, tile), :]
                even.append(jnp.where(low, a, pltpu.roll(b, NSA_GROUPS, 0)))
                odd.append(jnp.where(low, pltpu.roll(a, NSA_GROUPS, 0), b))
        lhs = jnp.concatenate([jnp.concatenate(even, axis=0), jnp.concatenate(odd, axis=0)], axis=1).astype(BF16)
        part = _dot(lhs, w_ref[pl.ds(r2 * 2 * NSA_DK, 2 * NSA_DK), :])
        acc = part if acc is None else acc + part
    o_ref[...] = acc


def half_proj_paged(pool2d, page_table, w1cat_one, pages):
    bd, n_pages = page_table.shape
    n_half = n_pages * HALVES_PER_PAGE
    rows = pages * HALVES_PER_PAGE * NSA_GROUPS

    def page_spec(i):
        return pl.BlockSpec((None, PAGE_SIZE * NSA_GROUPS, NSA_DK),
                            lambda b, j, pt: (pt[b, j * pages + i], 0, 0))

    grid_spec = pltpu.PrefetchScalarGridSpec(
        num_scalar_prefetch=1,
        grid=(bd, n_pages // pages),
        in_specs=[page_spec(i) for i in range(pages)]
        + [pl.BlockSpec((CMP_STRIDE * NSA_DK, 2 * CMP_HIDDEN), lambda b, j, pt: (0, 0))],
        out_specs=pl.BlockSpec((None, rows, 2 * CMP_HIDDEN), lambda b, j, pt: (b, j, 0)),
    )
    return pl.pallas_call(
        functools.partial(_half_proj_paged_kernel, pages=pages),
        out_shape=jax.ShapeDtypeStruct((bd, n_half * NSA_GROUPS, 2 * CMP_HIDDEN), F32),
        grid_spec=grid_spec,
        compiler_params=_params(("parallel", "arbitrary")),
        name="half_proj_paged",
    )(page_table, *([pool2d] * pages), w1cat_one)


def _compress_finish_kernel(fs_ref, pe_ref, w1_ref, w2_ref, o_ref, *, n_rows, step):
    pe = jnp.broadcast_to(pe_ref[...], (8, CMP_LEN * NSA_DK))
    c = _dot(pe, w1_ref[...])[0:1]
    first = fs_ref[:, 0:CMP_HIDDEN]
    second_next = pltpu.roll(fs_ref[:, CMP_HIDDEN:2 * CMP_HIDDEN], n_rows - step, 0)
    hid = _gelu_tanh(first + second_next + c)
    out = _dot(hid.astype(BF16), w2_ref[...])
    row = lax.broadcasted_iota(jnp.int32, (n_rows, 1), 0)
    o_ref[...] = jnp.where(row < n_rows - step, out, 0.0)


def compress_finish(fs, pe, w1, w2, step):
    kv, b_sz, n_sets, n_rows, _ = fs.shape
    return pl.pallas_call(
        functools.partial(_compress_finish_kernel, n_rows=n_rows, step=step),
        out_shape=jax.ShapeDtypeStruct((kv, b_sz, n_sets, n_rows, NSA_DK), F32),
        grid=(kv, b_sz, n_sets),
        in_specs=[pl.BlockSpec((None, None, None, n_rows, 2 * CMP_HIDDEN), lambda k, b, g: (k, b, g, 0, 0)),
                  pl.BlockSpec((None, 1, CMP_LEN * NSA_DK), lambda k, b, g: (k, 0, 0)),
                  pl.BlockSpec((None, CMP_LEN * NSA_DK, CMP_HIDDEN), lambda k, b, g: (k, 0, 0)),
                  pl.BlockSpec((None, CMP_HIDDEN, NSA_DK), lambda k, b, g: (k, 0, 0))],
        out_specs=pl.BlockSpec((None, None, None, n_rows, NSA_DK), lambda k, b, g: (k, b, g, 0, 0)),
        compiler_params=_params(("parallel", "parallel", "parallel")),
        name="compress_finish",
    )(fs, pe, w1, w2)


def _cmp_probs(s, pos, n_cmp):
    c = lax.broadcasted_iota(jnp.int32, (1, s.shape[1]), 1)
    mask = (c * CMP_STRIDE + (CMP_LEN - 1) <= pos) & (c < n_cmp)
    s = jnp.where(mask, s, NEG)
    m = jnp.max(s, axis=-1, keepdims=True)
    e = jnp.where(mask, jnp.exp(s - m), 0.0)
    l = jnp.sum(e, axis=-1, keepdims=True)
    return e / jnp.where(l > 0.0, l, 1.0)


def _block_scores_t(pg, n_rows):
    ncp = pg.shape[1]
    blk = lax.broadcasted_iota(jnp.int32, (n_rows, ncp), 0)
    c = lax.broadcasted_iota(jnp.int32, (n_rows, ncp), 1)
    hpb = SEL_BLOCK // CMP_STRIDE
    spread = (c // hpb == blk).astype(F32) + ((c + 1) // hpb == blk).astype(F32)
    return _dot_nt_split(spread.astype(BF16), pg)


def _select_blocks_t(ps_t, pos_row, n_blk):
    blk = lax.broadcasted_iota(jnp.int32, ps_t.shape, 0)
    cur = pos_row // SEL_BLOCK
    forced = (blk == 0) | (blk == cur) | (blk == cur - 1)
    score = jnp.where(forced, BIG, jnp.where(blk <= cur, ps_t, NEG))
    score = jnp.where(blk < n_blk, score, PAD_SCORE)
    blk_f = blk.astype(F32)
    sel = jnp.zeros(ps_t.shape, F32)
    for _ in range(N_SEL):
        m = jnp.max(score, axis=0, keepdims=True)
        first = jnp.min(jnp.where(score == m, blk_f, float(ps_t.shape[0])), axis=0, keepdims=True)
        hit = blk_f == first
        sel = jnp.where(hit, 1.0, sel)
        score = jnp.where(hit, REMOVED, score)
    return sel


def _lane_groups(x, op):
    parts = [x[:, i * LANES:(i + 1) * LANES] for i in range(x.shape[1] // LANES)]
    while len(parts) > 1:
        parts = [op(parts[i], parts[i + 1]) for i in range(0, len(parts) - 1, 2)] + (
            [parts[-1]] if len(parts) % 2 else [])
    return parts[0]


def _block_expand(n_blocks, n_keys):
    blk = lax.broadcasted_iota(jnp.int32, (n_blocks, n_keys), 0)
    key = lax.broadcasted_iota(jnp.int32, (n_blocks, n_keys), 1)
    return (key // SEL_BLOCK == blk).astype(BF16)


def _online_update(state, s, mask, v_bf):
    m, l, acc = state
    s = jnp.where(mask, s, NEG)
    m_new = jnp.maximum(m, jnp.max(s, axis=-1, keepdims=True))
    alpha = jnp.exp(m - m_new)
    p = jnp.where(mask, jnp.exp(s - m_new), 0.0)
    l = alpha * l + jnp.sum(p, axis=-1, keepdims=True)
    acc = alpha * acc + _dot(p.astype(BF16), v_bf)
    return m_new, l, acc


def _masked_attend(q_bf, k_bf, v_bf, mask):
    s = jnp.where(mask, _dot_nt(q_bf, k_bf), NEG)
    m = jnp.max(s, axis=-1, keepdims=True)
    e = jnp.where(mask, jnp.exp(s - m), 0.0)
    l = jnp.sum(e, axis=-1, keepdims=True)
    return _dot((e / l).astype(BF16), v_bf)


SEL_KEYS = 512


def _prompt_block_rows(t_len):
    return max(-(-t_len // SEL_BLOCK), N_SEL)


def _nsa_prompt_kernel(q_ref, kc_ref, vc_ref, ks_ref, vs_ref, kw_ref, vw_ref, gate_ref, o_ref, sel_ref, s_ref,
                       kv_ref, *, t_len):
    qb = QUERY_BLOCK
    start = pl.program_id(2) * qb

    @pl.when(pl.program_id(2) == 0)
    def _():
        for i, ref in enumerate((ks_ref, vs_ref, kw_ref, vw_ref)):
            kv_ref[i] = ref[...].astype(BF16)

    pos = start + lax.broadcasted_iota(jnp.int32, (qb, 1), 0)
    pos_row = start + lax.broadcasted_iota(jnp.int32, (1, qb), 1)
    n_cmp = t_len // CMP_STRIDE - 1
    n_blk = max(-(-t_len // SEL_BLOCK), N_SEL)
    q = q_ref[...] * NSA_SCALE
    qh = [q[:, h * NSA_DK:(h + 1) * NSA_DK].astype(BF16) for h in range(NSA_HPG)]

    kc = kc_ref[...].astype(BF16)
    vc = vc_ref[...].astype(BF16)
    o_cmp = []
    pg = jnp.zeros((qb, kc.shape[0]), F32)
    for h in range(NSA_HPG):
        p = _cmp_probs(_dot_nt(qh[h], kc), pos, n_cmp)
        o_cmp.append(_dot(p.astype(BF16), vc))
        pg = pg + p
    sel_ref[...] = _select_blocks_t(_block_scores_t(pg, _prompt_block_rows(t_len)), pos_row, n_blk)

    w_rows = WINDOW + qb
    base = pl.multiple_of(jnp.maximum(start - WINDOW, 0), qb)
    kw = kv_ref[2, pl.ds(base, w_rows), :]
    vw = kv_ref[3, pl.ds(base, w_rows), :]
    dist = pos - (base + lax.broadcasted_iota(jnp.int32, (1, w_rows), 1))
    w_mask = (dist >= 0) & (dist <= WINDOW)
    o_win = [_masked_attend(qh[h], kw, vw, w_mask) for h in range(NSA_HPG)]

    n_steps = (start + qb + SEL_KEYS - 1) // SEL_KEYS
    blocks_per_step = SEL_KEYS // SEL_BLOCK
    expand = _block_expand(blocks_per_step, SEL_KEYS)

    def scores_step(j, m_run):
        k0 = pl.multiple_of(j * SEL_KEYS, SEL_KEYS)
        k_bf = kv_ref[0, pl.ds(k0, SEL_KEYS), :]
        key = k0 + lax.broadcasted_iota(jnp.int32, (1, SEL_KEYS), 1)
        chosen = sel_ref[pl.ds(pl.multiple_of(j * blocks_per_step, blocks_per_step), blocks_per_step), :]
        mask = (_dot_tn(chosen.astype(BF16), expand) > 0.5) & (key <= pos)
        out = []
        for h in range(NSA_HPG):
            s = jnp.where(mask, _dot_nt(qh[h], k_bf), NEG)
            s_ref[j, h] = s
            out.append(jnp.maximum(m_run[h], _lane_groups(s, jnp.maximum)))
        return tuple(out)

    m_run = lax.fori_loop(0, n_steps, scores_step, tuple(jnp.full((qb, LANES), NEG, F32) for _ in range(NSA_HPG)))
    m_fin = [jnp.max(m, axis=-1, keepdims=True) for m in m_run]

    def values_step(j, carry):
        k0 = pl.multiple_of(j * SEL_KEYS, SEL_KEYS)
        v_bf = kv_ref[1, pl.ds(k0, SEL_KEYS), :]
        out = []
        for h in range(NSA_HPG):
            l_run, acc = carry[h]
            p = jnp.exp(s_ref[j, h] - m_fin[h])
            out.append((l_run + _lane_groups(p, jnp.add), acc + _dot(p.astype(BF16), v_bf)))
        return tuple(out)

    sel_state = lax.fori_loop(0, n_steps, values_step,
                              tuple((jnp.zeros((qb, LANES), F32), jnp.zeros((qb, NSA_DK), F32))
                                    for _ in range(NSA_HPG)))

    gates = jax.nn.sigmoid(gate_ref[...])
    outs = []
    for h in range(NSA_HPG):
        l_run, acc = sel_state[h]
        o_sel = acc / jnp.sum(l_run, axis=-1, keepdims=True)
        outs.append(gates[:, 3 * h:3 * h + 1] * o_cmp[h] + gates[:, 3 * h + 1:3 * h + 2] * o_sel
                    + gates[:, 3 * h + 2:3 * h + 3] * o_win[h])
    o_ref[...] = jnp.concatenate(outs, axis=1).astype(o_ref.dtype)


def nsa_prompt(z, kcvc, b_sz, t_len):
    assert t_len % SEL_KEYS == 0 and t_len >= WINDOW + QUERY_BLOCK
    nqb = t_len // QUERY_BLOCK
    n_half = t_len // CMP_STRIDE
    hd = NSA_HPG * NSA_DK

    def rows(col0):
        return pl.BlockSpec((t_len, NSA_DK), lambda b, g, i: (b, col0 // NSA_DK + g))

    def cmp_spec(kv):
        return pl.BlockSpec((None, None, None, n_half, NSA_DK), lambda b, g, i: (kv, b, g, 0, 0))

    return pl.pallas_call(
        functools.partial(_nsa_prompt_kernel, t_len=t_len),
        out_shape=jax.ShapeDtypeStruct((b_sz * t_len, NSA_HEADS * NSA_DK), BF16),
        grid=(b_sz, NSA_GROUPS, nqb),
        in_specs=[pl.BlockSpec((QUERY_BLOCK, hd), lambda b, g, i: (b * nqb + i, g)),
                  cmp_spec(0), cmp_spec(1),
                  rows(C_KS), rows(C_VS), rows(C_KW), rows(C_VW),
                  pl.BlockSpec((QUERY_BLOCK, LANES), lambda b, g, i: (b * nqb + i, C_TAIL // LANES + g))],
        out_specs=pl.BlockSpec((QUERY_BLOCK, hd), lambda b, g, i: (b * nqb + i, g)),
        scratch_shapes=[pltpu.VMEM((_prompt_block_rows(t_len), QUERY_BLOCK), F32),
                        pltpu.VMEM((t_len // SEL_KEYS, NSA_HPG, QUERY_BLOCK, SEL_KEYS), F32),
                        pltpu.VMEM((4, t_len, NSA_DK), BF16)],
        compiler_params=_params(("parallel", "parallel", "arbitrary")),
        name="nsa_prompt",
    )(z, kcvc, kcvc, z, z, z, z, z)


S_PAD = 8


def _nsa_sample_kernel(pt_ref, q_ref, kc_ref, vc_ref, kn_ref, vn_ref, kw_ref, vw_ref, kwn_ref, vwn_ref, gate_ref, *refs,
                       pages, past, s_len, n_cmp, ncp, n_blk, blk_rows, win_rows):
    kpages, vpages = refs[:pages], refs[pages:2 * pages]
    o_ref = refs[2 * pages]
    sel_ref, m_ref, l_ref, acc_ref, oc_ref = refs[2 * pages + 1:]
    j = pl.program_id(1)
    n_j = pl.num_programs(1)
    rows = NSA_HPG * s_len
    step = lax.broadcasted_iota(jnp.int32, (rows, 1), 0) % s_len
    pos = past + step
    qg = [(q_ref[g] * NSA_SCALE).astype(BF16) for g in range(NSA_GROUPS)]

    @pl.when(j == 0)
    def _():
        step_i = lax.broadcasted_iota(jnp.int32, (LANES, rows), 1) % s_len
        lane_i = lax.broadcasted_iota(jnp.int32, (LANES, rows), 0)
        step_o = lax.broadcasted_iota(jnp.int32, (rows, LANES), 0) % s_len
        lane_o = lax.broadcasted_iota(jnp.int32, (rows, LANES), 1)
        pos_row = past + lax.broadcasted_iota(jnp.int32, (1, LANES), 1) % S_PAD
        pg = jnp.zeros((LANES, ncp), F32)
        for g in range(NSA_GROUPS):
            kc = kc_ref[pl.ds(g, ncp, stride=NSA_GROUPS), :].astype(BF16)
            vc = vc_ref[pl.ds(g, ncp, stride=NSA_GROUPS), :].astype(BF16)
            p = _cmp_probs(_dot_nt(qg[g], kc), pos, n_cmp)
            oc_ref[g] = _dot(p.astype(BF16), vc)
            fold = (step_i + g * S_PAD == lane_i).astype(BF16)
            pg = pg + _dot_split(fold, p)
            m_ref[g] = jnp.full((rows, 1), NEG, F32)
            l_ref[g] = jnp.zeros((rows, 1), F32)
            acc_ref[g] = jnp.zeros((rows, NSA_DK), F32)
        sel_t = _select_blocks_t(_block_scores_t(pg, blk_rows), pos_row, n_blk).astype(BF16)
        for g in range(NSA_GROUPS):
            unfold = (step_o + g * S_PAD == lane_o).astype(BF16)
            sel_ref[g] = _dot_nt(sel_t, unfold)

    n_keys = pages * PAGE_SIZE
    blocks_per_step = n_keys // SEL_BLOCK
    expand = _block_expand(blocks_per_step, n_keys)
    key = j * n_keys + lax.broadcasted_iota(jnp.int32, (1, n_keys), 1)
    blk0 = pl.multiple_of(j * blocks_per_step, blocks_per_step)
    for g in range(NSA_GROUPS):
        k_bf = jnp.concatenate([kpages[p][pl.ds(g, PAGE_SIZE, stride=NSA_GROUPS), :] for p in range(pages)],
                               axis=0).astype(BF16)
        v_bf = jnp.concatenate([vpages[p][pl.ds(g, PAGE_SIZE, stride=NSA_GROUPS), :] for p in range(pages)],
                               axis=0).astype(BF16)
        chosen = _dot_tn(sel_ref[g, pl.ds(blk0, blocks_per_step), :].astype(BF16), expand) > 0.5
        m, l, acc = _online_update((m_ref[g], l_ref[g], acc_ref[g]), _dot_nt(qg[g], k_bf),
                                   chosen & (key <= pos), v_bf)
        m_ref[g] = m
        l_ref[g] = l
        acc_ref[g] = acc

    @pl.when(j == n_j - 1)
    def _():
        gates = jax.nn.sigmoid(gate_ref[...])
        new_i = lax.broadcasted_iota(jnp.int32, (1, S_PAD), 1)
        new_key = past + new_i
        new_blk = past // SEL_BLOCK
        first_row = (lax.broadcasted_iota(jnp.int32, (8, S_PAD), 0) == 0).astype(BF16)
        dist = pos - (past - win_rows + lax.broadcasted_iota(jnp.int32, (1, win_rows), 1))
        w_mask = (dist >= 0) & (dist <= WINDOW)
        n_mask = (new_key <= pos) & (pos - new_key <= WINDOW) & (new_i < s_len)
        for g in range(NSA_GROUPS):
            chosen_new = _dot_tn(sel_ref[g, new_blk:new_blk + 8, :].astype(BF16), first_row) > 0.5
            mask = chosen_new & (new_key <= pos) & (new_i < s_len)
            _, l, acc = _online_update((m_ref[g], l_ref[g], acc_ref[g]),
                                       _dot_nt(qg[g], kn_ref[g].astype(BF16)), mask, vn_ref[g].astype(BF16))
            o_sel = acc / l
            kw = kw_ref[pl.ds(g, win_rows, stride=NSA_GROUPS), :].astype(BF16)
            vw = vw_ref[pl.ds(g, win_rows, stride=NSA_GROUPS), :].astype(BF16)
            s_old = jnp.where(w_mask, _dot_nt(qg[g], kw), NEG)
            s_new = jnp.where(n_mask, _dot_nt(qg[g], kwn_ref[g].astype(BF16)), NEG)
            m_w = jnp.maximum(jnp.max(s_old, axis=-1, keepdims=True), jnp.max(s_new, axis=-1, keepdims=True))
            e_old = jnp.where(w_mask, jnp.exp(s_old - m_w), 0.0)
            e_new = jnp.where(n_mask, jnp.exp(s_new - m_w), 0.0)
            l_w = jnp.sum(e_old, axis=-1, keepdims=True) + jnp.sum(e_new, axis=-1, keepdims=True)
            o_win = (_dot(e_old.astype(BF16), vw) + _dot(e_new.astype(BF16), vwn_ref[g].astype(BF16))) / l_w
            gt = gates[g]
            o_ref[g] = gt[:, 0:1] * oc_ref[g] + gt[:, 1:2] * o_sel + gt[:, 2:3] * o_win


def nsa_sample(q, kc, vc, k_new, v_new, kw_old, vw_old, kw_new, vw_new, gate, pool_k, pool_v, page_table, s_len,
               pages):
    bd, n_pages = page_table.shape
    past = n_pages * PAGE_SIZE
    rows = NSA_HPG * s_len
    n_cmp = past // CMP_STRIDE - 1
    n_blk = max(-(-(past + s_len) // SEL_BLOCK), N_SEL)
    assert s_len <= S_PAD and s_len < CMP_STRIDE and n_pages % pages == 0
    assert (past // SEL_BLOCK) % 8 == 0 and (pages * PAGE_SIZE // SEL_BLOCK) % 8 == 0
    blk_rows = -(-(past // SEL_BLOCK + 8) // LANES) * LANES
    ncp = kc.shape[1] // NSA_GROUPS
    win_rows = kw_old.shape[1] // NSA_GROUPS
    new_spec = pl.BlockSpec((None, NSA_GROUPS, S_PAD, NSA_DK), lambda b, j, pt: (b, 0, 0, 0))

    def per_b(shape):
        return pl.BlockSpec((None,) + shape, lambda b, j, pt: (b,) + (0,) * len(shape))

    def page_spec(i):
        return pl.BlockSpec((None, PAGE_SIZE * NSA_GROUPS, NSA_DK), lambda b, j, pt: (pt[b, j * pages + i], 0, 0))

    grid_spec = pltpu.PrefetchScalarGridSpec(
        num_scalar_prefetch=1,
        grid=(bd, n_pages // pages),
        in_specs=[per_b((NSA_GROUPS, rows, NSA_DK)), per_b((ncp * NSA_GROUPS, NSA_DK)), per_b((ncp * NSA_GROUPS, NSA_DK)),
                  new_spec, new_spec,
                  per_b((win_rows * NSA_GROUPS, NSA_DK)), per_b((win_rows * NSA_GROUPS, NSA_DK)),
                  new_spec, new_spec,
                  per_b((NSA_GROUPS, rows, LANES))]
        + [page_spec(i) for i in range(pages)] * 2,
        out_specs=per_b((NSA_GROUPS, rows, NSA_DK)),
        scratch_shapes=[pltpu.VMEM((NSA_GROUPS, blk_rows, rows), F32),
                        pltpu.VMEM((NSA_GROUPS, rows, 1), F32),
                        pltpu.VMEM((NSA_GROUPS, rows, 1), F32),
                        pltpu.VMEM((NSA_GROUPS, rows, NSA_DK), F32),
                        pltpu.VMEM((NSA_GROUPS, rows, NSA_DK), F32)],
    )
    return pl.pallas_call(
        functools.partial(_nsa_sample_kernel, pages=pages, past=past, s_len=s_len, n_cmp=n_cmp, ncp=ncp, n_blk=n_blk,
                          blk_rows=blk_rows, win_rows=win_rows),
        out_shape=jax.ShapeDtypeStruct((bd, NSA_GROUPS, rows, NSA_DK), F32),
        grid_spec=grid_spec,
        compiler_params=_params(("parallel", "arbitrary")),
        name="nsa_sample",
    )(page_table, q, kc, vc, k_new, v_new, kw_old, vw_old, kw_new, vw_new, gate,
      *([pool_k] * pages), *([pool_v] * pages))


GLA_HEADS_PER_STEP = 2


def _gla_kernel(q_ref, k_ref, v_ref, gr_ref, lr_ref, wlr_ref, blr_ref, ng_ref, s0_ref, o_ref, sT_out_ref,
                st_ref, *, chunk, t_valid):
    c = pl.program_id(2)
    row = lax.broadcasted_iota(jnp.int32, (chunk, 1), 0)
    tri = (lax.broadcasted_iota(jnp.int32, (chunk, chunk), 1)
           <= lax.broadcasted_iota(jnp.int32, (chunk, chunk), 0)).astype(BF16)
    lr = lr_ref[...].astype(BF16)

    @pl.when(c == 0)
    def _():
        for hh in range(GLA_HEADS_PER_STEP):
            st_ref[hh] = s0_ref[hh].T

    states = []
    for hh in range(GLA_HEADS_PER_STEP):
        dk = slice(hh * GLA_DK, (hh + 1) * GLA_DK)
        dv = slice(hh * GLA_DV, (hh + 1) * GLA_DV)
        x = _dot(lr, wlr_ref[hh]) + blr_ref[hh]
        log_a = jnp.where(row < t_valid, jax.nn.log_sigmoid(x) / GLA_TAU, 0.0)
        b = _dot_split(tri, log_a)
        qs = q_ref[:, dk] * (GLA_DK ** -0.5)
        k = k_ref[:, dk]
        a_t = jnp.zeros((chunk, LANES), F32)
        for t in range(min(chunk, t_valid)):
            n_s = -(-(t + 1) // 8) * 8
            seen = lax.broadcasted_iota(jnp.int32, (n_s, GLA_DK), 0) <= t
            w = jnp.where(seen, jnp.exp(jnp.where(seen, b[t:t + 1] - b[0:n_s], 0.0)), 0.0)
            col = jnp.sum(k[0:n_s] * w * qs[t:t + 1], axis=-1, keepdims=True)
            filled = jnp.where(lax.broadcasted_iota(jnp.int32, (n_s, LANES), 1) == t, col, a_t[0:n_s])
            a_t = filled if n_s == chunk else jnp.concatenate([filled, a_t[n_s:]], axis=0)
        v_bf = v_ref[:, dv].astype(BF16)
        st = st_ref[hh]
        o = (_dot_tn(a_t[:, 0:chunk].astype(BF16), v_bf)
             + _dot_nt((qs * jnp.exp(b)).astype(BF16), st.astype(BF16)))
        b_last = b[chunk - 1:chunk, :]
        k_dec = (k * jnp.exp(b_last - b)).astype(BF16)
        st_new = st * jnp.exp(b_last) + _dot_tn(v_bf, k_dec)
        st_ref[hh] = st_new
        states.append(st_new)
        gr = gr_ref[:, dv]
        o_ref[:, dv] = (_rms(o, ng_ref[...]) * (gr * jax.nn.sigmoid(gr))).astype(o_ref.dtype)

    @pl.when(c == pl.num_programs(2) - 1)
    def _():
        for hh in range(GLA_HEADS_PER_STEP):
            sT_out_ref[hh] = states[hh].T


def gla(z, wlr, blr, norm_g, s0, b_sz, t_len, chunk, t_valid):
    assert chunk <= LANES and chunk % 8 == 0 and t_len % chunk == 0 and GLA_HEADS % GLA_HEADS_PER_STEP == 0
    nck = t_len // chunk
    hp = GLA_HEADS_PER_STEP

    def seg(col0, width):
        assert col0 % (hp * width) == 0
        return pl.BlockSpec((chunk, hp * width), lambda b, h, c: (b * nck + c, col0 // (hp * width) + h))

    return pl.pallas_call(
        functools.partial(_gla_kernel, chunk=chunk, t_valid=t_valid),
        out_shape=(jax.ShapeDtypeStruct((b_sz * t_len, GLA_HEADS * GLA_DV), BF16),
                   jax.ShapeDtypeStruct((b_sz, GLA_HEADS, GLA_DK, GLA_DV), F32)),
        grid=(b_sz, GLA_HEADS // hp, nck),
        in_specs=[seg(C_GQ, GLA_DK), seg(C_GK, GLA_DK), seg(C_GV, GLA_DV), seg(C_GR, GLA_DV),
                  pl.BlockSpec((chunk, LANES), lambda b, h, c: (b * nck + c, C_TAIL // LANES)),
                  pl.BlockSpec((hp, LANES, GLA_DK), lambda b, h, c: (h, 0, 0)),
                  pl.BlockSpec((hp, 1, GLA_DK), lambda b, h, c: (h, 0, 0)),
                  pl.BlockSpec((1, GLA_DV), lambda b, h, c: (0, 0)),
                  pl.BlockSpec((None, hp, GLA_DK, GLA_DV), lambda b, h, c: (b, h, 0, 0))],
        out_specs=(pl.BlockSpec((chunk, hp * GLA_DV), lambda b, h, c: (b * nck + c, h)),
                   pl.BlockSpec((None, hp, GLA_DK, GLA_DV), lambda b, h, c: (b, h, 0, 0))),
        scratch_shapes=[pltpu.VMEM((hp, GLA_DV, GLA_DK), F32)],
        compiler_params=_params(("parallel", "parallel", "arbitrary")),
        name="gla",
    )(z, z, z, z, z, wlr, blr, norm_g.reshape(1, GLA_DV), s0)


def _merge_kernel(on_ref, og_ref, wn_ref, wg_ref, ma_ref, mb_ref, y_ref):
    y = (jax.nn.sigmoid(ma_ref[...]) * _dot(on_ref[...], wn_ref[...])
         + jax.nn.sigmoid(mb_ref[...]) * _dot(og_ref[...], wg_ref[...]))
    y_ref[...] = y.astype(y_ref.dtype)


def merge(o_nsa, o_gla, w_nsa, w_gla, z, tm, tn):
    n = o_nsa.shape[0]
    return pl.pallas_call(
        _merge_kernel,
        out_shape=jax.ShapeDtypeStruct((n, D_MODEL), BF16),
        grid=(n // tm, D_MODEL // tn),
        in_specs=[pl.BlockSpec((tm, o_nsa.shape[1]), lambda i, j: (i, 0)),
                  pl.BlockSpec((tm, o_gla.shape[1]), lambda i, j: (i, 0)),
                  pl.BlockSpec((w_nsa.shape[0], tn), lambda i, j: (0, j)),
                  pl.BlockSpec((w_gla.shape[0], tn), lambda i, j: (0, j)),
                  pl.BlockSpec((tm, tn), lambda i, j: (i, C_MA // tn + j)),
                  pl.BlockSpec((tm, tn), lambda i, j: (i, C_MB // tn + j))],
        out_specs=pl.BlockSpec((tm, tn), lambda i, j: (i, j)),
        compiler_params=_params(("parallel", "arbitrary")),
        name="merge",
    )(o_nsa, o_gla, w_nsa, w_gla, z, z)


def _out_proj_kernel(y_ref, w_ref, x_ref, o_ref):
    o_ref[...] = x_ref[...] + _dot(y_ref[...], w_ref[...])


def out_proj(y, w_out, x, tm, tn):
    n = y.shape[0]
    return pl.pallas_call(
        _out_proj_kernel,
        out_shape=jax.ShapeDtypeStruct((n, D_MODEL), F32),
        grid=(n // tm, D_MODEL // tn),
        in_specs=[pl.BlockSpec((tm, D_MODEL), lambda i, j: (i, 0)),
                  pl.BlockSpec((D_MODEL, tn), lambda i, j: (0, j)),
                  pl.BlockSpec((tm, tn), lambda i, j: (i, j))],
        out_specs=pl.BlockSpec((tm, tn), lambda i, j: (i, j)),
        compiler_params=_params(("parallel", "arbitrary")),
        name="out_proj",
    )(y, w_out, x)


def _top_values(s, count):
    vals = []
    for _ in range(count):
        m = jnp.max(s, axis=0, keepdims=True)
        s = jnp.where(s == m, REMOVED, s)
        vals.append(m)
    return vals


def _peer_scores_kernel(q_ref, k1_ref, k2_ref, ns1_ref, d2_ref, e1_ref, e2_ref):
    n_top = PEER_TOPK + 1
    pad_rows = [jnp.full((1, q_ref.shape[0]), REMOVED, F32)]
    for h in range(PEER_HEADS):
        q1 = q_ref[:, h * 2 * PEER_HALF:h * 2 * PEER_HALF + PEER_HALF].astype(BF16)
        q2 = q_ref[:, h * 2 * PEER_HALF + PEER_HALF:(h + 1) * 2 * PEER_HALF].astype(BF16)
        s1 = _dot_nt(k1_ref[h], q1)
        s2 = _dot_nt(k2_ref[h], q2)
        v1 = _top_values(s1, n_top)
        v2 = _top_values(s2, n_top)
        rows = [v1[i] + v2[j] for i in range(n_top) for j in range(n_top // (i + 1))]
        cand = jnp.concatenate(rows + pad_rows * (-len(rows) % 8), axis=0)
        top = _top_values(cand, n_top)
        z = sum(jnp.exp(t - top[0]) for t in top[:PEER_TOPK])
        ns1_ref[h] = -s1
        d2_ref[h] = s2 - 0.5 * (top[PEER_TOPK - 1] + top[PEER_TOPK])
        e1_ref[h] = jnp.exp(s1 - v1[0])
        e2_ref[h] = jnp.exp(s2 - v2[0]) / z


def peer_scores(qp, keys1, keys2, tn):
    n = qp.shape[0]
    big = jax.ShapeDtypeStruct((PEER_HEADS, PEER_NKEYS, n), F32)
    big_spec = pl.BlockSpec((PEER_HEADS, PEER_NKEYS, tn), lambda i: (0, 0, i))
    key_spec = pl.BlockSpec((PEER_HEADS, PEER_NKEYS, PEER_HALF), lambda i: (0, 0, 0))
    return pl.pallas_call(
        _peer_scores_kernel,
        out_shape=(big, big, big, big),
        grid=(n // tn,),
        in_specs=[pl.BlockSpec((tn, qp.shape[1]), lambda i: (i, 0)), key_spec, key_spec],
        out_specs=(big_spec, big_spec, big_spec, big_spec),
        compiler_params=_params(("parallel",)),
        name="peer_scores",
    )(qp, keys1, keys2)


PEER_I1_PER_TILE = 8


def _peer_dense_kernel(x_ref, g2_ref, gf_ref, u_ref, v_ref, ns1_ref, e1_ref, d2_ref, e2_ref, o_ref,
                       h_ref, acc_ref):
    e = pl.program_id(1)

    @pl.when(e == 0)
    def _():
        h_ref[...] = _rms(x_ref[...], g2_ref[...]).astype(BF16)
        acc_ref[...] = jnp.zeros_like(acc_ref)

    pre = _dot_nt(u_ref[...], h_ref[...])
    n_tok = pre.shape[1]
    tile = min(LANES, n_tok)
    parts = []
    for c in range(PEER_I1_PER_TILE):
        cols = []
        for t0 in range(0, n_tok, tile):
            tok = slice(t0, t0 + tile)
            w = jnp.zeros((PEER_NKEYS, tile), F32)
            for h in range(PEER_HEADS):
                keep = d2_ref[h, :, tok] >= ns1_ref[h, c:c + 1, tok]
                w = w + jnp.where(keep, e1_ref[h, c:c + 1, tok] * e2_ref[h, :, tok], 0.0)
            cols.append(w * _gelu_tanh(pre[c * PEER_NKEYS:(c + 1) * PEER_NKEYS, tok]))
        parts.append(jnp.concatenate(cols, axis=1).astype(BF16))
    acc_ref[...] += _dot_tn(jnp.concatenate(parts, axis=0), v_ref[...])

    @pl.when(e == pl.num_programs(1) - 1)
    def _():
        o_ref[...] = _rms(x_ref[...] + acc_ref[...], gf_ref[...])


def peer_dense(x, g2, gf, u_bf, v_bf, ns1, d2, e1, e2, tn):
    n = x.shape[0]
    te = PEER_I1_PER_TILE * PEER_NKEYS
    n_exp = u_bf.shape[0]
    sub = pl.BlockSpec((PEER_HEADS, PEER_I1_PER_TILE, tn), lambda i, e: (0, e, i))
    full = pl.BlockSpec((PEER_HEADS, PEER_NKEYS, tn), lambda i, e: (0, 0, i))
    vec = pl.BlockSpec((1, D_MODEL), lambda i, e: (0, 0))
    return pl.pallas_call(
        _peer_dense_kernel,
        out_shape=jax.ShapeDtypeStruct((n, D_MODEL), F32),
        grid=(n // tn, n_exp // te),
        in_specs=[pl.BlockSpec((tn, D_MODEL), lambda i, e: (i, 0)), vec, vec,
                  pl.BlockSpec((te, D_MODEL), lambda i, e: (e, 0)),
                  pl.BlockSpec((te, D_MODEL), lambda i, e: (e, 0)),
                  sub, sub, full, full],
        out_specs=pl.BlockSpec((tn, D_MODEL), lambda i, e: (i, 0)),
        scratch_shapes=[pltpu.VMEM((tn, D_MODEL), BF16), pltpu.VMEM((tn, D_MODEL), F32)],
        compiler_params=_params(("parallel", "arbitrary")),
        name="peer_dense",
    )(x, g2.reshape(1, D_MODEL), gf.reshape(1, D_MODEL), u_bf, v_bf, ns1, e1, d2, e2)


PACK_ROWS = Z_COLS - C_TAIL


def _pack_w_in_kernel(ia_ref, ib_ref, kind_ref, a_ref, b_ref, o_ref, *, shifts, gate_row0, per_group, glr_row0):
    kind = kind_ref[pl.program_id(0)]
    for k, s in enumerate(shifts):
        @pl.when(kind == k)
        def _(s=s):
            src = a_ref[...] if s == 0 else jnp.concatenate([a_ref[s:, :], b_ref[:s, :]], axis=0)
            o_ref[...] = src.astype(BF16)

    @pl.when(kind == len(shifts))
    def _():
        out_row = lax.broadcasted_iota(jnp.int32, (PACK_ROWS, PACK_ROWS), 0)
        src_row = lax.broadcasted_iota(jnp.int32, (PACK_ROWS, PACK_ROWS), 1)
        group, r = out_row // LANES, out_row % LANES
        pick = (src_row == gate_row0 + group * per_group + r) & (r < per_group)
        out = _dot(pick.astype(BF16), a_ref[...].astype(BF16))
        pick = (src_row == glr_row0 + out_row - GLR_LANE) & (out_row >= GLR_LANE) & (out_row < GLR_LANE + GLA_RANK)
        out = out + _dot(pick.astype(BF16), b_ref[...].astype(BF16))
        o_ref[...] = out.astype(BF16)


def _pack_w_in(w_in):
    offs = np.concatenate([[0], np.cumsum(IN_SPLITS)]).astype(np.int64)
    plain = (0, 1, 2, 3, 4, 5, 6, 8, 9, 10, 11, 13, 14)
    src_start = np.concatenate([np.arange(offs[i], offs[i + 1], PACK_ROWS) for i in plain])
    assert all((offs[i + 1] - offs[i]) % PACK_ROWS == 0 for i in plain) and len(src_start) * PACK_ROWS == C_TAIL
    shifts = tuple(sorted(set(int(s) for s in src_start % PACK_ROWS)))
    gate_col, glr_col = int(offs[7]), int(offs[12])
    per_group = 3 * NSA_HPG
    assert gate_col % PACK_ROWS + NSA_GROUPS * per_group <= PACK_ROWS and PACK_ROWS == NSA_GROUPS * LANES
    ia = np.concatenate([src_start // PACK_ROWS, [gate_col // PACK_ROWS]])
    ib = np.concatenate([src_start // PACK_ROWS + 1, [glr_col // PACK_ROWS]])
    kind = np.concatenate([[shifts.index(int(s)) for s in src_start % PACK_ROWS], [len(shifts)]])
    assert all(s % 8 == 0 for s in shifts)
    d = w_in.shape[0]
    w_t = w_in.T
    grid_spec = pltpu.PrefetchScalarGridSpec(
        num_scalar_prefetch=3,
        grid=(Z_COLS // PACK_ROWS,),
        in_specs=[pl.BlockSpec((PACK_ROWS, d), lambda c, ia, ib, kind: (ia[c], 0)),
                  pl.BlockSpec((PACK_ROWS, d), lambda c, ia, ib, kind: (ib[c], 0))],
        out_specs=pl.BlockSpec((PACK_ROWS, d), lambda c, ia, ib, kind: (c, 0)),
    )
    return pl.pallas_call(
        functools.partial(_pack_w_in_kernel, shifts=shifts, gate_row0=gate_col % PACK_ROWS, per_group=per_group,
                          glr_row0=glr_col % PACK_ROWS),
        out_shape=jax.ShapeDtypeStruct((Z_COLS, d), BF16),
        grid_spec=grid_spec,
        compiler_params=_params(("arbitrary",)),
        name="pack_w_in",
    )(jnp.asarray(ia, jnp.int32), jnp.asarray(ib, jnp.int32), jnp.asarray(kind, jnp.int32), w_t, w_t)


def _pack_cmp(pe, w1, w2):
    w1cat = jnp.concatenate([w1[:CMP_STRIDE].reshape(CMP_STRIDE * NSA_DK, CMP_HIDDEN),
                             w1[CMP_STRIDE:].reshape(CMP_STRIDE * NSA_DK, CMP_HIDDEN)], axis=1).astype(BF16)
    return (w1cat, pe.reshape(1, CMP_LEN * NSA_DK).astype(BF16),
            w1.reshape(CMP_LEN * NSA_DK, CMP_HIDDEN).astype(BF16), w2.astype(BF16))


def _row_tile(n, cap):
    t = min(n, cap)
    while n % t:
        t //= 2
    return t


def _channel_tail(x2d, z, o_nsa, o_gla, wts):
    n = x2d.shape[0]
    tm = _row_tile(n, 1024)
    y = merge(o_nsa, o_gla, wts["w_nsa"], wts["w_gla"], z, tm, 512)
    x1 = out_proj(y, wts["w_out"], x2d, tm, 512)
    qp = norm_matmul(x1, wts["norm2_g"], wts["w_q"], tm, 512)
    tn = _row_tile(n, 256)
    ns1, d2, e1, e2 = peer_scores(qp, wts["keys1"], wts["keys2"], tn)
    return peer_dense(x1, wts["norm2_g"], wts["norm_f_g"], wts["u"], wts["v"], ns1, d2, e1, e2, _row_tile(n, 512))


def kernel(x_prompt, x_sample, cache_k_cmp, cache_v_cmp, cache_k_slc, cache_v_slc, cache_k_win, cache_v_win, state_gla, page_table, norm1_g, w_in, cmp_pe_k, cmp_w1_k, cmp_w2_k, cmp_pe_v, cmp_w1_v, cmp_w2_v, gla_w_lr2, gla_b_lr, gla_norm_g, w_nsa_proj, w_gla_proj, w_out, norm2_g, peer_w_q, peer_keys1, peer_keys2, peer_u, peer_v, norm_f_g):
    b_sz, t_len, _ = x_prompt.shape
    bd, s_len, _ = x_sample.shape
    n_pool = cache_k_cmp.shape[0]
    wb = cache_k_win.shape[1]

    w_pack = _pack_w_in(w_in)
    cmp_k = _pack_cmp(cmp_pe_k, cmp_w1_k, cmp_w2_k)
    cmp_v = _pack_cmp(cmp_pe_v, cmp_w1_v, cmp_w2_v)
    cmp_kv = [jnp.stack([a, b]) for a, b in zip(cmp_k, cmp_v)]
    wlr = jnp.zeros((LANES, GLA_HEADS * GLA_DK), F32).at[GLR_LANE:GLR_LANE + GLA_RANK].set(gla_w_lr2)
    wlr = wlr.reshape(LANES, GLA_HEADS, GLA_DK).transpose(1, 0, 2).astype(BF16)
    blr = gla_b_lr.reshape(GLA_HEADS, 1, GLA_DK)
    wts = dict(w_nsa=w_nsa_proj.astype(BF16), w_gla=w_gla_proj.astype(BF16), w_out=w_out.astype(BF16),
               w_q=peer_w_q.astype(BF16), norm2_g=norm2_g, norm_f_g=norm_f_g,
               keys1=peer_keys1.astype(BF16), keys2=peer_keys2.astype(BF16),
               u=peer_u.astype(BF16), v=peer_v.astype(BF16))

    n_p = b_sz * t_len
    xp = x_prompt.reshape(n_p, D_MODEL)
    z_p = norm_matmul(xp, norm1_g, w_pack, _row_tile(n_p, 1024), GD, w_rows_are_outputs=True)
    heads_p = split_heads(z_p, _row_tile(n_p, 1024))
    fs_p = half_proj_dense(z_p, cmp_kv[0], b_sz, t_len)
    kcvc_p = compress_finish(fs_p, cmp_kv[1], cmp_kv[2], cmp_kv[3], 1)
    o_nsa_p = nsa_prompt(z_p, kcvc_p, b_sz, t_len)
    s0 = jnp.zeros((b_sz, GLA_HEADS, GLA_DK, GLA_DV), F32)
    o_gla_p, gla_state_p = gla(z_p, wlr, blr, gla_norm_g, s0, b_sz, t_len, GLA_CHUNK, GLA_CHUNK)
    y_prompt = _channel_tail(xp, z_p, o_nsa_p, o_gla_p, wts).reshape(b_sz, t_len, D_MODEL)
    kcr_p, vcr_p, ksr_p, vsr_p, kwr_p, vwr_p = (a.reshape(b_sz, t_len, NSA_GROUPS, NSA_DK) for a in heads_p)
    wl = min(WINDOW, t_len)
    k_win_p = kwr_p[:, t_len - wl:]
    v_win_p = vwr_p[:, t_len - wl:]

    n_s = bd * s_len
    xs = x_sample.reshape(n_s, D_MODEL)
    z_s = norm_matmul(xs, norm1_g, w_pack, _row_tile(n_s, 1024), GD, w_rows_are_outputs=True)
    heads_s = split_heads(z_s, _row_tile(n_s, 1024))
    kcr_s, vcr_s, ksr_s, vsr_s, kwr_s, vwr_s = (a.reshape(bd, s_len, NSA_GROUPS, NSA_DK) for a in heads_s)

    pool2d = lambda p: p.reshape(n_pool, PAGE_SIZE * NSA_GROUPS, NSA_DK)
    def compress_pool(pool, prm):
        fs = half_proj_paged(pool2d(pool), page_table, prm[0], min(64, page_table.shape[1]))
        return compress_finish(fs[None, :, None], prm[1][None], prm[2][None], prm[3][None], NSA_GROUPS)[0, :, 0]

    kc_s = compress_pool(cache_k_cmp, cmp_k)
    vc_s = compress_pool(cache_v_cmp, cmp_v)

    assert wb >= s_len
    k_win_s = jnp.concatenate([cache_k_win[:, s_len:], kwr_s], axis=1)
    v_win_s = jnp.concatenate([cache_v_win[:, s_len:], vwr_s], axis=1)
    win_2d = lambda a: a.reshape(bd, wb * NSA_GROUPS, NSA_DK)

    def new_rows(a):
        return jnp.pad(a.transpose(0, 2, 1, 3), ((0, 0), (0, 0), (0, S_PAD - s_len), (0, 0)))

    rows = NSA_HPG * s_len
    q_s = z_s[:, C_Q:C_Q + NSA_HEADS * NSA_DK].reshape(bd, s_len, NSA_GROUPS, NSA_HPG, NSA_DK)
    q_s = q_s.transpose(0, 2, 3, 1, 4).reshape(bd, NSA_GROUPS, rows, NSA_DK)
    gate_s = z_s[:, C_TAIL:C_TAIL + 512].reshape(bd, s_len, NSA_GROUPS, LANES)[..., :3 * NSA_HPG]
    gate_s = gate_s.reshape(bd, s_len, NSA_GROUPS, NSA_HPG, 3).transpose(0, 2, 3, 1, 4).reshape(bd, NSA_GROUPS, rows, 3)
    gate_s = jnp.pad(gate_s, ((0, 0), (0, 0), (0, 0), (0, LANES - 3)))
    o_s = nsa_sample(q_s, kc_s, vc_s, new_rows(ksr_s), new_rows(vsr_s), win_2d(cache_k_win), win_2d(cache_v_win),
                     new_rows(kwr_s), new_rows(vwr_s), gate_s, pool2d(cache_k_slc), pool2d(cache_v_slc), page_table,
                     s_len, min(32, page_table.shape[1]))
    o_nsa_s = o_s.reshape(bd, NSA_GROUPS, NSA_HPG, s_len, NSA_DK).transpose(0, 3, 1, 2, 4)
    o_nsa_s = o_nsa_s.reshape(n_s, NSA_HEADS * NSA_DK).astype(BF16)

    z_s_pad = jnp.pad(z_s.reshape(bd, s_len, Z_COLS), ((0, 0), (0, S_PAD - s_len), (0, 0))).reshape(bd * S_PAD, Z_COLS)
    o_gla_s, gla_state_s = gla(z_s_pad, wlr, blr, gla_norm_g, state_gla, bd, S_PAD, S_PAD, s_len)
    o_gla_s = o_gla_s.reshape(bd, S_PAD, GLA_HEADS * GLA_DV)[:, :s_len].reshape(n_s, GLA_HEADS * GLA_DV)
    y_sample = _channel_tail(xs, z_s, o_nsa_s, o_gla_s, wts).reshape(bd, s_len, D_MODEL)

    return (y_prompt, y_sample, kcr_p, vcr_p, ksr_p, vsr_p, k_win_p, v_win_p, gla_state_p,
            kcr_s, vcr_s, ksr_s, vsr_s, k_win_s, v_win_s, gla_state_s)
```

```python
import functools

import numpy as np
import jax
import jax.numpy as jnp
from jax import lax
from jax.experimental import pallas as pl
from jax.experimental.pallas import tpu as pltpu

F32 = jnp.float32
BF16 = jnp.bfloat16

D_MODEL = 2048
PAGE_SIZE = 128
NSA_HEADS = 16
NSA_GROUPS = 4
NSA_HPG = NSA_HEADS // NSA_GROUPS
NSA_DK = 128
NSA_SCALE = NSA_DK ** -0.5
CMP_LEN = 32
CMP_STRIDE = 16
CMP_HIDDEN = 256
SEL_BLOCK = 64
N_SEL = 16
WINDOW = 512
QUERY_BLOCK = 256
GLA_HEADS = 4
GLA_DK = 256
GLA_DV = 512
GLA_RANK = 16
GLA_TAU = 16.0
GLA_CHUNK = 64
PEER_HEADS = 8
PEER_NKEYS = 128
PEER_HALF = 128
PEER_TOPK = 16
NORM_EPS = 1e-6
NEG = -1e30
BIG = 1e30
PAD_SCORE = -3e38
REMOVED = -float("inf")

IN_SPLITS = (NSA_HEADS * NSA_DK,) + (NSA_GROUPS * NSA_DK,) * 6 + (
    3 * NSA_HEADS, GLA_HEADS * GLA_DK, GLA_HEADS * GLA_DK, GLA_HEADS * GLA_DV,
    GLA_HEADS * GLA_DV, GLA_RANK, D_MODEL, D_MODEL)

GD = NSA_GROUPS * NSA_DK
C_Q = 0
C_KC = C_Q + NSA_HEADS * NSA_DK
C_VC = C_KC + GD
C_KS = C_VC + GD
C_VS = C_KS + GD
C_KW = C_VS + GD
C_VW = C_KW + GD
C_GQ = C_VW + GD
C_GK = C_GQ + GLA_HEADS * GLA_DK
C_GV = C_GK + GLA_HEADS * GLA_DK
C_GR = C_GV + GLA_HEADS * GLA_DV
C_MA = C_GR + GLA_HEADS * GLA_DV
C_MB = C_MA + D_MODEL
C_TAIL = C_MB + D_MODEL
Z_COLS = C_TAIL + 512
GLR_LANE = 16
LANES = 128

VMEM_LIMIT = 56 * 1024 * 1024


def _params(sem):
    return pltpu.CompilerParams(dimension_semantics=sem, vmem_limit_bytes=VMEM_LIMIT)


def _dot(a, b):
    return jnp.dot(a, b, preferred_element_type=F32)


def _dot_nt(a, b):
    return lax.dot_general(a, b, (((1,), (1,)), ((), ())), preferred_element_type=F32)


def _dot_tn(a, b):
    return lax.dot_general(a, b, (((0,), (0,)), ((), ())), preferred_element_type=F32)


def _rms(x, g):
    return x * lax.rsqrt(jnp.mean(x * x, axis=-1, keepdims=True) + NORM_EPS) * g


GELU_C0 = 2.0 * (2.0 / np.pi) ** 0.5
GELU_C1 = GELU_C0 * 0.044715


def _gelu_tanh(x):
    return x / (1.0 + jnp.exp(-(x * (GELU_C0 + GELU_C1 * (x * x)))))


def _bf16_pieces(x):
    hi = x.astype(BF16)
    rest = x - hi.astype(F32)
    mid = rest.astype(BF16)
    lo = (rest - mid.astype(F32)).astype(BF16)
    return hi, mid, lo


def _dot_nt_split(small_ints_bf16, x):
    return sum(_dot_nt(small_ints_bf16, piece) for piece in _bf16_pieces(x))


def _dot_split(small_ints_bf16, x):
    return sum(_dot(small_ints_bf16, piece) for piece in _bf16_pieces(x))


def _norm_matmul_kernel(x_ref, g_ref, w_ref, o_ref, h_ref, *, w_rows_are_outputs):
    @pl.when(pl.program_id(1) == 0)
    def _():
        h_ref[...] = _rms(x_ref[...], g_ref[...]).astype(BF16)

    o_ref[...] = (_dot_nt if w_rows_are_outputs else _dot)(h_ref[...], w_ref[...])


def norm_matmul(x, g, w, tm, tn, w_rows_are_outputs=False):
    n, d = x.shape
    cols = w.shape[0] if w_rows_are_outputs else w.shape[1]
    w_spec = (pl.BlockSpec((tn, d), lambda i, j: (j, 0)) if w_rows_are_outputs
              else pl.BlockSpec((d, tn), lambda i, j: (0, j)))
    return pl.pallas_call(
        functools.partial(_norm_matmul_kernel, w_rows_are_outputs=w_rows_are_outputs),
        out_shape=jax.ShapeDtypeStruct((n, cols), F32),
        grid=(n // tm, cols // tn),
        in_specs=[pl.BlockSpec((tm, d), lambda i, j: (i, 0)),
                  pl.BlockSpec((1, d), lambda i, j: (0, 0)),
                  w_spec],
        out_specs=pl.BlockSpec((tm, tn), lambda i, j: (i, j)),
        scratch_shapes=[pltpu.VMEM((tm, d), BF16)],
        compiler_params=_params(("parallel", "arbitrary")),
        name="norm_matmul",
    )(x, g.reshape(1, d), w)


KV_SEGMENTS = 6


def _split_heads_kernel(z_ref, *head_refs):
    k = pl.program_id(1)
    for seg in range(KV_SEGMENTS):
        @pl.when(k == seg)
        def _(seg=seg):
            for g in range(NSA_GROUPS):
                head_refs[seg][:, g, :] = z_ref[:, g * NSA_DK:(g + 1) * NSA_DK]


def split_heads(z, tm):
    n = z.shape[0]
    heads = jax.ShapeDtypeStruct((n, NSA_GROUPS, NSA_DK), F32)
    return pl.pallas_call(
        _split_heads_kernel,
        out_shape=(heads,) * KV_SEGMENTS,
        grid=(n // tm, KV_SEGMENTS),
        in_specs=[pl.BlockSpec((tm, GD), lambda i, k: (i, C_KC // GD + k))],
        out_specs=(pl.BlockSpec((tm, NSA_GROUPS, NSA_DK), lambda i, k: (i, 0, 0)),) * KV_SEGMENTS,
        compiler_params=_params(("parallel", "arbitrary")),
        name="split_heads",
    )(z)


def _half_rows(ref, first_row, n_half, row_stride, lane0):
    parts = [ref[pl.ds(first_row + r * row_stride, n_half, stride=CMP_STRIDE * row_stride), pl.ds(lane0, NSA_DK)]
             for r in range(CMP_STRIDE)]
    return jnp.concatenate(parts, axis=1)


def _half_proj_dense_kernel(x_ref, w_ref, o_ref, *, n_half):
    o_ref[...] = _dot(_half_rows(x_ref, 0, n_half, 1, 0).astype(BF16), w_ref[...])


def half_proj_dense(z, w1cat, b_sz, t_len):
    n_half = t_len // CMP_STRIDE
    return pl.pallas_call(
        functools.partial(_half_proj_dense_kernel, n_half=n_half),
        out_shape=jax.ShapeDtypeStruct((2, b_sz, NSA_GROUPS, n_half, 2 * CMP_HIDDEN), F32),
        grid=(2, b_sz, NSA_GROUPS),
        in_specs=[pl.BlockSpec((t_len, NSA_DK), lambda kv, b, g: (b, C_KC // NSA_DK + kv * NSA_GROUPS + g)),
                  pl.BlockSpec((None, CMP_STRIDE * NSA_DK, 2 * CMP_HIDDEN), lambda kv, b, g: (kv, 0, 0))],
        out_specs=pl.BlockSpec((None, None, None, n_half, 2 * CMP_HIDDEN), lambda kv, b, g: (kv, b, g, 0, 0)),
        compiler_params=_params(("parallel", "parallel", "parallel")),
        name="half_proj_dense",
    )(z, w1cat)


HALVES_PER_PAGE = PAGE_SIZE // CMP_STRIDE


def _half_proj_paged_kernel(pt_ref, *refs, pages):
    page_refs, w_ref, o_ref = refs[:pages], refs[pages], refs[pages + 1]
    tile = 2 * NSA_GROUPS
    rows_per_half = CMP_STRIDE * NSA_GROUPS
    low = lax.broadcasted_iota(jnp.int32, (tile, NSA_DK), 0) < NSA_GROUPS
    acc = None
    for r2 in range(CMP_STRIDE // 2):
        even, odd = [], []
        for p in range(pages):
            for n in range(0, HALVES_PER_PAGE, 2):
                a = page_refs[p][pl.ds(n * rows_per_half + tile * r2, tile), :]
                b = page_refs[p][pl.ds((n + 1) * rows_per_half + tile * r2, tile), :]
                even.append(jnp.where(low, a, pltpu.roll(b, NSA_GROUPS, 0)))
                odd.append(jnp.where(low, pltpu.roll(a, NSA_GROUPS, 0), b))
        lhs = jnp.concatenate([jnp.concatenate(even, axis=0), jnp.concatenate(odd, axis=0)], axis=1).astype(BF16)
        part = _dot(lhs, w_ref[pl.ds(r2 * 2 * NSA_DK, 2 * NSA_DK), :])
        acc = part if acc is None else acc + part
    o_ref[...] = acc


def half_proj_paged(pool2d, page_table, w1cat_one, pages):
    bd, n_pages = page_table.shape
    n_half = n_pages * HALVES_PER_PAGE
    rows = pages * HALVES_PER_PAGE * NSA_GROUPS

    def page_spec(i):
        return pl.BlockSpec((None, PAGE_SIZE * NSA_GROUPS, NSA_DK),
                            lambda b, j, pt: (pt[b, j * pages + i], 0, 0))

    grid_spec = pltpu.PrefetchScalarGridSpec(
        num_scalar_prefetch=1,
        grid=(bd, n_pages // pages),
        in_specs=[page_spec(i) for i in range(pages)]
        + [pl.BlockSpec((CMP_STRIDE * NSA_DK, 2 * CMP_HIDDEN), lambda b, j, pt: (0, 0))],
        out_specs=pl.BlockSpec((None, rows, 2 * CMP_HIDDEN), lambda b, j, pt: (b, j, 0)),
    )
    return pl.pallas_call(
        functools.partial(_half_proj_paged_kernel, pages=pages),
        out_shape=jax.ShapeDtypeStruct((bd, n_half * NSA_GROUPS, 2 * CMP_HIDDEN), F32),
        grid_spec=grid_spec,
        compiler_params=_params(("parallel", "arbitrary")),
        name="half_proj_paged",
    )(page_table, *([pool2d] * pages), w1cat_one)


def _compress_finish_kernel(fs_ref, pe_ref, w1_ref, w2_ref, o_ref, *, n_rows, step):
    pe = jnp.broadcast_to(pe_ref[...], (8, CMP_LEN * NSA_DK))
    c = _dot(pe, w1_ref[...])[0:1]
    first = fs_ref[:, 0:CMP_HIDDEN]
    second_next = pltpu.roll(fs_ref[:, CMP_HIDDEN:2 * CMP_HIDDEN], n_rows - step, 0)
    hid = _gelu_tanh(first + second_next + c)
    out = _dot(hid.astype(BF16), w2_ref[...])
    row = lax.broadcasted_iota(jnp.int32, (n_rows, 1), 0)
    o_ref[...] = jnp.where(row < n_rows - step, out, 0.0)


def compress_finish(fs, pe, w1, w2, step):
    kv, b_sz, n_sets, n_rows, _ = fs.shape
    return pl.pallas_call(
        functools.partial(_compress_finish_kernel, n_rows=n_rows, step=step),
        out_shape=jax.ShapeDtypeStruct((kv, b_sz, n_sets, n_rows, NSA_DK), F32),
        grid=(kv, b_sz, n_sets),
        in_specs=[pl.BlockSpec((None, None, None, n_rows, 2 * CMP_HIDDEN), lambda k, b, g: (k, b, g, 0, 0)),
                  pl.BlockSpec((None, 1, CMP_LEN * NSA_DK), lambda k, b, g: (k, 0, 0)),
                  pl.BlockSpec((None, CMP_LEN * NSA_DK, CMP_HIDDEN), lambda k, b, g: (k, 0, 0)),
                  pl.BlockSpec((None, CMP_HIDDEN, NSA_DK), lambda k, b, g: (k, 0, 0))],
        out_specs=pl.BlockSpec((None, None, None, n_rows, NSA_DK), lambda k, b, g: (k, b, g, 0, 0)),
        compiler_params=_params(("parallel", "parallel", "parallel")),
        name="compress_finish",
    )(fs, pe, w1, w2)


def _cmp_probs(s, pos, n_cmp):
    c = lax.broadcasted_iota(jnp.int32, (1, s.shape[1]), 1)
    mask = (c * CMP_STRIDE + (CMP_LEN - 1) <= pos) & (c < n_cmp)
    s = jnp.where(mask, s, NEG)
    m = jnp.max(s, axis=-1, keepdims=True)
    e = jnp.where(mask, jnp.exp(s - m), 0.0)
    l = jnp.sum(e, axis=-1, keepdims=True)
    return e / jnp.where(l > 0.0, l, 1.0)


def _block_scores_t(pg, n_rows):
    ncp = pg.shape[1]
    blk = lax.broadcasted_iota(jnp.int32, (n_rows, ncp), 0)
    c = lax.broadcasted_iota(jnp.int32, (n_rows, ncp), 1)
    hpb = SEL_BLOCK // CMP_STRIDE
    spread = (c // hpb == blk).astype(F32) + ((c + 1) // hpb == blk).astype(F32)
    return _dot_nt_split(spread.astype(BF16), pg)


def _select_blocks_t(ps_t, pos_row, n_blk):
    blk = lax.broadcasted_iota(jnp.int32, ps_t.shape, 0)
    cur = pos_row // SEL_BLOCK
    forced = (blk == 0) | (blk == cur) | (blk == cur - 1)
    score = jnp.where(forced, BIG, jnp.where(blk <= cur, ps_t, NEG))
    score = jnp.where(blk < n_blk, score, PAD_SCORE)
    blk_f = blk.astype(F32)
    sel = jnp.zeros(ps_t.shape, F32)
    for _ in range(N_SEL):
        m = jnp.max(score, axis=0, keepdims=True)
        first = jnp.min(jnp.where(score == m, blk_f, float(ps_t.shape[0])), axis=0, keepdims=True)
        hit = blk_f == first
        sel = jnp.where(hit, 1.0, sel)
        score = jnp.where(hit, REMOVED, score)
    return sel


def _lane_groups(x, op):
    parts = [x[:, i * LANES:(i + 1) * LANES] for i in range(x.shape[1] // LANES)]
    while len(parts) > 1:
        parts = [op(parts[i], parts[i + 1]) for i in range(0, len(parts) - 1, 2)] + (
            [parts[-1]] if len(parts) % 2 else [])
    return parts[0]


def _block_expand(n_blocks, n_keys):
    blk = lax.broadcasted_iota(jnp.int32, (n_blocks, n_keys), 0)
    key = lax.broadcasted_iota(jnp.int32, (n_blocks, n_keys), 1)
    return (key // SEL_BLOCK == blk).astype(BF16)


def _online_update(state, s, mask, v_bf):
    m, l, acc = state
    s = jnp.where(mask, s, NEG)
    m_new = jnp.maximum(m, jnp.max(s, axis=-1, keepdims=True))
    alpha = jnp.exp(m - m_new)
    p = jnp.where(mask, jnp.exp(s - m_new), 0.0)
    l = alpha * l + jnp.sum(p, axis=-1, keepdims=True)
    acc = alpha * acc + _dot(p.astype(BF16), v_bf)
    return m_new, l, acc


def _masked_attend(q_bf, k_bf, v_bf, mask):
    s = jnp.where(mask, _dot_nt(q_bf, k_bf), NEG)
    m = jnp.max(s, axis=-1, keepdims=True)
    e = jnp.where(mask, jnp.exp(s - m), 0.0)
    l = jnp.sum(e, axis=-1, keepdims=True)
    return _dot((e / l).astype(BF16), v_bf)


SEL_KEYS = 512


def _prompt_block_rows(t_len):
    return max(-(-t_len // SEL_BLOCK), N_SEL)


def _nsa_prompt_kernel(q_ref, kc_ref, vc_ref, ks_ref, vs_ref, kw_ref, vw_ref, gate_ref, o_ref, sel_ref, s_ref,
                       kv_ref, *, t_len):
    qb = QUERY_BLOCK
    start = pl.program_id(2) * qb

    @pl.when(pl.program_id(2) == 0)
    def _():
        for i, ref in enumerate((ks_ref, vs_ref, kw_ref, vw_ref)):
            kv_ref[i] = ref[...].astype(BF16)

    pos = start + lax.broadcasted_iota(jnp.int32, (qb, 1), 0)
    pos_row = start + lax.broadcasted_iota(jnp.int32, (1, qb), 1)
    n_cmp = t_len // CMP_STRIDE - 1
    n_blk = max(-(-t_len // SEL_BLOCK), N_SEL)
    q = q_ref[...] * NSA_SCALE
    qh = [q[:, h * NSA_DK:(h + 1) * NSA_DK].astype(BF16) for h in range(NSA_HPG)]

    kc = kc_ref[...].astype(BF16)
    vc = vc_ref[...].astype(BF16)
    o_cmp = []
    pg = jnp.zeros((qb, kc.shape[0]), F32)
    for h in range(NSA_HPG):
        p = _cmp_probs(_dot_nt(qh[h], kc), pos, n_cmp)
        o_cmp.append(_dot(p.astype(BF16), vc))
        pg = pg + p
    sel_ref[...] = _select_blocks_t(_block_scores_t(pg, _prompt_block_rows(t_len)), pos_row, n_blk)

    w_rows = WINDOW + qb
    base = pl.multiple_of(jnp.maximum(start - WINDOW, 0), qb)
    kw = kv_ref[2, pl.ds(base, w_rows), :]
    vw = kv_ref[3, pl.ds(base, w_rows), :]
    dist = pos - (base + lax.broadcasted_iota(jnp.int32, (1, w_rows), 1))
    w_mask = (dist >= 0) & (dist <= WINDOW)
    o_win = [_masked_attend(qh[h], kw, vw, w_mask) for h in range(NSA_HPG)]

    n_steps = (start + qb + SEL_KEYS - 1) // SEL_KEYS
    blocks_per_step = SEL_KEYS // SEL_BLOCK
    expand = _block_expand(blocks_per_step, SEL_KEYS)

    def scores_step(j, m_run):
        k0 = pl.multiple_of(j * SEL_KEYS, SEL_KEYS)
        k_bf = kv_ref[0, pl.ds(k0, SEL_KEYS), :]
        key = k0 + lax.broadcasted_iota(jnp.int32, (1, SEL_KEYS), 1)
        chosen = sel_ref[pl.ds(pl.multiple_of(j * blocks_per_step, blocks_per_step), blocks_per_step), :]
        mask = (_dot_tn(chosen.astype(BF16), expand) > 0.5) & (key <= pos)
        out = []
        for h in range(NSA_HPG):
            s = jnp.where(mask, _dot_nt(qh[h], k_bf), NEG)
            s_ref[j, h] = s
            out.append(jnp.maximum(m_run[h], _lane_groups(s, jnp.maximum)))
        return tuple(out)

    m_run = lax.fori_loop(0, n_steps, scores_step, tuple(jnp.full((qb, LANES), NEG, F32) for _ in range(NSA_HPG)))
    m_fin = [jnp.max(m, axis=-1, keepdims=True) for m in m_run]

    def values_step(j, carry):
        k0 = pl.multiple_of(j * SEL_KEYS, SEL_KEYS)
        v_bf = kv_ref[1, pl.ds(k0, SEL_KEYS), :]
        out = []
        for h in range(NSA_HPG):
            l_run, acc = carry[h]
            p = jnp.exp(s_ref[j, h] - m_fin[h])
            out.append((l_run + _lane_groups(p, jnp.add), acc + _dot(p.astype(BF16), v_bf)))
        return tuple(out)

    sel_state = lax.fori_loop(0, n_steps, values_step,
                              tuple((jnp.zeros((qb, LANES), F32), jnp.zeros((qb, NSA_DK), F32))
                                    for _ in range(NSA_HPG)))

    gates = jax.nn.sigmoid(gate_ref[...])
    outs = []
    for h in range(NSA_HPG):
        l_run, acc = sel_state[h]
        o_sel = acc / jnp.sum(l_run, axis=-1, keepdims=True)
        outs.append(gates[:, 3 * h:3 * h + 1] * o_cmp[h] + gates[:, 3 * h + 1:3 * h + 2] * o_sel
                    + gates[:, 3 * h + 2:3 * h + 3] * o_win[h])
    o_ref[...] = jnp.concatenate(outs, axis=1).astype(o_ref.dtype)


def nsa_prompt(z, kcvc, b_sz, t_len):
    assert t_len % SEL_KEYS == 0 and t_len >= WINDOW + QUERY_BLOCK
    nqb = t_len // QUERY_BLOCK
    n_half = t_len // CMP_STRIDE
    hd = NSA_HPG * NSA_DK

    def rows(col0):
        return pl.BlockSpec((t_len, NSA_DK), lambda b, g, i: (b, col0 // NSA_DK + g))

    def cmp_spec(kv):
        return pl.BlockSpec((None, None, None, n_half, NSA_DK), lambda b, g, i: (kv, b, g, 0, 0))

    return pl.pallas_call(
        functools.partial(_nsa_prompt_kernel, t_len=t_len),
        out_shape=jax.ShapeDtypeStruct((b_sz * t_len, NSA_HEADS * NSA_DK), BF16),
        grid=(b_sz, NSA_GROUPS, nqb),
        in_specs=[pl.BlockSpec((QUERY_BLOCK, hd), lambda b, g, i: (b * nqb + i, g)),
                  cmp_spec(0), cmp_spec(1),
                  rows(C_KS), rows(C_VS), rows(C_KW), rows(C_VW),
                  pl.BlockSpec((QUERY_BLOCK, LANES), lambda b, g, i: (b * nqb + i, C_TAIL // LANES + g))],
        out_specs=pl.BlockSpec((QUERY_BLOCK, hd), lambda b, g, i: (b * nqb + i, g)),
        scratch_shapes=[pltpu.VMEM((_prompt_block_rows(t_len), QUERY_BLOCK), F32),
                        pltpu.VMEM((t_len // SEL_KEYS, NSA_HPG, QUERY_BLOCK, SEL_KEYS), F32),
                        pltpu.VMEM((4, t_len, NSA_DK), BF16)],
        compiler_params=_params(("parallel", "parallel", "arbitrary")),
        name="nsa_prompt",
    )(z, kcvc, kcvc, z, z, z, z, z)


S_PAD = 8


def _nsa_sample_kernel(pt_ref, q_ref, kc_ref, vc_ref, kn_ref, vn_ref, kw_ref, vw_ref, kwn_ref, vwn_ref, gate_ref, *refs,
                       pages, past, s_len, n_cmp, ncp, n_blk, blk_rows, win_rows):
    kpages, vpages = refs[:pages], refs[pages:2 * pages]
    o_ref = refs[2 * pages]
    sel_ref, m_ref, l_ref, acc_ref, oc_ref = refs[2 * pages + 1:]
    j = pl.program_id(1)
    n_j = pl.num_programs(1)
    rows = NSA_HPG * s_len
    step = lax.broadcasted_iota(jnp.int32, (rows, 1), 0) % s_len
    pos = past + step
    qg = [(q_ref[g] * NSA_SCALE).astype(BF16) for g in range(NSA_GROUPS)]

    @pl.when(j == 0)
    def _():
        step_i = lax.broadcasted_iota(jnp.int32, (LANES, rows), 1) % s_len
        lane_i = lax.broadcasted_iota(jnp.int32, (LANES, rows), 0)
        step_o = lax.broadcasted_iota(jnp.int32, (rows, LANES), 0) % s_len
        lane_o = lax.broadcasted_iota(jnp.int32, (rows, LANES), 1)
        pos_row = past + lax.broadcasted_iota(jnp.int32, (1, LANES), 1) % S_PAD
        pg = jnp.zeros((LANES, ncp), F32)
        for g in range(NSA_GROUPS):
            kc = kc_ref[pl.ds(g, ncp, stride=NSA_GROUPS), :].astype(BF16)
            vc = vc_ref[pl.ds(g, ncp, stride=NSA_GROUPS), :].astype(BF16)
            p = _cmp_probs(_dot_nt(qg[g], kc), pos, n_cmp)
            oc_ref[g] = _dot(p.astype(BF16), vc)
            fold = (step_i + g * S_PAD == lane_i).astype(BF16)
            pg = pg + _dot_split(fold, p)
            m_ref[g] = jnp.full((rows, 1), NEG, F32)
            l_ref[g] = jnp.zeros((rows, 1), F32)
            acc_ref[g] = jnp.zeros((rows, NSA_DK), F32)
        sel_t = _select_blocks_t(_block_scores_t(pg, blk_rows), pos_row, n_blk).astype(BF16)
        for g in range(NSA_GROUPS):
            unfold = (step_o + g * S_PAD == lane_o).astype(BF16)
            sel_ref[g] = _dot_nt(sel_t, unfold)

    n_keys = pages * PAGE_SIZE
    blocks_per_step = n_keys // SEL_BLOCK
    expand = _block_expand(blocks_per_step, n_keys)
    key = j * n_keys + lax.broadcasted_iota(jnp.int32, (1, n_keys), 1)
    blk0 = pl.multiple_of(j * blocks_per_step, blocks_per_step)
    for g in range(NSA_GROUPS):
        k_bf = jnp.concatenate([kpages[p][pl.ds(g, PAGE_SIZE, stride=NSA_GROUPS), :] for p in range(pages)],
                               axis=0).astype(BF16)
        v_bf = jnp.concatenate([vpages[p][pl.ds(g, PAGE_SIZE, stride=NSA_GROUPS), :] for p in range(pages)],
                               axis=0).astype(BF16)
        chosen = _dot_tn(sel_ref[g, pl.ds(blk0, blocks_per_step), :].astype(BF16), expand) > 0.5
        m, l, acc = _online_update((m_ref[g], l_ref[g], acc_ref[g]), _dot_nt(qg[g], k_bf),
                                   chosen & (key <= pos), v_bf)
        m_ref[g] = m
        l_ref[g] = l
        acc_ref[g] = acc

    @pl.when(j == n_j - 1)
    def _():
        gates = jax.nn.sigmoid(gate_ref[...])
        new_i = lax.broadcasted_iota(jnp.int32, (1, S_PAD), 1)
        new_key = past + new_i
        new_blk = past // SEL_BLOCK
        first_row = (lax.broadcasted_iota(jnp.int32, (8, S_PAD), 0) == 0).astype(BF16)
        dist = pos - (past - win_rows + lax.broadcasted_iota(jnp.int32, (1, win_rows), 1))
        w_mask = (dist >= 0) & (dist <= WINDOW)
        n_mask = (new_key <= pos) & (pos - new_key <= WINDOW) & (new_i < s_len)
        for g in range(NSA_GROUPS):
            chosen_new = _dot_tn(sel_ref[g, new_blk:new_blk + 8, :].astype(BF16), first_row) > 0.5
            mask = chosen_new & (new_key <= pos) & (new_i < s_len)
            _, l, acc = _online_update((m_ref[g], l_ref[g], acc_ref[g]),
                                       _dot_nt(qg[g], kn_ref[g].astype(BF16)), mask, vn_ref[g].astype(BF16))
            o_sel = acc / l
            kw = kw_ref[pl.ds(g, win_rows, stride=NSA_GROUPS), :].astype(BF16)
            vw = vw_ref[pl.ds(g, win_rows, stride=NSA_GROUPS), :].astype(BF16)
            s_old = jnp.where(w_mask, _dot_nt(qg[g], kw), NEG)
            s_new = jnp.where(n_mask, _dot_nt(qg[g], kwn_ref[g].astype(BF16)), NEG)
            m_w = jnp.maximum(jnp.max(s_old, axis=-1, keepdims=True), jnp.max(s_new, axis=-1, keepdims=True))
            e_old = jnp.where(w_mask, jnp.exp(s_old - m_w), 0.0)
            e_new = jnp.where(n_mask, jnp.exp(s_new - m_w), 0.0)
            l_w = jnp.sum(e_old, axis=-1, keepdims=True) + jnp.sum(e_new, axis=-1, keepdims=True)
            o_win = (_dot(e_old.astype(BF16), vw) + _dot(e_new.astype(BF16), vwn_ref[g].astype(BF16))) / l_w
            gt = gates[g]
            o_ref[g] = gt[:, 0:1] * oc_ref[g] + gt[:, 1:2] * o_sel + gt[:, 2:3] * o_win


def nsa_sample(q, kc, vc, k_new, v_new, kw_old, vw_old, kw_new, vw_new, gate, pool_k, pool_v, page_table, s_len,
               pages):
    bd, n_pages = page_table.shape
    past = n_pages * PAGE_SIZE
    rows = NSA_HPG * s_len
    n_cmp = past // CMP_STRIDE - 1
    n_blk = max(-(-(past + s_len) // SEL_BLOCK), N_SEL)
    assert s_len <= S_PAD and s_len < CMP_STRIDE and n_pages % pages == 0
    assert (past // SEL_BLOCK) % 8 == 0 and (pages * PAGE_SIZE // SEL_BLOCK) % 8 == 0
    blk_rows = -(-(past // SEL_BLOCK + 8) // LANES) * LANES
    ncp = kc.shape[1] // NSA_GROUPS
    win_rows = kw_old.shape[1] // NSA_GROUPS
    new_spec = pl.BlockSpec((None, NSA_GROUPS, S_PAD, NSA_DK), lambda b, j, pt: (b, 0, 0, 0))

    def per_b(shape):
        return pl.BlockSpec((None,) + shape, lambda b, j, pt: (b,) + (0,) * len(shape))

    def page_spec(i):
        return pl.BlockSpec((None, PAGE_SIZE * NSA_GROUPS, NSA_DK), lambda b, j, pt: (pt[b, j * pages + i], 0, 0))

    grid_spec = pltpu.PrefetchScalarGridSpec(
        num_scalar_prefetch=1,
        grid=(bd, n_pages // pages),
        in_specs=[per_b((NSA_GROUPS, rows, NSA_DK)), per_b((ncp * NSA_GROUPS, NSA_DK)), per_b((ncp * NSA_GROUPS, NSA_DK)),
                  new_spec, new_spec,
                  per_b((win_rows * NSA_GROUPS, NSA_DK)), per_b((win_rows * NSA_GROUPS, NSA_DK)),
                  new_spec, new_spec,
                  per_b((NSA_GROUPS, rows, LANES))]
        + [page_spec(i) for i in range(pages)] * 2,
        out_specs=per_b((NSA_GROUPS, rows, NSA_DK)),
        scratch_shapes=[pltpu.VMEM((NSA_GROUPS, blk_rows, rows), F32),
                        pltpu.VMEM((NSA_GROUPS, rows, 1), F32),
                        pltpu.VMEM((NSA_GROUPS, rows, 1), F32),
                        pltpu.VMEM((NSA_GROUPS, rows, NSA_DK), F32),
                        pltpu.VMEM((NSA_GROUPS, rows, NSA_DK), F32)],
    )
    return pl.pallas_call(
        functools.partial(_nsa_sample_kernel, pages=pages, past=past, s_len=s_len, n_cmp=n_cmp, ncp=ncp, n_blk=n_blk,
                          blk_rows=blk_rows, win_rows=win_rows),
        out_shape=jax.ShapeDtypeStruct((bd, NSA_GROUPS, rows, NSA_DK), F32),
        grid_spec=grid_spec,
        compiler_params=_params(("parallel", "arbitrary")),
        name="nsa_sample",
    )(page_table, q, kc, vc, k_new, v_new, kw_old, vw_old, kw_new, vw_new, gate,
      *([pool_k] * pages), *([pool_v] * pages))


GLA_HEADS_PER_STEP = 2


def _gla_kernel(q_ref, k_ref, v_ref, gr_ref, lr_ref, wlr_ref, blr_ref, ng_ref, s0_ref, o_ref, sT_out_ref,
                st_ref, *, chunk, t_valid):
    c = pl.program_id(2)
    row = lax.broadcasted_iota(jnp.int32, (chunk, 1), 0)
    tri = (lax.broadcasted_iota(jnp.int32, (chunk, chunk), 1)
           <= lax.broadcasted_iota(jnp.int32, (chunk, chunk), 0)).astype(BF16)
    lr = lr_ref[...].astype(BF16)

    @pl.when(c == 0)
    def _():
        for hh in range(GLA_HEADS_PER_STEP):
            st_ref[hh] = s0_ref[hh].T

    states = []
    for hh in range(GLA_HEADS_PER_STEP):
        dk = slice(hh * GLA_DK, (hh + 1) * GLA_DK)
        dv = slice(hh * GLA_DV, (hh + 1) * GLA_DV)
        x = _dot(lr, wlr_ref[hh]) + blr_ref[hh]
        log_a = jnp.where(row < t_valid, jax.nn.log_sigmoid(x) / GLA_TAU, 0.0)
        b = _dot_split(tri, log_a)
        qs = q_ref[:, dk] * (GLA_DK ** -0.5)
        k = k_ref[:, dk]
        a_t = jnp.zeros((chunk, LANES), F32)
        for t in range(min(chunk, t_valid)):
            n_s = -(-(t + 1) // 8) * 8
            seen = lax.broadcasted_iota(jnp.int32, (n_s, GLA_DK), 0) <= t
            w = jnp.where(seen, jnp.exp(b[t:t + 1] - b[0:n_s]), 0.0)
            col = jnp.sum(k[0:n_s] * w * qs[t:t + 1], axis=-1, keepdims=True)
            filled = jnp.where(lax.broadcasted_iota(jnp.int32, (n_s, LANES), 1) == t, col, a_t[0:n_s])
            a_t = filled if n_s == chunk else jnp.concatenate([filled, a_t[n_s:]], axis=0)
        v_bf = v_ref[:, dv].astype(BF16)
        st = st_ref[hh]
        o = (_dot_tn(a_t[:, 0:chunk].astype(BF16), v_bf)
             + _dot_nt((qs * jnp.exp(b)).astype(BF16), st.astype(BF16)))
        b_last = b[chunk - 1:chunk, :]
        k_dec = (k * jnp.exp(b_last - b)).astype(BF16)
        st_new = st * jnp.exp(b_last) + _dot_tn(v_bf, k_dec)
        st_ref[hh] = st_new
        states.append(st_new)
        gr = gr_ref[:, dv]
        o_ref[:, dv] = (_rms(o, ng_ref[...]) * (gr * jax.nn.sigmoid(gr))).astype(o_ref.dtype)

    @pl.when(c == pl.num_programs(2) - 1)
    def _():
        for hh in range(GLA_HEADS_PER_STEP):
            sT_out_ref[hh] = states[hh].T


def gla(z, wlr, blr, norm_g, s0, b_sz, t_len, chunk, t_valid):
    assert chunk <= LANES and chunk % 8 == 0 and t_len % chunk == 0 and GLA_HEADS % GLA_HEADS_PER_STEP == 0
    nck = t_len // chunk
    hp = GLA_HEADS_PER_STEP

    def seg(col0, width):
        assert col0 % (hp * width) == 0
        return pl.BlockSpec((chunk, hp * width), lambda b, h, c: (b * nck + c, col0 // (hp * width) + h))

    return pl.pallas_call(
        functools.partial(_gla_kernel, chunk=chunk, t_valid=t_valid),
        out_shape=(jax.ShapeDtypeStruct((b_sz * t_len, GLA_HEADS * GLA_DV), BF16),
                   jax.ShapeDtypeStruct((b_sz, GLA_HEADS, GLA_DK, GLA_DV), F32)),
        grid=(b_sz, GLA_HEADS // hp, nck),
        in_specs=[seg(C_GQ, GLA_DK), seg(C_GK, GLA_DK), seg(C_GV, GLA_DV), seg(C_GR, GLA_DV),
                  pl.BlockSpec((chunk, LANES), lambda b, h, c: (b * nck + c, C_TAIL // LANES)),
                  pl.BlockSpec((hp, LANES, GLA_DK), lambda b, h, c: (h, 0, 0)),
                  pl.BlockSpec((hp, 1, GLA_DK), lambda b, h, c: (h, 0, 0)),
                  pl.BlockSpec((1, GLA_DV), lambda b, h, c: (0, 0)),
                  pl.BlockSpec((None, hp, GLA_DK, GLA_DV), lambda b, h, c: (b, h, 0, 0))],
        out_specs=(pl.BlockSpec((chunk, hp * GLA_DV), lambda b, h, c: (b * nck + c, h)),
                   pl.BlockSpec((None, hp, GLA_DK, GLA_DV), lambda b, h, c: (b, h, 0, 0))),
        scratch_shapes=[pltpu.VMEM((hp, GLA_DV, GLA_DK), F32)],
        compiler_params=_params(("parallel", "parallel", "arbitrary")),
        name="gla",
    )(z, z, z, z, z, wlr, blr, norm_g.reshape(1, GLA_DV), s0)


def _merge_kernel(on_ref, og_ref, wn_ref, wg_ref, ma_ref, mb_ref, y_ref):
    y = (jax.nn.sigmoid(ma_ref[...]) * _dot(on_ref[...], wn_ref[...])
         + jax.nn.sigmoid(mb_ref[...]) * _dot(og_ref[...], wg_ref[...]))
    y_ref[...] = y.astype(y_ref.dtype)


def merge(o_nsa, o_gla, w_nsa, w_gla, z, tm, tn):
    n = o_nsa.shape[0]
    return pl.pallas_call(
        _merge_kernel,
        out_shape=jax.ShapeDtypeStruct((n, D_MODEL), BF16),
        grid=(n // tm, D_MODEL // tn),
        in_specs=[pl.BlockSpec((tm, o_nsa.shape[1]), lambda i, j: (i, 0)),
                  pl.BlockSpec((tm, o_gla.shape[1]), lambda i, j: (i, 0)),
                  pl.BlockSpec((w_nsa.shape[0], tn), lambda i, j: (0, j)),
                  pl.BlockSpec((w_gla.shape[0], tn), lambda i, j: (0, j)),
                  pl.BlockSpec((tm, tn), lambda i, j: (i, C_MA // tn + j)),
                  pl.BlockSpec((tm, tn), lambda i, j: (i, C_MB // tn + j))],
        out_specs=pl.BlockSpec((tm, tn), lambda i, j: (i, j)),
        compiler_params=_params(("parallel", "arbitrary")),
        name="merge",
    )(o_nsa, o_gla, w_nsa, w_gla, z, z)


def _out_proj_kernel(y_ref, w_ref, x_ref, o_ref):
    o_ref[...] = x_ref[...] + _dot(y_ref[...], w_ref[...])


def out_proj(y, w_out, x, tm, tn):
    n = y.shape[0]
    return pl.pallas_call(
        _out_proj_kernel,
        out_shape=jax.ShapeDtypeStruct((n, D_MODEL), F32),
        grid=(n // tm, D_MODEL // tn),
        in_specs=[pl.BlockSpec((tm, D_MODEL), lambda i, j: (i, 0)),
                  pl.BlockSpec((D_MODEL, tn), lambda i, j: (0, j)),
                  pl.BlockSpec((tm, tn), lambda i, j: (i, j))],
        out_specs=pl.BlockSpec((tm, tn), lambda i, j: (i, j)),
        compiler_params=_params(("parallel", "arbitrary")),
        name="out_proj",
    )(y, w_out, x)


def _top_values(s, count):
    vals = []
    for _ in range(count):
        m = jnp.max(s, axis=0, keepdims=True)
        s = jnp.where(s == m, REMOVED, s)
        vals.append(m)
    return vals


def _peer_scores_kernel(q_ref, k1_ref, k2_ref, ns1_ref, d2_ref, e1_ref, e2_ref):
    n_top = PEER_TOPK + 1
    pad_rows = [jnp.full((1, q_ref.shape[0]), REMOVED, F32)]
    for h in range(PEER_HEADS):
        q1 = q_ref[:, h * 2 * PEER_HALF:h * 2 * PEER_HALF + PEER_HALF].astype(BF16)
        q2 = q_ref[:, h * 2 * PEER_HALF + PEER_HALF:(h + 1) * 2 * PEER_HALF].astype(BF16)
        s1 = _dot_nt(k1_ref[h], q1)
        s2 = _dot_nt(k2_ref[h], q2)
        v1 = _top_values(s1, n_top)
        v2 = _top_values(s2, n_top)
        rows = [v1[i] + v2[j] for i in range(n_top) for j in range(n_top // (i + 1))]
        cand = jnp.concatenate(rows + pad_rows * (-len(rows) % 8), axis=0)
        top = _top_values(cand, n_top)
        z = sum(jnp.exp(t - top[0]) for t in top[:PEER_TOPK])
        ns1_ref[h] = -s1
        d2_ref[h] = s2 - 0.5 * (top[PEER_TOPK - 1] + top[PEER_TOPK])
        e1_ref[h] = jnp.exp(s1 - v1[0])
        e2_ref[h] = jnp.exp(s2 - v2[0]) / z


def peer_scores(qp, keys1, keys2, tn):
    n = qp.shape[0]
    big = jax.ShapeDtypeStruct((PEER_HEADS, PEER_NKEYS, n), F32)
    big_spec = pl.BlockSpec((PEER_HEADS, PEER_NKEYS, tn), lambda i: (0, 0, i))
    key_spec = pl.BlockSpec((PEER_HEADS, PEER_NKEYS, PEER_HALF), lambda i: (0, 0, 0))
    return pl.pallas_call(
        _peer_scores_kernel,
        out_shape=(big, big, big, big),
        grid=(n // tn,),
        in_specs=[pl.BlockSpec((tn, qp.shape[1]), lambda i: (i, 0)), key_spec, key_spec],
        out_specs=(big_spec, big_spec, big_spec, big_spec),
        compiler_params=_params(("parallel",)),
        name="peer_scores",
    )(qp, keys1, keys2)


PEER_I1_PER_TILE = 8


def _peer_dense_kernel(x_ref, g2_ref, gf_ref, u_ref, v_ref, ns1_ref, e1_ref, d2_ref, e2_ref, o_ref,
                       h_ref, acc_ref):
    e = pl.program_id(1)

    @pl.when(e == 0)
    def _():
        h_ref[...] = _rms(x_ref[...], g2_ref[...]).astype(BF16)
        acc_ref[...] = jnp.zeros_like(acc_ref)

    pre = _dot_nt(u_ref[...], h_ref[...])
    n_tok = pre.shape[1]
    tile = min(LANES, n_tok)
    parts = []
    for c in range(PEER_I1_PER_TILE):
        cols = []
        for t0 in range(0, n_tok, tile):
            tok = slice(t0, t0 + tile)
            w = None
            for h in range(PEER_HEADS):
                keep = d2_ref[h, :, tok] >= ns1_ref[h, c:c + 1, tok]
                gate = jnp.where(keep, e1_ref[h, c:c + 1, tok] * e2_ref[h, :, tok], 0.0)
                w = gate if w is None else w + gate
            cols.append(w * _gelu_tanh(pre[c * PEER_NKEYS:(c + 1) * PEER_NKEYS, tok]))
        parts.append(jnp.concatenate(cols, axis=1).astype(BF16))
    acc_ref[...] += _dot_tn(jnp.concatenate(parts, axis=0), v_ref[...])

    @pl.when(e == pl.num_programs(1) - 1)
    def _():
        o_ref[...] = _rms(x_ref[...] + acc_ref[...], gf_ref[...])


def peer_dense(x, g2, gf, u_bf, v_bf, ns1, d2, e1, e2, tn):
    n = x.shape[0]
    te = PEER_I1_PER_TILE * PEER_NKEYS
    n_exp = u_bf.shape[0]
    sub = pl.BlockSpec((PEER_HEADS, PEER_I1_PER_TILE, tn), lambda i, e: (0, e, i))
    full = pl.BlockSpec((PEER_HEADS, PEER_NKEYS, tn), lambda i, e: (0, 0, i))
    vec = pl.BlockSpec((1, D_MODEL), lambda i, e: (0, 0))
    return pl.pallas_call(
        _peer_dense_kernel,
        out_shape=jax.ShapeDtypeStruct((n, D_MODEL), F32),
        grid=(n // tn, n_exp // te),
        in_specs=[pl.BlockSpec((tn, D_MODEL), lambda i, e: (i, 0)), vec, vec,
                  pl.BlockSpec((te, D_MODEL), lambda i, e: (e, 0)),
                  pl.BlockSpec((te, D_MODEL), lambda i, e: (e, 0)),
                  sub, sub, full, full],
        out_specs=pl.BlockSpec((tn, D_MODEL), lambda i, e: (i, 0)),
        scratch_shapes=[pltpu.VMEM((tn, D_MODEL), BF16), pltpu.VMEM((tn, D_MODEL), F32)],
        compiler_params=_params(("parallel", "arbitrary")),
        name="peer_dense",
    )(x, g2.reshape(1, D_MODEL), gf.reshape(1, D_MODEL), u_bf, v_bf, ns1, e1, d2, e2)


PACK_ROWS = Z_COLS - C_TAIL


def _pack_w_in_kernel(ia_ref, ib_ref, kind_ref, a_ref, b_ref, o_ref, *, shifts, gate_row0, per_group, glr_row0):
    kind = kind_ref[pl.program_id(0)]
    for k, s in enumerate(shifts):
        @pl.when(kind == k)
        def _(s=s):
            src = a_ref[...] if s == 0 else jnp.concatenate([a_ref[s:, :], b_ref[:s, :]], axis=0)
            o_ref[...] = src.astype(BF16)

    @pl.when(kind == len(shifts))
    def _():
        out_row = lax.broadcasted_iota(jnp.int32, (PACK_ROWS, PACK_ROWS), 0)
        src_row = lax.broadcasted_iota(jnp.int32, (PACK_ROWS, PACK_ROWS), 1)
        group, r = out_row // LANES, out_row % LANES
        pick = (src_row == gate_row0 + group * per_group + r) & (r < per_group)
        out = _dot(pick.astype(BF16), a_ref[...].astype(BF16))
        pick = (src_row == glr_row0 + out_row - GLR_LANE) & (out_row >= GLR_LANE) & (out_row < GLR_LANE + GLA_RANK)
        out = out + _dot(pick.astype(BF16), b_ref[...].astype(BF16))
        o_ref[...] = out.astype(BF16)


def _pack_w_in(w_in):
    offs = np.concatenate([[0], np.cumsum(IN_SPLITS)]).astype(np.int64)
    plain = (0, 1, 2, 3, 4, 5, 6, 8, 9, 10, 11, 13, 14)
    src_start = np.concatenate([np.arange(offs[i], offs[i + 1], PACK_ROWS) for i in plain])
    assert all((offs[i + 1] - offs[i]) % PACK_ROWS == 0 for i in plain) and len(src_start) * PACK_ROWS == C_TAIL
    shifts = tuple(sorted(set(int(s) for s in src_start % PACK_ROWS)))
    gate_col, glr_col = int(offs[7]), int(offs[12])
    per_group = 3 * NSA_HPG
    assert gate_col % PACK_ROWS + NSA_GROUPS * per_group <= PACK_ROWS and PACK_ROWS == NSA_GROUPS * LANES
    ia = np.concatenate([src_start // PACK_ROWS, [gate_col // PACK_ROWS]])
    ib = np.concatenate([src_start // PACK_ROWS + 1, [glr_col // PACK_ROWS]])
    kind = np.concatenate([[shifts.index(int(s)) for s in src_start % PACK_ROWS], [len(shifts)]])
    assert all(s % 8 == 0 for s in shifts)
    d = w_in.shape[0]
    w_t = w_in.T
    grid_spec = pltpu.PrefetchScalarGridSpec(
        num_scalar_prefetch=3,
        grid=(Z_COLS // PACK_ROWS,),
        in_specs=[pl.BlockSpec((PACK_ROWS, d), lambda c, ia, ib, kind: (ia[c], 0)),
                  pl.BlockSpec((PACK_ROWS, d), lambda c, ia, ib, kind: (ib[c], 0))],
        out_specs=pl.BlockSpec((PACK_ROWS, d), lambda c, ia, ib, kind: (c, 0)),
    )
    return pl.pallas_call(
        functools.partial(_pack_w_in_kernel, shifts=shifts, gate_row0=gate_col % PACK_ROWS, per_group=per_group,
                          glr_row0=glr_col % PACK_ROWS),
        out_shape=jax.ShapeDtypeStruct((Z_COLS, d), BF16),
        grid_spec=grid_spec,
        compiler_params=_params(("arbitrary",)),
        name="pack_w_in",
    )(jnp.asarray(ia, jnp.int32), jnp.asarray(ib, jnp.int32), jnp.asarray(kind, jnp.int32), w_t, w_t)


def _pack_cmp(pe, w1, w2):
    w1cat = jnp.concatenate([w1[:CMP_STRIDE].reshape(CMP_STRIDE * NSA_DK, CMP_HIDDEN),
                             w1[CMP_STRIDE:].reshape(CMP_STRIDE * NSA_DK, CMP_HIDDEN)], axis=1).astype(BF16)
    return (w1cat, pe.reshape(1, CMP_LEN * NSA_DK).astype(BF16),
            w1.reshape(CMP_LEN * NSA_DK, CMP_HIDDEN).astype(BF16), w2.astype(BF16))


def _row_tile(n, cap):
    t = min(n, cap)
    while n % t:
        t //= 2
    return t


def _channel_tail(x2d, z, o_nsa, o_gla, wts):
    n = x2d.shape[0]
    tm = _row_tile(n, 1024)
    y = merge(o_nsa, o_gla, wts["w_nsa"], wts["w_gla"], z, tm, 512)
    x1 = out_proj(y, wts["w_out"], x2d, tm, 512)
    qp = norm_matmul(x1, wts["norm2_g"], wts["w_q"], tm, 512)
    tn = _row_tile(n, 256)
    ns1, d2, e1, e2 = peer_scores(qp, wts["keys1"], wts["keys2"], tn)
    return peer_dense(x1, wts["norm2_g"], wts["norm_f_g"], wts["u"], wts["v"], ns1, d2, e1, e2, _row_tile(n, 512))


def kernel(x_prompt, x_sample, cache_k_cmp, cache_v_cmp, cache_k_slc, cache_v_slc, cache_k_win, cache_v_win, state_gla, page_table, norm1_g, w_in, cmp_pe_k, cmp_w1_k, cmp_w2_k, cmp_pe_v, cmp_w1_v, cmp_w2_v, gla_w_lr2, gla_b_lr, gla_norm_g, w_nsa_proj, w_gla_proj, w_out, norm2_g, peer_w_q, peer_keys1, peer_keys2, peer_u, peer_v, norm_f_g):
    b_sz, t_len, _ = x_prompt.shape
    bd, s_len, _ = x_sample.shape
    n_pool = cache_k_cmp.shape[0]
    wb = cache_k_win.shape[1]

    w_pack = _pack_w_in(w_in)
    cmp_k = _pack_cmp(cmp_pe_k, cmp_w1_k, cmp_w2_k)
    cmp_v = _pack_cmp(cmp_pe_v, cmp_w1_v, cmp_w2_v)
    cmp_kv = [jnp.stack([a, b]) for a, b in zip(cmp_k, cmp_v)]
    wlr = jnp.zeros((LANES, GLA_HEADS * GLA_DK), F32).at[GLR_LANE:GLR_LANE + GLA_RANK].set(gla_w_lr2)
    wlr = wlr.reshape(LANES, GLA_HEADS, GLA_DK).transpose(1, 0, 2).astype(BF16)
    blr = gla_b_lr.reshape(GLA_HEADS, 1, GLA_DK)
    wts = dict(w_nsa=w_nsa_proj.astype(BF16), w_gla=w_gla_proj.astype(BF16), w_out=w_out.astype(BF16),
               w_q=peer_w_q.astype(BF16), norm2_g=norm2_g, norm_f_g=norm_f_g,
               keys1=peer_keys1.astype(BF16), keys2=peer_keys2.astype(BF16),
               u=peer_u.astype(BF16), v=peer_v.astype(BF16))

    n_p = b_sz * t_len
    xp = x_prompt.reshape(n_p, D_MODEL)
    z_p = norm_matmul(xp, norm1_g, w_pack, _row_tile(n_p, 1024), GD, w_rows_are_outputs=True)
    heads_p = split_heads(z_p, _row_tile(n_p, 1024))
    fs_p = half_proj_dense(z_p, cmp_kv[0], b_sz, t_len)
    kcvc_p = compress_finish(fs_p, cmp_kv[1], cmp_kv[2], cmp_kv[3], 1)
    o_nsa_p = nsa_prompt(z_p, kcvc_p, b_sz, t_len)
    s0 = jnp.zeros((b_sz, GLA_HEADS, GLA_DK, GLA_DV), F32)
    o_gla_p, gla_state_p = gla(z_p, wlr, blr, gla_norm_g, s0, b_sz, t_len, GLA_CHUNK, GLA_CHUNK)
    y_prompt = _channel_tail(xp, z_p, o_nsa_p, o_gla_p, wts).reshape(b_sz, t_len, D_MODEL)
    kcr_p, vcr_p, ksr_p, vsr_p, kwr_p, vwr_p = (a.reshape(b_sz, t_len, NSA_GROUPS, NSA_DK) for a in heads_p)
    wl = min(WINDOW, t_len)
    k_win_p = kwr_p[:, t_len - wl:]
    v_win_p = vwr_p[:, t_len - wl:]

    n_s = bd * s_len
    xs = x_sample.reshape(n_s, D_MODEL)
    z_s = norm_matmul(xs, norm1_g, w_pack, _row_tile(n_s, 1024), GD, w_rows_are_outputs=True)
    heads_s = split_heads(z_s, _row_tile(n_s, 1024))
    kcr_s, vcr_s, ksr_s, vsr_s, kwr_s, vwr_s = (a.reshape(bd, s_len, NSA_GROUPS, NSA_DK) for a in heads_s)

    pool2d = lambda p: p.reshape(n_pool, PAGE_SIZE * NSA_GROUPS, NSA_DK)
    def compress_pool(pool, prm):
        fs = half_proj_paged(pool2d(pool), page_table, prm[0], min(64, page_table.shape[1]))
        return compress_finish(fs[None, :, None], prm[1][None], prm[2][None], prm[3][None], NSA_GROUPS)[0, :, 0]

    kc_s = compress_pool(cache_k_cmp, cmp_k)
    vc_s = compress_pool(cache_v_cmp, cmp_v)

    assert wb >= s_len
    k_win_s = jnp.concatenate([cache_k_win[:, s_len:], kwr_s], axis=1)
    v_win_s = jnp.concatenate([cache_v_win[:, s_len:], vwr_s], axis=1)
    win_2d = lambda a: a.reshape(bd, wb * NSA_GROUPS, NSA_DK)

    def new_rows(a):
        return jnp.pad(a.transpose(0, 2, 1, 3), ((0, 0), (0, 0), (0, S_PAD - s_len), (0, 0)))

    rows = NSA_HPG * s_len
    q_s = z_s[:, C_Q:C_Q + NSA_HEADS * NSA_DK].reshape(bd, s_len, NSA_GROUPS, NSA_HPG, NSA_DK)
    q_s = q_s.transpose(0, 2, 3, 1, 4).reshape(bd, NSA_GROUPS, rows, NSA_DK)
    gate_s = z_s[:, C_TAIL:C_TAIL + 512].reshape(bd, s_len, NSA_GROUPS, LANES)[..., :3 * NSA_HPG]
    gate_s = gate_s.reshape(bd, s_len, NSA_GROUPS, NSA_HPG, 3).transpose(0, 2, 3, 1, 4).reshape(bd, NSA_GROUPS, rows, 3)
    gate_s = jnp.pad(gate_s, ((0, 0), (0, 0), (0, 0), (0, LANES - 3)))
    o_s = nsa_sample(q_s, kc_s, vc_s, new_rows(ksr_s), new_rows(vsr_s), win_2d(cache_k_win), win_2d(cache_v_win),
                     new_rows(kwr_s), new_rows(vwr_s), gate_s, pool2d(cache_k_slc), pool2d(cache_v_slc), page_table,
                     s_len, min(32, page_table.shape[1]))
    o_nsa_s = o_s.reshape(bd, NSA_GROUPS, NSA_HPG, s_len, NSA_DK).transpose(0, 3, 1, 2, 4)
    o_nsa_s = o_nsa_s.reshape(n_s, NSA_HEADS * NSA_DK).astype(BF16)

    z_s_pad = jnp.pad(z_s.reshape(bd, s_len, Z_COLS), ((0, 0), (0, S_PAD - s_len), (0, 0))).reshape(bd * S_PAD, Z_COLS)
    o_gla_s, gla_state_s = gla(z_s_pad, wlr, blr, gla_norm_g, state_gla, bd, S_PAD, S_PAD, s_len)
    o_gla_s = o_gla_s.reshape(bd, S_PAD, GLA_HEADS * GLA_DV)[:, :s_len].reshape(n_s, GLA_HEADS * GLA_DV)
    y_sample = _channel_tail(xs, z_s, o_nsa_s, o_gla_s, wts).reshape(bd, s_len, D_MODEL)

    return (y_prompt, y_sample, kcr_p, vcr_p, ksr_p, vsr_p, k_win_p, v_win_p, gla_state_p,
            kcr_s, vcr_s, ksr_s, vsr_s, k_win_s, v_win_s, gla_state_s)
```
